```python
import math
import jax, jax.numpy as jnp
from jax import lax
import numpy as np

D_MODEL = 1024
BATCH = 16
SEQ = 256
DEPTH = 4
DEC_BATCH = 8
DEC_SEQ = 2048
PAST_LEN = 256

GRID_W = 64
N_MIXERS = 4
NORM_EPS = 1e-6
D_FF = 2816
HY_ORDER = 2
HY_EMB = 33
HY_FILTER_W = 64
HY_SHORT_K = 3
HY_DECAY_TARGET = 1e-2
HY_FAST_PCT = 0.3
HY_SLOW_PCT = 1.5
CF_K = 31
SC_K = 3
MLA_HEADS = 8
MLA_Q_RANK = 384
MLA_KV_RANK = 256
MLA_NOPE = 128
MLA_ROPE = 64
MLA_V = 128
ROPE_THETA = 10000.0
Q_BLOCK = 128

kernel_name = 'hybrid_diffusion_trunk_step'


def _n_of(m):
    return (DEPTH - m + N_MIXERS - 1) // N_MIXERS


def rms_norm(x, g):
    x32 = x.astype(jnp.float32)
    y = x32 * lax.rsqrt(jnp.mean(x32 * x32, axis=-1, keepdims=True) + NORM_EPS)
    return y.astype(x.dtype) * g


def layer_norm(x, g, b):
    x32 = x.astype(jnp.float32)
    mu = jnp.mean(x32, axis=-1, keepdims=True)
    var = jnp.mean(jnp.square(x32 - mu), axis=-1, keepdims=True)
    return ((x32 - mu) * lax.rsqrt(var + NORM_EPS)).astype(x.dtype) * g + b


def dwconv(x, w, b=None):
    k = w.shape[0]
    y = lax.conv_general_dilated(x, w[:, None, :].astype(x.dtype), window_strides=(1,),
                                 padding=[(k // 2, k // 2)],
                                 dimension_numbers=('NWC', 'WIO', 'NWC'),
                                 feature_group_count=x.shape[-1])
    return y if b is None else y + b


def modulation(cond, w, b):
    m = jax.nn.silu(cond) @ w + b
    m = m.reshape(m.shape[0], 1, 6, D_MODEL)
    return tuple(m[:, :, k] for k in range(6))


def swiglu(h, wg, wu, wd):
    return (jax.nn.silu(h @ wg) * (h @ wu)) @ wd


def hyena_filters(seq_len, w1, b1, freq, w2, b2, w3):
    f32 = jnp.float32
    t = jnp.linspace(0.0, 1.0, seq_len, dtype=f32)[:, None]
    bands = (HY_EMB - 1) // 2
    w = 2.0 * math.pi * jnp.arange(seq_len, dtype=f32)[:, None] / seq_len
    f = jnp.linspace(1e-4, bands - 1, bands, dtype=f32)[None, :]
    z = jnp.concatenate([t, jnp.cos(f * w), -jnp.sin(f * w)], axis=-1).astype(w1.dtype)
    h = jnp.sin(freq[0] * (z @ w1 + b1))
    h = jnp.sin(freq[1] * (h @ w2 + b2))
    h = (h @ w3).reshape(seq_len, 2, HY_ORDER, D_MODEL)
    deltas = jnp.linspace(math.log(HY_DECAY_TARGET) / HY_SLOW_PCT,
                          math.log(HY_DECAY_TARGET) / HY_FAST_PCT, D_MODEL, dtype=f32)
    decay = jnp.exp(-t * jnp.abs(deltas))
    h = h.astype(f32) * decay[:, None, None, :]
    hf, hb = h[:, 0], h[:, 1]
    hc = jnp.concatenate([hf[:1] + hb[:1], hf[1:], jnp.zeros_like(hf[:1]), hb[:0:-1]], axis=0)
    return jnp.fft.rfft(hc, axis=0)


def long_conv(z, hfreq, skip):
    seq_len = z.shape[1]
    z32 = z.astype(jnp.float32)
    zf = jnp.fft.rfft(z32, n=2 * seq_len, axis=1)
    y = jnp.fft.irfft(zf * hfreq[None], n=2 * seq_len, axis=1)[:, :seq_len]
    return (y + z32 * skip.astype(jnp.float32)).astype(z.dtype)


def hyena(h, P, j):
    seq_len = h.shape[1]
    p = dwconv(h @ P['hy_w_in'][j] + P['hy_b_in'][j], P['hy_conv_w'][j], P['hy_conv_b'][j])
    v, x1, x2 = jnp.split(p, 3, axis=-1)
    hfreq = hyena_filters(seq_len, P['hy_f_w1'][j], P['hy_f_b1'][j], P['hy_f_freq'][j],
                          P['hy_f_w2'][j], P['hy_f_b2'][j], P['hy_f_w3'][j])
    skip = P['hy_skip'][j]
    zz = x1 * long_conv(v, hfreq[:, 0], skip[0])
    zz = x2 * long_conv(zz, hfreq[:, 1], skip[1])
    return zz @ P['hy_w_out'][j] + P['hy_b_out'][j]


def conformer_conv(h, P, j):
    a, g = jnp.split(h @ P['cf_w_pw1'][j] + P['cf_b_pw1'][j], 2, axis=-1)
    u = a * jax.nn.sigmoid(g)
    u = dwconv(u, P['cf_dw_w'][j], P['cf_dw_b'][j])
    u = jax.nn.silu(layer_norm(u, P['cf_ln_g'][j], P['cf_ln_b'][j]))
    return u @ P['cf_w_pw2'][j] + P['cf_b_pw2'][j]


def short_conv(h, P, j):
    bg, cg, hv = jnp.split(h @ P['sc_w_in'][j], 3, axis=-1)
    return (bg * dwconv(cg * hv, P['sc_conv_w'][j])) @ P['sc_w_out'][j]


def rope_2d_tables(seq_len):
    f32 = jnp.float32
    rows = seq_len // GRID_W
    row = jnp.repeat(jnp.arange(rows, dtype=f32), GRID_W)
    col = jnp.tile(jnp.arange(GRID_W, dtype=f32), rows)
    nf = MLA_ROPE // 4
    inv = jnp.exp(-math.log(ROPE_THETA) * jnp.arange(nf, dtype=f32) * (4.0 / MLA_ROPE))
    ang = jnp.stack([row[:, None] * inv, col[:, None] * inv], axis=1)
    return jnp.cos(ang), jnp.sin(ang)


def apply_rope(x, cos, sin):
    nf = MLA_ROPE // 4
    xr = x.reshape(*x.shape[:-1], 2, 2, nf)
    x1, x2 = xr[..., 0, :], xr[..., 1, :]
    cos = cos.astype(x.dtype)
    sin = sin.astype(x.dtype)
    return jnp.stack([x1 * cos - x2 * sin, x1 * sin + x2 * cos], axis=-2).reshape(x.shape)


def mla_project(h, P, j):
    bsz, seq_len = h.shape[:2]
    cq = rms_norm(h @ P['mla_w_dq'][j], P['mla_g_q'][j])
    q = (cq @ P['mla_w_uq'][j]).reshape(bsz, seq_len, MLA_HEADS, MLA_NOPE + MLA_ROPE)
    kv = h @ P['mla_w_dkv'][j]
    ckv = rms_norm(kv[..., :MLA_KV_RANK], P['mla_g_kv'][j])
    kpe = kv[..., MLA_KV_RANK:]
    return q[..., :MLA_NOPE], q[..., MLA_NOPE:], ckv, kpe


def block_attention(qn, qp, kn, kp, v):
    bsz, lq = qn.shape[:2]
    nb = lq // Q_BLOCK
    scale = (MLA_NOPE + MLA_ROPE) ** -0.5

    def blocks(a):
        return jnp.moveaxis(a.reshape(bsz, nb, Q_BLOCK, *a.shape[2:]), 1, 0)

    def one(args):
        qn_b, qp_b = args
        s = (jnp.einsum('bqhd,bkhd->bhqk', qn_b, kn)
             + jnp.einsum('bqhr,bkr->bhqk', qp_b, kp))
        p = jax.nn.softmax(s.astype(jnp.float32) * scale, axis=-1).astype(v.dtype)
        return jnp.einsum('bhqk,bkhd->bqhd', p, v)

    o = lax.map(one, (blocks(qn), blocks(qp)))
    return jnp.moveaxis(o, 0, 1).reshape(bsz, lq, MLA_HEADS, MLA_V)


def mla_attend(qn, qp, ckv, kpe, P, j):
    bsz, lq = qn.shape[:2]
    lk = ckv.shape[1]
    kv = (ckv @ P['mla_w_ukv'][j]).reshape(bsz, lk, MLA_HEADS, MLA_NOPE + MLA_V)
    o = block_attention(qn, qp, kv[..., :MLA_NOPE], kpe, kv[..., MLA_NOPE:])
    return o.reshape(bsz, lq, MLA_HEADS * MLA_V) @ P['mla_w_o'][j]


def run_trunk(x, cond, P, cache):
    seq_len = x.shape[1]
    rope = None if cache is None else rope_2d_tables(seq_len)
    new_ckv, new_kpe = [], []
    for i in range(DEPTH):
        m, j = i % N_MIXERS, i // N_MIXERS
        sh1, sc1, g1, sh2, sc2, g2 = modulation(cond, P['w_mod'][i], P['b_mod'][i])
        gn = P['norm_g'][i]
        h = rms_norm(x, gn[0]) * (1 + sc1) + sh1
        if m == 0:
            h = hyena(h, P, j)
        elif m == 1:
            h = conformer_conv(h, P, j)
        elif m == 2:
            h = short_conv(h, P, j)
        else:
            qn, qp, ckv, kpe = mla_project(h, P, j)
            if cache is None:
                new_ckv.append(ckv)
                new_kpe.append(kpe)
            else:
                cos, sin = rope
                qp = apply_rope(qp, cos[:, None], sin[:, None])
                kpe = apply_rope(kpe, cos, sin)
                ckv = jnp.concatenate([cache[0][:, j], ckv], axis=1)
                kpe = jnp.concatenate([cache[1][:, j], kpe], axis=1)
            h = mla_attend(qn, qp, ckv, kpe, P, j)
        x = x + g1 * rms_norm(h, gn[1])
        h = rms_norm(x, gn[2]) * (1 + sc2) + sh2
        h = swiglu(h, P['ffn_w_gate'][i], P['ffn_w_up'][i], P['ffn_w_down'][i])
        x = x + g2 * rms_norm(h, gn[3])
    return x, new_ckv, new_kpe


def setup_inputs(seed: int = 0) -> dict:
    key = jax.random.key(seed)
    ks = iter(jax.random.split(key, 64))
    D = D_MODEL
    nA, nB, nC, nD = _n_of(0), _n_of(1), _n_of(2), _n_of(3)
    HQK = MLA_HEADS * (MLA_NOPE + MLA_ROPE)

    def nrm(shape, scale):
        return scale * jax.random.normal(next(ks), shape, jnp.float32)

    def gain(shape):
        return 1.0 + nrm(shape, 0.01)

    return {
        'x_prompt': nrm((BATCH, SEQ, D), 1.0),
        'x_sample': nrm((DEC_BATCH, DEC_SEQ, D), 1.0),
        'c': nrm((DEC_BATCH, D), 1.0),
        'cache_ckv': nrm((DEC_BATCH, nD, PAST_LEN, MLA_KV_RANK), 1.0),
        'cache_kpe': nrm((DEC_BATCH, nD, PAST_LEN, MLA_ROPE), 1.0),
        'c_ctx': nrm((D,), 1.0),
        'w_mod': nrm((DEPTH, D, 6 * D), 0.5 * D ** -0.5),
        'b_mod': nrm((DEPTH, 6 * D), 0.01),
        'norm_g': gain((DEPTH, 4, D)),
        'ffn_w_gate': nrm((DEPTH, D, D_FF), D ** -0.5),
        'ffn_w_up': nrm((DEPTH, D, D_FF), D ** -0.5),
        'ffn_w_down': nrm((DEPTH, D_FF, D), D_FF ** -0.5),
        'hy_w_in': nrm((nA, D, 3 * D), D ** -0.5),
        'hy_b_in': nrm((nA, 3 * D), 0.01),
        'hy_conv_w': nrm((nA, HY_SHORT_K, 3 * D), HY_SHORT_K ** -0.5),
        'hy_conv_b': nrm((nA, 3 * D), 0.01),
        'hy_f_w1': nrm((nA, HY_EMB, HY_FILTER_W), HY_EMB ** -0.5),
        'hy_f_b1': nrm((nA, HY_FILTER_W), 0.01),
        'hy_f_freq': gain((nA, 2, HY_FILTER_W)),
        'hy_f_w2': nrm((nA, HY_FILTER_W, HY_FILTER_W), HY_FILTER_W ** -0.5),
        'hy_f_b2': nrm((nA, HY_FILTER_W), 0.01),
        'hy_f_w3': nrm((nA, HY_FILTER_W, 2 * HY_ORDER * D), 0.05 * HY_FILTER_W ** -0.5),
        'hy_skip': nrm((nA, HY_ORDER, D), 0.1),
        'hy_w_out': nrm((nA, D, D), D ** -0.5),
        'hy_b_out': nrm((nA, D), 0.01),
        'cf_w_pw1': nrm((nB, D, 2 * D), D ** -0.5),
        'cf_b_pw1': nrm((nB, 2 * D), 0.01),
        'cf_dw_w': nrm((nB, CF_K, D), CF_K ** -0.5),
        'cf_dw_b': nrm((nB, D), 0.01),
        'cf_ln_g': gain((nB, D)),
        'cf_ln_b': nrm((nB, D), 0.01),
        'cf_w_pw2': nrm((nB, D, D), D ** -0.5),
        'cf_b_pw2': nrm((nB, D), 0.01),
        'sc_w_in': nrm((nC, D, 3 * D), D ** -0.5),
        'sc_conv_w': nrm((nC, SC_K, D), SC_K ** -0.5),
        'sc_w_out': nrm((nC, D, D), D ** -0.5),
        'mla_w_dq': nrm((nD, D, MLA_Q_RANK), D ** -0.5),
        'mla_g_q': gain((nD, MLA_Q_RANK)),
        'mla_w_uq': nrm((nD, MLA_Q_RANK, HQK), MLA_Q_RANK ** -0.5),
        'mla_w_dkv': nrm((nD, D, MLA_KV_RANK + MLA_ROPE), D ** -0.5),
        'mla_g_kv': gain((nD, MLA_KV_RANK)),
        'mla_w_ukv': nrm((nD, MLA_KV_RANK, MLA_HEADS * (MLA_NOPE + MLA_V)), MLA_KV_RANK ** -0.5),
        'mla_w_o': nrm((nD, MLA_HEADS * MLA_V, D), (MLA_HEADS * MLA_V) ** -0.5),
    }


def reference(x_prompt, x_sample, c, cache_ckv, cache_kpe, c_ctx, w_mod, b_mod, norm_g,
              ffn_w_gate, ffn_w_up, ffn_w_down,
              hy_w_in, hy_b_in, hy_conv_w, hy_conv_b, hy_f_w1, hy_f_b1, hy_f_freq,
              hy_f_w2, hy_f_b2, hy_f_w3, hy_skip, hy_w_out, hy_b_out,
              cf_w_pw1, cf_b_pw1, cf_dw_w, cf_dw_b, cf_ln_g, cf_ln_b, cf_w_pw2, cf_b_pw2,
              sc_w_in, sc_conv_w, sc_w_out,
              mla_w_dq, mla_g_q, mla_w_uq, mla_w_dkv, mla_g_kv, mla_w_ukv, mla_w_o):
    P = dict(w_mod=w_mod, b_mod=b_mod, norm_g=norm_g,
             ffn_w_gate=ffn_w_gate, ffn_w_up=ffn_w_up, ffn_w_down=ffn_w_down,
             hy_w_in=hy_w_in, hy_b_in=hy_b_in, hy_conv_w=hy_conv_w, hy_conv_b=hy_conv_b,
             hy_f_w1=hy_f_w1, hy_f_b1=hy_f_b1, hy_f_freq=hy_f_freq, hy_f_w2=hy_f_w2,
             hy_f_b2=hy_f_b2, hy_f_w3=hy_f_w3, hy_skip=hy_skip, hy_w_out=hy_w_out,
             hy_b_out=hy_b_out,
             cf_w_pw1=cf_w_pw1, cf_b_pw1=cf_b_pw1, cf_dw_w=cf_dw_w, cf_dw_b=cf_dw_b,
             cf_ln_g=cf_ln_g, cf_ln_b=cf_ln_b, cf_w_pw2=cf_w_pw2, cf_b_pw2=cf_b_pw2,
             sc_w_in=sc_w_in, sc_conv_w=sc_conv_w, sc_w_out=sc_w_out,
             mla_w_dq=mla_w_dq, mla_g_q=mla_g_q, mla_w_uq=mla_w_uq, mla_w_dkv=mla_w_dkv,
             mla_g_kv=mla_g_kv, mla_w_ukv=mla_w_ukv, mla_w_o=mla_w_o)
    y_prompt, ckv_list, kpe_list = run_trunk(x_prompt, c_ctx[None, :], P, None)
    new_ckv = jnp.stack(ckv_list, axis=1)
    new_kpe = jnp.stack(kpe_list, axis=1)
    y_sample, _, _ = run_trunk(x_sample, c, P, (cache_ckv, cache_kpe))
    return (y_prompt, y_sample, new_ckv, new_kpe)
```

```python
import functools
import math

import jax
import jax.numpy as jnp
from jax import lax
from jax.experimental import pallas as pl
from jax.experimental.pallas import tpu as pltpu

F32 = jnp.float32
BF16 = jnp.bfloat16

NORM_EPS = 1e-6
GRID_W = 64
ROPE_THETA = 10000.0
HY_DECAY_TARGET = 1e-2
HY_FAST_PCT = 0.3
HY_SLOW_PCT = 1.5

V7X_VMEM_LIMIT_BYTES = 60000 * 1024
LANE = 128
SUBLANES = 8

TM = 512
DC = 256


def _dot(a, b):
    return jnp.dot(a, b, preferred_element_type=F32)


def _dot_hi(a, b):
    return jnp.dot(a, b, preferred_element_type=F32, precision=lax.Precision.HIGHEST)


def _rms(x):
    return x * lax.rsqrt(jnp.mean(x * x, axis=-1, keepdims=True) + NORM_EPS)


def _sigmoid(x):
    return 1.0 / (1.0 + jnp.exp(-x))


def _silu(x):
    return x * _sigmoid(x)


def _params(semantics):
    return pltpu.CompilerParams(dimension_semantics=semantics,
                                vmem_limit_bytes=V7X_VMEM_LIMIT_BYTES)


def _resident(shape):
    zeros = (0,) * len(shape)
    return pl.BlockSpec(shape, lambda *_: zeros, pipeline_mode=pl.Buffered(1))


class Geometry:
    def __init__(self, nb_ctx, l_ctx, nb_lat, l_lat, d):
        self.nb_ctx, self.l_ctx, self.nb_lat, self.l_lat, self.d = nb_ctx, l_ctx, nb_lat, l_lat, d
        self.n_ctx = nb_ctx * l_ctx
        self.n_lat = nb_lat * l_lat
        self.n = self.n_ctx + self.n_lat
        assert self.n_ctx % TM == 0 and l_lat % TM == 0 and self.n_ctx % l_lat == 0

    def cond_of_tile(self, i):
        row = i * TM
        return jnp.where(row < self.n_ctx, 0, 1 + (row - self.n_ctx) // self.l_lat)


def _mod_kernel(c_ref, w_ref, b_ref, o_ref):
    a = _silu(c_ref[...]).astype(BF16)
    o_ref[...] = _dot(a, w_ref[...].astype(BF16)) + b_ref[...]


def modulation_all(cond, w_mod, b_mod):
    depth, d, d6 = w_mod.shape
    r = cond.shape[0]
    tn = 1536
    return pl.pallas_call(
        _mod_kernel,
        out_shape=jax.ShapeDtypeStruct((depth, r, d6), F32),
        grid=(depth, d6 // tn),
        in_specs=[pl.BlockSpec((r, d), lambda l, j: (0, 0)),
                  pl.BlockSpec((None, d, tn), lambda l, j: (l, 0, j)),
                  pl.BlockSpec((None, 1, tn), lambda l, j: (l, 0, j))],
        out_specs=pl.BlockSpec((None, r, tn), lambda l, j: (l, 0, j)),
        compiler_params=_params(("parallel", "parallel")),
        name="modulation",
    )(cond, w_mod, b_mod.reshape(depth, 1, d6))


def _prenorm(x_ref, mod_ref, gn_ref, which):
    x = x_ref[...]
    g = gn_ref[2 * which:2 * which + 1, :]
    sh = mod_ref[3 * which:3 * which + 1, :]
    sc = mod_ref[3 * which + 1:3 * which + 2, :]
    return ((_rms(x) * g) * (1.0 + sc) + sh).astype(BF16)


def _row_specs(geo):
    d = geo.d
    return [pl.BlockSpec((TM, d), lambda i: (i, 0)),
            pl.BlockSpec((None, 6, d), lambda i: (geo.cond_of_tile(i), 0, 0)),
            pl.BlockSpec((4, d), lambda i: (0, 0))]


def _hy_proj_kernel(x_ref, mod_ref, gn_ref, w_ref, b_ref, o_ref):
    h = _prenorm(x_ref, mod_ref, gn_ref, 0)
    step = 512
    for c in range(0, o_ref.shape[1], step):
        o_ref[:, c:c + step] = (_dot(h, w_ref[:, c:c + step]) + b_ref[:, c:c + step]).astype(BF16)


def hyena_in_proj(geo, x, mod, gn, w_in, b_in):
    d, d3 = w_in.shape
    return pl.pallas_call(
        _hy_proj_kernel,
        out_shape=jax.ShapeDtypeStruct((geo.n, d3), BF16),
        grid=(geo.n // TM,),
        in_specs=_row_specs(geo) + [_resident((d, d3)), _resident((1, d3))],
        out_specs=pl.BlockSpec((TM, d3), lambda i: (i, 0)),
        compiler_params=_params(("parallel",)),
        name="hyena_in_proj",
    )(x, mod, gn, w_in, b_in.reshape(1, d3))


def _cf_proj_kernel(x_ref, mod_ref, gn_ref, w_ref, b_ref, o_ref):
    h = _prenorm(x_ref, mod_ref, gn_ref, 0)
    d = o_ref.shape[1]
    step = 512
    for c in range(0, d, step):
        a = _dot(h, w_ref[:, c:c + step]) + b_ref[:, c:c + step]
        g = _dot(h, w_ref[:, d + c:d + c + step]) + b_ref[:, d + c:d + c + step]
        o_ref[:, c:c + step] = (a * _sigmoid(g)).astype(BF16)


def conformer_in_proj(geo, x, mod, gn, w_pw1, b_pw1):
    d, d2 = w_pw1.shape
    return pl.pallas_call(
        _cf_proj_kernel,
        out_shape=jax.ShapeDtypeStruct((geo.n, d), BF16),
        grid=(geo.n // TM,),
        in_specs=_row_specs(geo) + [_resident((d, d2)), _resident((1, d2))],
        out_specs=pl.BlockSpec((TM, d), lambda i: (i, 0)),
        compiler_params=_params(("parallel",)),
        name="conformer_in_proj",
    )(x, mod, gn, w_pw1, b_pw1.reshape(1, d2))


def _sc_proj_kernel(x_ref, mod_ref, gn_ref, w_ref, bg_ref, q_ref):
    h = _prenorm(x_ref, mod_ref, gn_ref, 0)
    d = bg_ref.shape[1]
    step = 512
    for c in range(0, d, step):
        bg_ref[:, c:c + step] = _dot(h, w_ref[:, c:c + step]).astype(BF16)
        cg = _dot(h, w_ref[:, d + c:d + c + step])
        hv = _dot(h, w_ref[:, 2 * d + c:2 * d + c + step])
        q_ref[:, c:c + step] = (cg * hv).astype(BF16)


def shortconv_in_proj(geo, x, mod, gn, w_in):
    d, d3 = w_in.shape
    out = jax.ShapeDtypeStruct((geo.n, d), BF16)
    return pl.pallas_call(
        _sc_proj_kernel,
        out_shape=(out, out),
        grid=(geo.n // TM,),
        in_specs=_row_specs(geo) + [_resident((d, d3))],
        out_specs=(pl.BlockSpec((TM, d), lambda i: (i, 0)), pl.BlockSpec((TM, d), lambda i: (i, 0))),
        compiler_params=_params(("parallel",)),
        name="shortconv_in_proj",
    )(x, mod, gn, w_in)


def _out_proj_kernel(a_ref, x_ref, mod_ref, gn_ref, w_ref, b_ref, o_ref):
    y = _dot(a_ref[...], w_ref[...]) + b_ref[...]
    o_ref[...] = x_ref[...] + mod_ref[2:3, :] * (_rms(y) * gn_ref[1:2, :])


def _out_proj_ln_kernel(a_ref, x_ref, mod_ref, gn_ref, w_ref, b_ref, lg_ref, lb_ref, o_ref):
    u = a_ref[...].astype(F32)
    mu = jnp.mean(u, axis=-1, keepdims=True)
    uc = u - mu
    var = jnp.mean(uc * uc, axis=-1, keepdims=True)
    v = (uc * lax.rsqrt(var + NORM_EPS)) * lg_ref[...] + lb_ref[...]
    y = _dot(_silu(v).astype(BF16), w_ref[...]) + b_ref[...]
    o_ref[...] = x_ref[...] + mod_ref[2:3, :] * (_rms(y) * gn_ref[1:2, :])


def out_proj_residual(geo, a, x, mod, gn, w, b, ln=None):
    d = geo.d
    a_spec = pl.BlockSpec((TM, d), lambda i: (i, 0))
    specs = [a_spec] + _row_specs(geo) + [_resident((d, d)), _resident((1, d))]
    args = [a, x, mod, gn, w, b.reshape(1, d)]
    body = _out_proj_kernel
    if ln is not None:
        specs += [_resident((1, d)), _resident((1, d))]
        args += [ln[0].reshape(1, d), ln[1].reshape(1, d)]
        body = _out_proj_ln_kernel
    return pl.pallas_call(
        body,
        out_shape=jax.ShapeDtypeStruct((geo.n, d), F32),
        grid=(geo.n // TM,),
        in_specs=specs,
        out_specs=pl.BlockSpec((TM, d), lambda i: (i, 0)),
        compiler_params=_params(("parallel",)),
        name="out_proj_residual" if ln is None else "out_proj_ln_residual",
    )(*args)


def _ffn_kernel(x_ref, mod_ref, gn_ref, wg_ref, wu_ref, wd_ref, o_ref, a_ref):
    h = _prenorm(x_ref, mod_ref, gn_ref, 1)
    step = 256
    for c in range(0, a_ref.shape[1], step):
        g = _dot(h, wg_ref[:, c:c + step])
        u = _dot(h, wu_ref[:, c:c + step])
        a_ref[:, c:c + step] = (_silu(g) * u).astype(BF16)
    y = _dot(a_ref[...], wd_ref[...])
    o_ref[...] = x_ref[...] + mod_ref[5:6, :] * (_rms(y) * gn_ref[3:4, :])


def ffn_residual(geo, x, mod, gn, wg, wu, wd):
    d, dff = wg.shape
    return pl.pallas_call(
        _ffn_kernel,
        out_shape=jax.ShapeDtypeStruct((geo.n, d), F32),
        grid=(geo.n // TM,),
        in_specs=_row_specs(geo) + [_resident((d, dff)), _resident((d, dff)), _resident((dff, d))],
        out_specs=pl.BlockSpec((TM, d), lambda i: (i, 0)),
        scratch_shapes=[pltpu.VMEM((TM, dff), BF16)],
        compiler_params=_params(("parallel",)),
        name="ffn_residual",
    )(x, mod, gn, wg, wu, wd)


def dft_tables(seq_len):
    k = jnp.arange(seq_len, dtype=jnp.int32)
    idx = (k[:, None] * k[None, :]) % (2 * seq_len)
    ang = idx.astype(F32) * (math.pi / seq_len)
    sign = (1 - 2 * (k % 2)).astype(F32)[:, None]
    cvec = jnp.where(k == 0, 0.5 / seq_len, 1.0 / seq_len).astype(F32)[:, None]
    return jnp.cos(ang).astype(BF16), jnp.sin(ang).astype(BF16), sign, cvec


def _filter_kernel(z_ref, w1_ref, b1_ref, fr_ref, w2_ref, b2_ref, w3f_ref, w3b_ref, t_ref, dl_ref,
                   sign_ref, cvec_ref, c_ref, s_ref, hr_ref, hi_ref, hn_ref):
    h = jnp.sin(fr_ref[0:1, :] * (_dot_hi(z_ref[...], w1_ref[...]) + b1_ref[...]))
    h = jnp.sin(fr_ref[1:2, :] * (_dot_hi(h, w2_ref[...]) + b2_ref[...]))
    decay = jnp.exp(-t_ref[...] * jnp.abs(dl_ref[...]))
    hf = _dot_hi(h, w3f_ref[...]) * decay
    hb = _dot_hi(h, w3b_ref[...]) * decay
    hs = hf + hb
    hd = hf - hb
    cv = cvec_ref[...]
    hr_ref[...] = cv * _dot(c_ref[...], hs.astype(BF16))
    hi_ref[...] = cv * _dot(s_ref[...], hd.astype(BF16))
    seq_len = z_ref.shape[0]
    hn_ref[...] = jnp.sum(hs * sign_ref[...], axis=0, keepdims=True) * (0.5 / seq_len)


def hyena_filter_spectrum(seq_len, d, tables, w1, b1, freq, w2, b2, w3):
    cmat, smat, sign, cvec = tables
    emb, fw = w1.shape
    bands = (emb - 1) // 2
    t = jnp.linspace(0.0, 1.0, seq_len, dtype=F32)[:, None]
    w = 2.0 * math.pi * jnp.arange(seq_len, dtype=F32)[:, None] / seq_len
    f = jnp.linspace(1e-4, bands - 1, bands, dtype=F32)[None, :]
    z = jnp.concatenate([t, jnp.cos(f * w), -jnp.sin(f * w)], axis=-1)
    z = jnp.pad(z, ((0, 0), (0, LANE - emb)))
    w1p = jnp.pad(w1, ((0, LANE - emb), (0, 0)))
    deltas = jnp.linspace(math.log(HY_DECAY_TARGET) / HY_SLOW_PCT,
                          math.log(HY_DECAY_TARGET) / HY_FAST_PCT, d, dtype=F32)[None, :]
    nd = d // DC
    full = lambda shape: pl.BlockSpec(shape, lambda j: (0,) * len(shape))
    spec_out = pl.BlockSpec((seq_len, DC), lambda j: (0, j))
    return pl.pallas_call(
        _filter_kernel,
        out_shape=(jax.ShapeDtypeStruct((seq_len, 2 * d), F32),
                   jax.ShapeDtypeStruct((seq_len, 2 * d), F32),
                   jax.ShapeDtypeStruct((1, 2 * d), F32)),
        grid=(2 * nd,),
        in_specs=[full((seq_len, LANE)), full((LANE, fw)), full((1, fw)), full((2, fw)),
                  full((fw, fw)), full((1, fw)),
                  pl.BlockSpec((fw, DC), lambda j: (0, j)),
                  pl.BlockSpec((fw, DC), lambda j: (0, j + 2 * nd)),
                  full((seq_len, 1)),
                  pl.BlockSpec((1, DC), lambda j: (0, j % nd)),
                  full((seq_len, 1)), full((seq_len, 1)),
                  _resident((seq_len, seq_len)), _resident((seq_len, seq_len))],
        out_specs=(spec_out, spec_out, pl.BlockSpec((1, DC), lambda j: (0, j))),
        compiler_params=_params(("parallel",)),
        name=f"hyena_filter_{seq_len}",
    )(z, w1p, b1.reshape(1, fw), freq, w2, b2.reshape(1, fw), w3, w3, t, deltas, sign, cvec, cmat, smat)


def _dwconv3(p_ref, w_ref, b_ref, first, last):
    p = p_ref[...].astype(F32)
    n = p.shape[0]
    prev = jnp.where(first, 0.0, pltpu.roll(p, 1, 0))
    nxt = jnp.where(last, 0.0, pltpu.roll(p, n - 1, 0))
    return w_ref[0:1, :] * prev + w_ref[1:2, :] * p + w_ref[2:3, :] * nxt + b_ref[...]


def _long_conv(z, cmat, smat, sign, hr, hi, hn, skip):
    zb = z.astype(BF16)
    a = _dot(cmat, zb)
    b = _dot(smat, zb)
    nyq = jnp.sum(z * sign, axis=0, keepdims=True)
    yr = (a * hr - b * hi).astype(BF16)
    yi = (a * hi + b * hr).astype(BF16)
    y = _dot(cmat, yr) + _dot(smat, yi) + sign * (nyq * hn)
    return y + z * skip


def _hyena_mix_kernel(pv_ref, p1_ref, p2_ref, wv_ref, w1_ref, w2_ref, bv_ref, b1_ref, b2_ref,
                      sign_ref, c_ref, s_ref, hr1_ref, hi1_ref, hn1_ref, hr2_ref, hi2_ref, hn2_ref,
                      skip_ref, o_ref):
    n = pv_ref.shape[0]
    rows = lax.broadcasted_iota(jnp.int32, (n, 1), 0)
    first, last = rows == 0, rows == n - 1
    v = _dwconv3(pv_ref, wv_ref, bv_ref, first, last)
    x1 = _dwconv3(p1_ref, w1_ref, b1_ref, first, last)
    sign = sign_ref[...]
    cmat, smat = c_ref[...], s_ref[...]
    zz = x1 * _long_conv(v, cmat, smat, sign, hr1_ref[...], hi1_ref[...], hn1_ref[...], skip_ref[0:1, :])
    x2 = _dwconv3(p2_ref, w2_ref, b2_ref, first, last)
    zz = x2 * _long_conv(zz, cmat, smat, sign, hr2_ref[...], hi2_ref[...], hn2_ref[...], skip_ref[1:2, :])
    o_ref[...] = zz.astype(BF16)


def hyena_mix(geo, p, conv_w, conv_b, skip, tables, spectrum, seq_len, row0, nb):
    d = geo.d
    cmat, smat, sign, _ = tables
    hr, hi, hn = spectrum
    dc = DC if seq_len > 512 else 512
    nd = d // dc
    blk0 = row0 // seq_len
    pspec = lambda part: pl.BlockSpec((seq_len, dc), lambda c, b: (blk0 + b, part * nd + c))
    wspec = lambda part: pl.BlockSpec((3, dc), lambda c, b: (0, part * nd + c))
    bspec = lambda part: pl.BlockSpec((1, dc), lambda c, b: (0, part * nd + c))
    hspec = lambda order: pl.BlockSpec((seq_len, dc), lambda c, b: (0, order * nd + c),
                                       pipeline_mode=pl.Buffered(1))
    nspec = lambda order: pl.BlockSpec((1, dc), lambda c, b: (0, order * nd + c))
    return pl.pallas_call(
        _hyena_mix_kernel,
        out_shape=jax.ShapeDtypeStruct((nb * seq_len, d), BF16),
        grid=(nd, nb),
        in_specs=[pspec(0), pspec(1), pspec(2), wspec(0), wspec(1), wspec(2), bspec(0), bspec(1), bspec(2),
                  _resident((seq_len, 1)), _resident((seq_len, seq_len)), _resident((seq_len, seq_len)),
                  hspec(0), hspec(0), nspec(0), hspec(1), hspec(1), nspec(1),
                  pl.BlockSpec((2, dc), lambda c, b: (0, c))],
        out_specs=pl.BlockSpec((seq_len, dc), lambda c, b: (b, c)),
        compiler_params=_params(("arbitrary", "arbitrary")),
        name=f"hyena_mix_{seq_len}",
    )(p, p, p, conv_w, conv_w, conv_w, conv_b, conv_b, conv_b, sign, cmat, smat,
      hr, hi, hn, hr, hi, hn, skip)


CONV_ROWS = 64
CONV_PAD = 16


def _stage_padded(src, pad_ref):
    n = src.shape[0]
    zeros = jnp.zeros((CONV_PAD, pad_ref.shape[1]), F32)
    pad_ref[0:CONV_PAD, :] = zeros
    pad_ref[CONV_PAD + n:2 * CONV_PAD + n, :] = zeros
    pad_ref[CONV_PAD:CONV_PAD + n, :] = src


def _dwconv_chunks(pad_ref, w_ref, n, ktaps, emit):
    half = ktaps // 2
    lo = (CONV_PAD - half) // SUBLANES * SUBLANES
    hi = -(-(CONV_PAD + half + CONV_ROWS) // SUBLANES) * SUBLANES
    rows = hi - lo

    def chunk(ci, carry):
        r0 = pl.multiple_of(ci * CONV_ROWS, CONV_ROWS)
        window = pad_ref[pl.ds(pl.multiple_of(r0 + lo, SUBLANES), rows), :]
        acc = None
        for k in range(ktaps):
            off = CONV_PAD - half + k - lo
            win = window if off == 0 else pltpu.roll(window, rows - off, 0)
            term = w_ref[k:k + 1, :] * win[0:CONV_ROWS, :]
            acc = term if acc is None else acc + term
        emit(r0, acc)
        return carry

    lax.fori_loop(0, n // CONV_ROWS, chunk, 0)


def _cf_conv_kernel(u_ref, w_ref, b_ref, o_ref, pad_ref):
    n = u_ref.shape[0]
    _stage_padded(u_ref[...].astype(F32), pad_ref)
    bias = b_ref[...]

    def emit(r0, acc):
        o_ref[pl.ds(r0, CONV_ROWS), :] = (acc + bias).astype(BF16)

    _dwconv_chunks(pad_ref, w_ref, n, w_ref.shape[0], emit)


def conformer_dwconv(geo, u, w, b, seq_len, row0, nb):
    d = geo.d
    dc = 256
    ktaps = w.shape[0]
    blk0 = row0 // seq_len
    return pl.pallas_call(
        _cf_conv_kernel,
        out_shape=jax.ShapeDtypeStruct((nb * seq_len, d), BF16),
        grid=(nb, d // dc),
        in_specs=[pl.BlockSpec((seq_len, dc), lambda b_, c: (blk0 + b_, c)),
                  pl.BlockSpec((ktaps, dc), lambda b_, c: (0, c)),
                  pl.BlockSpec((1, dc), lambda b_, c: (0, c))],
        out_specs=pl.BlockSpec((seq_len, dc), lambda b_, c: (b_, c)),
        scratch_shapes=[pltpu.VMEM((seq_len + 2 * CONV_PAD, dc), F32)],
        compiler_params=_params(("parallel", "parallel")),
        name=f"conformer_dwconv_{seq_len}",
    )(u, w, b.reshape(1, d))


def _sc_conv_kernel(q_ref, bg_ref, w_ref, o_ref, pad_ref):
    n = q_ref.shape[0]
    _stage_padded(q_ref[...].astype(F32), pad_ref)

    def emit(r0, acc):
        gate = bg_ref[pl.ds(r0, CONV_ROWS), :].astype(F32)
        o_ref[pl.ds(r0, CONV_ROWS), :] = (gate * acc).astype(BF16)

    _dwconv_chunks(pad_ref, w_ref, n, w_ref.shape[0], emit)


def shortconv_gate(geo, q, bg, w, seq_len, row0, nb):
    d = geo.d
    dc = 256
    ktaps = w.shape[0]
    blk0 = row0 // seq_len
    seq = pl.BlockSpec((seq_len, dc), lambda b_, c: (blk0 + b_, c))
    return pl.pallas_call(
        _sc_conv_kernel,
        out_shape=jax.ShapeDtypeStruct((nb * seq_len, d), BF16),
        grid=(nb, d // dc),
        in_specs=[seq, seq, pl.BlockSpec((ktaps, dc), lambda b_, c: (0, c))],
        out_specs=pl.BlockSpec((seq_len, dc), lambda b_, c: (b_, c)),
        scratch_shapes=[pltpu.VMEM((seq_len + 2 * CONV_PAD, dc), F32)],
        compiler_params=_params(("parallel", "parallel")),
        name=f"shortconv_gate_{seq_len}",
    )(q, bg, w)


def _mla_proj_kernel(x_ref, mod_ref, gn_ref, wdq_ref, gq_ref, wun_ref, wup_ref, wus_ref,
                     wkc_ref, wkp_ref, wks_ref, gkv_ref, cosq_ref, sinq_ref,
                     q_ref, ckv_ref, kpe_ref, *, heads, nope, rope):
    h = _prenorm(x_ref, mod_ref, gn_ref, 0)
    cq = (_rms(_dot(h, wdq_ref[...])) * gq_ref[...]).astype(BF16)
    qn = _dot(cq, wun_ref[...])
    cos2, sin2 = cosq_ref[...], sinq_ref[...]
    for pair in range(heads // 2):
        lo = pair * 2 * rope
        qp = (_dot(cq, wup_ref[:, lo:lo + 2 * rope]) * cos2
              + _dot(cq, wus_ref[:, lo:lo + 2 * rope]) * sin2)
        for j in range(2):
            hd = 2 * pair + j
            q_ref[hd, :, 0:nope] = qn[:, hd * nope:(hd + 1) * nope].astype(BF16)
            q_ref[hd, :, nope:nope + rope] = qp[:, j * rope:(j + 1) * rope].astype(BF16)
    ckv_ref[...] = _rms(_dot(h, wkc_ref[...])) * gkv_ref[...]
    kpe_ref[...] = (_dot(h, wkp_ref[...]) * cos2[:, 0:rope] + _dot(h, wks_ref[...]) * sin2[:, 0:rope])


def mla_project(geo, x, mod, gn, wdq, gq, wun, wup, wus, wkc, wkp, wks, gkv, cos2, sin2, heads, nope, rope):
    d = geo.d
    qr, kvr = wdq.shape[1], wkc.shape[1]
    row = lambda w: pl.BlockSpec((TM, w), lambda i: (i, 0))
    return pl.pallas_call(
        functools.partial(_mla_proj_kernel, heads=heads, nope=nope, rope=rope),
        out_shape=(jax.ShapeDtypeStruct((heads, geo.n, nope + rope), BF16),
                   jax.ShapeDtypeStruct((geo.n, kvr), F32),
                   jax.ShapeDtypeStruct((geo.n, rope), F32)),
        grid=(geo.n // TM,),
        in_specs=_row_specs(geo) + [_resident((d, qr)), _resident((1, qr)),
                                    _resident((qr, heads * nope)), _resident((qr, heads * rope)),
                                    _resident((qr, heads * rope)),
                                    _resident((d, kvr)), _resident((d, rope)), _resident((d, rope)),
                                    _resident((1, kvr)), row(2 * rope), row(2 * rope)],
        out_specs=(pl.BlockSpec((heads, TM, nope + rope), lambda i: (0, i, 0)), row(kvr), row(rope)),
        compiler_params=_params(("parallel",)),
        name="mla_project",
    )(x, mod, gn, wdq, gq.reshape(1, qr), wun, wup, wus, wkc, wkp, wks, gkv.reshape(1, kvr), cos2, sin2)


def _mla_kv_kernel(ckv_ref, kpe_ref, wn_ref, wv_ref, k_ref, v_ref, *, heads, nope, rope):
    c = ckv_ref[...].astype(BF16)
    kn = _dot(c, wn_ref[...])
    v_ref[...] = _dot(c, wv_ref[...]).astype(BF16)
    kp = kpe_ref[...].astype(BF16)
    for hd in range(heads):
        k_ref[hd, :, 0:nope] = kn[:, hd * nope:(hd + 1) * nope].astype(BF16)
        k_ref[hd, :, nope:nope + rope] = kp


def mla_expand_kv(ckv, kpe, wn, wv, heads, nope, rope, vdim):
    nb, lk, r = ckv.shape
    tk = 768 if lk % 768 == 0 else 256
    assert lk % tk == 0
    return pl.pallas_call(
        functools.partial(_mla_kv_kernel, heads=heads, nope=nope, rope=rope),
        out_shape=(jax.ShapeDtypeStruct((nb, heads, lk, nope + rope), BF16),
                   jax.ShapeDtypeStruct((nb, lk, heads * vdim), BF16)),
        grid=(nb, lk // tk),
        in_specs=[pl.BlockSpec((None, tk, r), lambda b, t: (b, t, 0)),
                  pl.BlockSpec((None, tk, rope), lambda b, t: (b, t, 0)),
                  _resident((r, heads * nope)), _resident((r, heads * vdim))],
        out_specs=(pl.BlockSpec((None, heads, tk, nope + rope), lambda b, t: (b, 0, t, 0)),
                   pl.BlockSpec((None, tk, heads * vdim), lambda b, t: (b, t, 0))),
        compiler_params=_params(("parallel", "parallel")),
        name=f"mla_expand_kv_{lk}",
    )(ckv, kpe, wn, wv)


def _attn_kernel(q_ref, k_ref, v_ref, o_ref, *, scale):
    s = lax.dot_general(q_ref[...], k_ref[...], (((1,), (1,)), ((), ())),
                        preferred_element_type=F32) * scale
    m = jnp.max(s, axis=-1, keepdims=True)
    p = jnp.exp(s - m)
    l = jnp.sum(p, axis=-1, keepdims=True)
    o = _dot(p.astype(BF16), v_ref[...])
    o_ref[...] = (o / l).astype(BF16)


def mla_attention(q, k, v, seq_len, row0, nb, scale, vdim):
    heads, _, dq = q.shape
    lk = k.shape[2]
    tq = min(seq_len, 512)
    nq = seq_len // tq
    blk0 = row0 // tq
    return pl.pallas_call(
        functools.partial(_attn_kernel, scale=scale),
        out_shape=jax.ShapeDtypeStruct((nb * seq_len, heads * vdim), BF16),
        grid=(nb, heads, nq),
        in_specs=[pl.BlockSpec((None, tq, dq), lambda b, h, t: (h, blk0 + b * nq + t, 0)),
                  pl.BlockSpec((None, None, lk, dq), lambda b, h, t: (b, h, 0, 0)),
                  pl.BlockSpec((None, lk, vdim), lambda b, h, t: (b, 0, h))],
        out_specs=pl.BlockSpec((tq, vdim), lambda b, h, t: (b * nq + t, h)),
        compiler_params=_params(("parallel", "parallel", "arbitrary")),
        name=f"mla_attention_{seq_len}",
    )(q, k, v)


def rope_tables(geo, rope):
    nf = rope // 4
    rows = geo.l_lat // GRID_W
    row = jnp.repeat(jnp.arange(rows, dtype=F32), GRID_W)
    col = jnp.tile(jnp.arange(GRID_W, dtype=F32), rows)
    inv = jnp.exp(-math.log(ROPE_THETA) * jnp.arange(nf, dtype=F32) * (4.0 / rope))
    ar, ac = row[:, None] * inv, col[:, None] * inv
    cos = jnp.concatenate([jnp.cos(ar), jnp.cos(ar), jnp.cos(ac), jnp.cos(ac)], axis=-1)
    sin = jnp.concatenate([-jnp.sin(ar), jnp.sin(ar), -jnp.sin(ac), jnp.sin(ac)], axis=-1)
    cos = jnp.concatenate([jnp.ones((geo.n_ctx, rope), F32), jnp.tile(cos, (geo.nb_lat, 1))], axis=0)
    sin = jnp.concatenate([jnp.zeros((geo.n_ctx, rope), F32), jnp.tile(sin, (geo.nb_lat, 1))], axis=0)
    return jnp.tile(cos, (1, 2)), jnp.tile(sin, (1, 2))


def _swap_rope_halves(w, rope):
    nf = rope // 4
    shape = w.shape
    w = w.reshape(shape[:-1] + (shape[-1] // (2 * nf), 2, nf))
    return w[..., ::-1, :].reshape(shape)


def kernel(x_prompt, x_sample, c, cache_ckv, cache_kpe, c_ctx, w_mod, b_mod, norm_g, ffn_w_gate, ffn_w_up, ffn_w_down, hy_w_in, hy_b_in, hy_conv_w, hy_conv_b, hy_f_w1, hy_f_b1, hy_f_freq, hy_f_w2, hy_f_b2, hy_f_w3, hy_skip, hy_w_out, hy_b_out, cf_w_pw1, cf_b_pw1, cf_dw_w, cf_dw_b, cf_ln_g, cf_ln_b, cf_w_pw2, cf_b_pw2, sc_w_in, sc_conv_w, sc_w_out, mla_w_dq, mla_g_q, mla_w_uq, mla_w_dkv, mla_g_kv, mla_w_ukv, mla_w_o):
    nb_ctx, l_ctx, d = x_prompt.shape
    nb_lat, l_lat, _ = x_sample.shape
    geo = Geometry(nb_ctx, l_ctx, nb_lat, l_lat, d)
    depth = w_mod.shape[0]
    n_mixers = 4

    x = jnp.concatenate([x_prompt.reshape(geo.n_ctx, d), x_sample.reshape(geo.n_lat, d)], axis=0)
    cond = jnp.concatenate([c_ctx[None, :], c], axis=0)
    n_cond = cond.shape[0]
    cond = jnp.pad(cond, ((0, -n_cond % 8), (0, 0)))
    mod_all = modulation_all(cond, w_mod, b_mod)[:, :n_cond].reshape(depth, n_cond, 6, d)

    passes = ((l_ctx, 0, nb_ctx), (l_lat, geo.n_ctx, nb_lat))
    zero_bias = jnp.zeros((d,), F32)
    new_ckv, new_kpe = [], []

    for i in range(depth):
        m, j = i % n_mixers, i // n_mixers
        mod, gn = mod_all[i], norm_g[i]
        if m == 0:
            p = hyena_in_proj(geo, x, mod, gn, hy_w_in[j].astype(BF16), hy_b_in[j])
            parts = []
            for seq_len, row0, nb in passes:
                tables = dft_tables(seq_len)
                spectrum = hyena_filter_spectrum(seq_len, d, tables, hy_f_w1[j], hy_f_b1[j], hy_f_freq[j],
                                                 hy_f_w2[j], hy_f_b2[j], hy_f_w3[j])
                parts.append(hyena_mix(geo, p, hy_conv_w[j], hy_conv_b[j].reshape(1, -1), hy_skip[j],
                                       tables, spectrum, seq_len, row0, nb))
            a = jnp.concatenate(parts, axis=0)
            x = out_proj_residual(geo, a, x, mod, gn, hy_w_out[j].astype(BF16), hy_b_out[j])
        elif m == 1:
            u = conformer_in_proj(geo, x, mod, gn, cf_w_pw1[j].astype(BF16), cf_b_pw1[j])
            a = jnp.concatenate([conformer_dwconv(geo, u, cf_dw_w[j], cf_dw_b[j], seq_len, row0, nb)
                                 for seq_len, row0, nb in passes], axis=0)
            x = out_proj_residual(geo, a, x, mod, gn, cf_w_pw2[j].astype(BF16), cf_b_pw2[j],
                                  ln=(cf_ln_g[j], cf_ln_b[j]))
        elif m == 2:
            bg, q = shortconv_in_proj(geo, x, mod, gn, sc_w_in[j].astype(BF16))
            a = jnp.concatenate([shortconv_gate(geo, q, bg, sc_conv_w[j], seq_len, row0, nb)
                                 for seq_len, row0, nb in passes], axis=0)
            x = out_proj_residual(geo, a, x, mod, gn, sc_w_out[j].astype(BF16), zero_bias)
        else:
            kvr = mla_g_kv.shape[1]
            rope = mla_w_dkv.shape[2] - kvr
            vdim = 128
            heads = mla_w_o.shape[1] // vdim
            nope = mla_w_uq.shape[2] // heads - rope
            wuq = mla_w_uq[j].reshape(-1, heads, nope + rope)
            wun = wuq[:, :, :nope].reshape(-1, heads * nope).astype(BF16)
            wup = wuq[:, :, nope:].reshape(-1, heads * rope)
            wukv = mla_w_ukv[j].reshape(kvr, heads, nope + vdim)
            wkn = wukv[:, :, :nope].reshape(kvr, heads * nope).astype(BF16)
            wkv = wukv[:, :, nope:].reshape(kvr, heads * vdim).astype(BF16)
            wkc, wkp = mla_w_dkv[j][:, :kvr], mla_w_dkv[j][:, kvr:]
            cos2, sin2 = rope_tables(geo, rope)
            q, ckv, kpe = mla_project(geo, x, mod, gn, mla_w_dq[j].astype(BF16), mla_g_q[j], wun,
                                      wup.astype(BF16), _swap_rope_halves(wup, rope).astype(BF16),
                                      wkc.astype(BF16), wkp.astype(BF16),
                                      _swap_rope_halves(wkp, rope).astype(BF16), mla_g_kv[j],
                                      cos2, sin2, heads, nope, rope)
            ckv_ctx = ckv[:geo.n_ctx].reshape(nb_ctx, l_ctx, kvr)
            kpe_ctx = kpe[:geo.n_ctx].reshape(nb_ctx, l_ctx, rope)
            new_ckv.append(ckv_ctx)
            new_kpe.append(kpe_ctx)
            ckv_lat = jnp.concatenate([cache_ckv[:, j], ckv[geo.n_ctx:].reshape(nb_lat, l_lat, kvr)], axis=1)
            kpe_lat = jnp.concatenate([cache_kpe[:, j], kpe[geo.n_ctx:].reshape(nb_lat, l_lat, rope)], axis=1)
            scale = (nope + rope) ** -0.5
            parts = []
            for (seq_len, row0, nb), (kc, kp) in zip(passes, ((ckv_ctx, kpe_ctx), (ckv_lat, kpe_lat))):
                kk, vv = mla_expand_kv(kc, kp, wkn, wkv, heads, nope, rope, vdim)
                parts.append(mla_attention(q, kk, vv, seq_len, row0, nb, scale, vdim))
            a = jnp.concatenate(parts, axis=0)
            x = out_proj_residual(geo, a, x, mod, gn, mla_w_o[j].astype(BF16), zero_bias)
        x = ffn_residual(geo, x, mod, gn, ffn_w_gate[i].astype(BF16), ffn_w_up[i].astype(BF16),
                         ffn_w_down[i].astype(BF16))

    y_prompt = x[:geo.n_ctx].reshape(nb_ctx, l_ctx, d)
    y_sample = x[geo.n_ctx:].reshape(nb_lat, l_lat, d)
    return (y_prompt, y_sample, jnp.stack(new_ckv, axis=1), jnp.stack(new_kpe, axis=1))
```

```python
import functools
import math

import jax
import jax.numpy as jnp
from jax import lax
from jax.experimental import pallas as pl
from jax.experimental.pallas import tpu as pltpu

F32 = jnp.float32
BF16 = jnp.bfloat16

NORM_EPS = 1e-6
GRID_W = 64
ROPE_THETA = 10000.0
HY_DECAY_TARGET = 1e-2
HY_FAST_PCT = 0.3
HY_SLOW_PCT = 1.5

V7X_VMEM_LIMIT_BYTES = 60000 * 1024
LANE = 128
SUBLANES = 8

TM = 512
DC = 256


def _dot(a, b):
    return jnp.dot(a, b, preferred_element_type=F32)


def _dot_hi(a, b):
    return jnp.dot(a, b, preferred_element_type=F32, precision=lax.Precision.HIGHEST)


def _rms(x):
    return x * lax.rsqrt(jnp.mean(x * x, axis=-1, keepdims=True) + NORM_EPS)


def _sigmoid(x):
    return 1.0 / (1.0 + jnp.exp(-x))


def _silu(x):
    return x * _sigmoid(x)


def _params(semantics):
    return pltpu.CompilerParams(dimension_semantics=semantics,
                                vmem_limit_bytes=V7X_VMEM_LIMIT_BYTES)


def _resident(shape):
    zeros = (0,) * len(shape)
    return pl.BlockSpec(shape, lambda *_: zeros, pipeline_mode=pl.Buffered(1))


class Geometry:
    def __init__(self, nb_ctx, l_ctx, nb_lat, l_lat, d):
        self.nb_ctx, self.l_ctx, self.nb_lat, self.l_lat, self.d = nb_ctx, l_ctx, nb_lat, l_lat, d
        self.n_ctx = nb_ctx * l_ctx
        self.n_lat = nb_lat * l_lat
        self.n = self.n_ctx + self.n_lat
        assert self.n_ctx % TM == 0 and l_lat % TM == 0 and self.n_ctx % l_lat == 0

    def cond_of_tile(self, i):
        row = i * TM
        return jnp.where(row < self.n_ctx, 0, 1 + (row - self.n_ctx) // self.l_lat)


def _mod_kernel(c_ref, w_ref, b_ref, o_ref):
    a = _silu(c_ref[...]).astype(BF16)
    o_ref[...] = _dot(a, w_ref[...].astype(BF16)) + b_ref[...]


def modulation_all(cond, w_mod, b_mod):
    depth, d, d6 = w_mod.shape
    r = cond.shape[0]
    tn = 1536
    return pl.pallas_call(
        _mod_kernel,
        out_shape=jax.ShapeDtypeStruct((depth, r, d6), F32),
        grid=(depth, d6 // tn),
        in_specs=[pl.BlockSpec((r, d), lambda l, j: (0, 0)),
                  pl.BlockSpec((None, d, tn), lambda l, j: (l, 0, j)),
                  pl.BlockSpec((None, 1, tn), lambda l, j: (l, 0, j))],
        out_specs=pl.BlockSpec((None, r, tn), lambda l, j: (l, 0, j)),
        compiler_params=_params(("parallel", "parallel")),
        name="modulation",
    )(cond, w_mod, b_mod.reshape(depth, 1, d6))


def _stream_args(op):
    return list(op) if isinstance(op, tuple) else [op]


def _stream_specs(geo, op, width):
    if not isinstance(op, tuple):
        return [pl.BlockSpec((TM, width), lambda i: (i, 0))]
    nct = geo.n_ctx // TM
    return [pl.BlockSpec((TM, width), lambda i: (jnp.minimum(i, nct - 1), 0)),
            pl.BlockSpec((TM, width), lambda i: (jnp.maximum(i - nct, 0), 0))]


def _stream_tile(refs, nct):
    if len(refs) == 1:
        return refs[0][...]
    return jnp.where(pl.program_id(0) < nct, refs[0][...], refs[1][...])


def _prenorm(x, mod_ref, gn_ref, which):
    g = gn_ref[2 * which:2 * which + 1, :]
    sh = mod_ref[3 * which:3 * which + 1, :]
    sc = mod_ref[3 * which + 1:3 * which + 2, :]
    return ((_rms(x) * g) * (1.0 + sc) + sh).astype(BF16)


def _row_specs(geo, x):
    d = geo.d
    return _stream_specs(geo, x, d) + [
        pl.BlockSpec((None, 6, d), lambda i: (geo.cond_of_tile(i), 0, 0)),
        pl.BlockSpec((4, d), lambda i: (0, 0))]


def _row_kernel(body, geo, x):
    nx, nct = len(_stream_args(x)), geo.n_ctx // TM

    def kern(*refs):
        body(_stream_tile(refs[:nx], nct), *refs[nx:])

    return kern


def _hy_proj_kernel(x, mod_ref, gn_ref, w_ref, b_ref, o_ref):
    h = _prenorm(x, mod_ref, gn_ref, 0)
    step = 512
    for c in range(0, o_ref.shape[1], step):
        o_ref[:, c:c + step] = (_dot(h, w_ref[:, c:c + step]) + b_ref[:, c:c + step]).astype(BF16)


def hyena_in_proj(geo, x, mod, gn, w_in, b_in):
    d, d3 = w_in.shape
    return pl.pallas_call(
        _row_kernel(_hy_proj_kernel, geo, x),
        out_shape=jax.ShapeDtypeStruct((geo.n, d3), BF16),
        grid=(geo.n // TM,),
        in_specs=_row_specs(geo, x) + [_resident((d, d3)), _resident((1, d3))],
        out_specs=pl.BlockSpec((TM, d3), lambda i: (i, 0)),
        compiler_params=_params(("parallel",)),
        name="hyena_in_proj",
    )(*_stream_args(x), mod, gn, w_in, b_in.reshape(1, d3))


def _cf_proj_kernel(x, mod_ref, gn_ref, w_ref, b_ref, o_ref):
    h = _prenorm(x, mod_ref, gn_ref, 0)
    d = o_ref.shape[1]
    step = 512
    for c in range(0, d, step):
        a = _dot(h, w_ref[:, c:c + step]) + b_ref[:, c:c + step]
        g = _dot(h, w_ref[:, d + c:d + c + step]) + b_ref[:, d + c:d + c + step]
        o_ref[:, c:c + step] = (a * _sigmoid(g)).astype(BF16)


def conformer_in_proj(geo, x, mod, gn, w_pw1, b_pw1):
    d, d2 = w_pw1.shape
    return pl.pallas_call(
        _row_kernel(_cf_proj_kernel, geo, x),
        out_shape=jax.ShapeDtypeStruct((geo.n, d), BF16),
        grid=(geo.n // TM,),
        in_specs=_row_specs(geo, x) + [_resident((d, d2)), _resident((1, d2))],
        out_specs=pl.BlockSpec((TM, d), lambda i: (i, 0)),
        compiler_params=_params(("parallel",)),
        name="conformer_in_proj",
    )(*_stream_args(x), mod, gn, w_pw1, b_pw1.reshape(1, d2))


def _sc_proj_kernel(x, mod_ref, gn_ref, w_ref, bg_ref, q_ref):
    h = _prenorm(x, mod_ref, gn_ref, 0)
    d = bg_ref.shape[1]
    step = 512
    for c in range(0, d, step):
        bg_ref[:, c:c + step] = _dot(h, w_ref[:, c:c + step]).astype(BF16)
        cg = _dot(h, w_ref[:, d + c:d + c + step])
        hv = _dot(h, w_ref[:, 2 * d + c:2 * d + c + step])
        q_ref[:, c:c + step] = (cg * hv).astype(BF16)


def shortconv_in_proj(geo, x, mod, gn, w_in):
    d, d3 = w_in.shape
    out = jax.ShapeDtypeStruct((geo.n, d), BF16)
    return pl.pallas_call(
        _row_kernel(_sc_proj_kernel, geo, x),
        out_shape=(out, out),
        grid=(geo.n // TM,),
        in_specs=_row_specs(geo, x) + [_resident((d, d3))],
        out_specs=(pl.BlockSpec((TM, d), lambda i: (i, 0)), pl.BlockSpec((TM, d), lambda i: (i, 0))),
        compiler_params=_params(("parallel",)),
        name="shortconv_in_proj",
    )(*_stream_args(x), mod, gn, w_in)


def _out_proj_kernel(a, x, mod_ref, gn_ref, w_ref, b_ref, o_ref):
    y = _dot(a, w_ref[...]) + b_ref[...]
    o_ref[...] = x + mod_ref[2:3, :] * (_rms(y) * gn_ref[1:2, :])


def _out_proj_ln_kernel(a, x, mod_ref, gn_ref, w_ref, b_ref, lg_ref, lb_ref, o_ref):
    u = a.astype(F32)
    mu = jnp.mean(u, axis=-1, keepdims=True)
    uc = u - mu
    var = jnp.mean(uc * uc, axis=-1, keepdims=True)
    v = (uc * lax.rsqrt(var + NORM_EPS)) * lg_ref[...] + lb_ref[...]
    y = _dot(_silu(v).astype(BF16), w_ref[...]) + b_ref[...]
    o_ref[...] = x + mod_ref[2:3, :] * (_rms(y) * gn_ref[1:2, :])


def out_proj_residual(geo, a, x, mod, gn, w, b, ln=None):
    d = geo.d
    na, nx, nct = len(_stream_args(a)), len(_stream_args(x)), geo.n_ctx // TM
    body = _out_proj_kernel if ln is None else _out_proj_ln_kernel

    def kern(*refs):
        body(_stream_tile(refs[:na], nct), _stream_tile(refs[na:na + nx], nct), *refs[na + nx:])

    specs = _stream_specs(geo, a, d) + _row_specs(geo, x) + [_resident((d, d)), _resident((1, d))]
    args = _stream_args(a) + _stream_args(x) + [mod, gn, w, b.reshape(1, d)]
    if ln is not None:
        specs += [_resident((1, d)), _resident((1, d))]
        args += [ln[0].reshape(1, d), ln[1].reshape(1, d)]
    return pl.pallas_call(
        kern,
        out_shape=jax.ShapeDtypeStruct((geo.n, d), F32),
        grid=(geo.n // TM,),
        in_specs=specs,
        out_specs=pl.BlockSpec((TM, d), lambda i: (i, 0)),
        compiler_params=_params(("parallel",)),
        name="out_proj_residual" if ln is None else "out_proj_ln_residual",
    )(*args)


def _ffn_kernel(x, mod_ref, gn_ref, wg_ref, wu_ref, wd_ref, *rest, nct):
    a_ref = rest[-1]
    h = _prenorm(x, mod_ref, gn_ref, 1)
    step = 256
    for c in range(0, a_ref.shape[1], step):
        g = _dot(h, wg_ref[:, c:c + step])
        u = _dot(h, wu_ref[:, c:c + step])
        a_ref[:, c:c + step] = (_silu(g) * u).astype(BF16)
    y = _dot(a_ref[...], wd_ref[...])
    out = x + mod_ref[5:6, :] * (_rms(y) * gn_ref[3:4, :])
    if len(rest) == 2:
        rest[0][...] = out
    else:
        i = pl.program_id(0)

        @pl.when(i < nct)
        def _():
            rest[0][...] = out

        @pl.when(i >= nct)
        def _():
            rest[1][...] = out


def ffn_residual(geo, x, mod, gn, wg, wu, wd, split=False):
    d, dff = wg.shape
    nct = geo.n_ctx // TM
    if split:
        out_shape = (jax.ShapeDtypeStruct((geo.n_ctx, d), F32), jax.ShapeDtypeStruct((geo.n_lat, d), F32))
        out_specs = tuple(_stream_specs(geo, (None, None), d))
    else:
        out_shape = jax.ShapeDtypeStruct((geo.n, d), F32)
        out_specs = pl.BlockSpec((TM, d), lambda i: (i, 0))
    return pl.pallas_call(
        _row_kernel(functools.partial(_ffn_kernel, nct=nct), geo, x),
        out_shape=out_shape,
        grid=(geo.n // TM,),
        in_specs=_row_specs(geo, x) + [_resident((d, dff)), _resident((d, dff)), _resident((dff, d))],
        out_specs=out_specs,
        scratch_shapes=[pltpu.VMEM((TM, dff), BF16)],
        compiler_params=_params(("arbitrary",) if split else ("parallel",)),
        name="ffn_residual",
    )(*_stream_args(x), mod, gn, wg, wu, wd)


def dft_tables(seq_len):
    k = jnp.arange(seq_len, dtype=jnp.int32)
    idx = (k[:, None] * k[None, :]) % (2 * seq_len)
    ang = idx.astype(F32) * (math.pi / seq_len)
    sign = (1 - 2 * (k % 2)).astype(F32)[:, None]
    cvec = jnp.where(k == 0, 0.5 / seq_len, 1.0 / seq_len).astype(F32)[:, None]
    cs = jnp.concatenate([jnp.cos(ang).astype(BF16), jnp.sin(ang).astype(BF16)], axis=1)
    return cs, sign, cvec


def _filter_kernel(z_ref, w1_ref, b1_ref, fr_ref, w2_ref, b2_ref, w3f_ref, w3b_ref, t_ref, dl_ref,
                   sign_ref, cvec_ref, cs_ref, hr_ref, hi_ref, hn_ref):
    seq_len = z_ref.shape[0]
    h = jnp.sin(fr_ref[0:1, :] * (_dot_hi(z_ref[...], w1_ref[...]) + b1_ref[...]))
    h = jnp.sin(fr_ref[1:2, :] * (_dot_hi(h, w2_ref[...]) + b2_ref[...]))
    decay = jnp.exp(-t_ref[...] * jnp.abs(dl_ref[...]))
    hf = _dot_hi(h, w3f_ref[...]) * decay
    hb = _dot_hi(h, w3b_ref[...]) * decay
    hs = hf + hb
    hd = hf - hb
    cv = cvec_ref[...]
    hr_ref[...] = cv * _dot(cs_ref[:, 0:seq_len], hs.astype(BF16))
    hi_ref[...] = cv * _dot(cs_ref[:, seq_len:2 * seq_len], hd.astype(BF16))
    hn_ref[...] = jnp.sum(hs * sign_ref[...], axis=0, keepdims=True) * (0.5 / seq_len)


def hyena_filter_spectrum(seq_len, d, tables, w1, b1, freq, w2, b2, w3):
    cs, sign, cvec = tables
    emb, fw = w1.shape
    bands = (emb - 1) // 2
    t = jnp.linspace(0.0, 1.0, seq_len, dtype=F32)[:, None]
    w = 2.0 * math.pi * jnp.arange(seq_len, dtype=F32)[:, None] / seq_len
    f = jnp.linspace(1e-4, bands - 1, bands, dtype=F32)[None, :]
    z = jnp.concatenate([t, jnp.cos(f * w), -jnp.sin(f * w)], axis=-1)
    z = jnp.pad(z, ((0, 0), (0, LANE - emb)))
    w1p = jnp.pad(w1, ((0, LANE - emb), (0, 0)))
    deltas = jnp.linspace(math.log(HY_DECAY_TARGET) / HY_SLOW_PCT,
                          math.log(HY_DECAY_TARGET) / HY_FAST_PCT, d, dtype=F32)[None, :]
    nd = d // DC
    full = lambda shape: pl.BlockSpec(shape, lambda j: (0,) * len(shape))
    spec_out = pl.BlockSpec((seq_len, DC), lambda j: (0, j))
    return pl.pallas_call(
        _filter_kernel,
        out_shape=(jax.ShapeDtypeStruct((seq_len, 2 * d), F32),
                   jax.ShapeDtypeStruct((seq_len, 2 * d), F32),
                   jax.ShapeDtypeStruct((1, 2 * d), F32)),
        grid=(2 * nd,),
        in_specs=[full((seq_len, LANE)), full((LANE, fw)), full((1, fw)), full((2, fw)),
                  full((fw, fw)), full((1, fw)),
                  pl.BlockSpec((fw, DC), lambda j: (0, j)),
                  pl.BlockSpec((fw, DC), lambda j: (0, j + 2 * nd)),
                  full((seq_len, 1)),
                  pl.BlockSpec((1, DC), lambda j: (0, j % nd)),
                  full((seq_len, 1)), full((seq_len, 1)),
                  _resident((seq_len, 2 * seq_len))],
        out_specs=(spec_out, spec_out, pl.BlockSpec((1, DC), lambda j: (0, j))),
        compiler_params=_params(("parallel",)),
        name=f"hyena_filter_{seq_len}",
    )(z, w1p, b1.reshape(1, fw), freq, w2, b2.reshape(1, fw), w3, w3, t, deltas, sign, cvec, cs)


HY_GROUP = 2
HY_HALO = 16


def _dwconv3_rows(p_ref, base, r, nrows, seq_len, w_ref, b_ref):
    lo, hi = max(r - HY_HALO, 0), min(r + nrows + HY_HALO, seq_len)
    win = p_ref[base + lo:base + hi, :].astype(F32)
    n = hi - lo
    prev = pltpu.roll(win, 1, 0)
    nxt = pltpu.roll(win, n - 1, 0)
    rows = lax.broadcasted_iota(jnp.int32, (n, 1), 0) + lo
    prev = jnp.where(rows == 0, 0.0, prev)
    nxt = jnp.where(rows == seq_len - 1, 0.0, nxt)
    y = w_ref[0:1, :] * prev + w_ref[1:2, :] * win + w_ref[2:3, :] * nxt + b_ref[...]
    return y[r - lo:r - lo + nrows, :]


def _hyena_mix_kernel(pv_ref, p1_ref, p2_ref, wv_ref, w1_ref, w2_ref, bv_ref, b1_ref, b2_ref,
                      cs_ref, hr1_ref, hi1_ref, hn1_ref, hr2_ref, hi2_ref, hn2_ref, skip_ref,
                      o_ref, z_ref, zb_ref, y_ref, *, seq_len, chunk):
    dc = pv_ref.shape[1]
    groups = pv_ref.shape[0] // seq_len
    chunks = range(0, seq_len, chunk)
    tile = lambda a: jnp.concatenate([a] * groups, axis=1)

    def sign_rows(r):
        rows = lax.broadcasted_iota(jnp.int32, (chunk, 1), 0) + r
        return (1 - 2 * (rows % 2)).astype(F32)

    def stage(r, val, nyq):
        z_ref[r:r + chunk, :] = val
        zb_ref[r:r + chunk, :] = val.astype(BF16)
        part = jnp.sum(val * sign_rows(r), axis=0, keepdims=True)
        return part if nyq is None else nyq + part

    def transform(hr_ref, hi_ref, hn_ref, skip, mult_ref, mw_ref, mb_ref, nyq, emit):
        for r in chunks:
            a = _dot(cs_ref[r:r + chunk, 0:seq_len], zb_ref[...])
            b = _dot(cs_ref[r:r + chunk, seq_len:2 * seq_len], zb_ref[...])
            hr, hi = tile(hr_ref[r:r + chunk, :]), tile(hi_ref[r:r + chunk, :])
            y_ref[r:r + chunk, :] = (a * hr - b * hi).astype(BF16)
            y_ref[seq_len + r:seq_len + r + chunk, :] = (a * hi + b * hr).astype(BF16)
        nyq_term = nyq * tile(hn_ref[...])
        skip_t = tile(skip)
        for r in chunks:
            y = _dot(cs_ref[r:r + chunk, :], y_ref[...])
            y = y + sign_rows(r) * nyq_term + z_ref[r:r + chunk, :] * skip_t
            mult = jnp.concatenate([_dwconv3_rows(mult_ref, g * seq_len, r, chunk, seq_len, mw_ref, mb_ref)
                                    for g in range(groups)], axis=1)
            emit(r, mult * y)

    nyq = None
    for r in chunks:
        v = jnp.concatenate([_dwconv3_rows(pv_ref, g * seq_len, r, chunk, seq_len, wv_ref, bv_ref)
                             for g in range(groups)], axis=1)
        nyq = stage(r, v, nyq)

    nyq2 = []

    def restage(r, val):
        nyq2[:] = [stage(r, val, nyq2[0] if nyq2 else None)]

    transform(hr1_ref, hi1_ref, hn1_ref, skip_ref[0:1, :], p1_ref, w1_ref, b1_ref, nyq, restage)

    def write_out(r, val):
        for g in range(groups):
            o_ref[g * seq_len + r:g * seq_len + r + chunk, :] = val[:, g * dc:(g + 1) * dc].astype(BF16)

    transform(hr2_ref, hi2_ref, hn2_ref, skip_ref[1:2, :], p2_ref, w2_ref, b2_ref, nyq2[0], write_out)


def hyena_mix(geo, p, conv_w, conv_b, skip, tables, spectrum, seq_len, row0, nb):
    d = geo.d
    cs = tables[0]
    hr, hi, hn = spectrum
    nd = d // DC
    rows = HY_GROUP * seq_len
    assert nb % HY_GROUP == 0 and row0 % rows == 0
    blk0 = row0 // rows
    chunk = min(seq_len, 512)
    pspec = lambda part: pl.BlockSpec((rows, DC), lambda c, b: (blk0 + b, part * nd + c))
    wspec = lambda part: pl.BlockSpec((3, DC), lambda c, b: (0, part * nd + c))
    bspec = lambda part: pl.BlockSpec((1, DC), lambda c, b: (0, part * nd + c))
    hspec = lambda order: pl.BlockSpec((seq_len, DC), lambda c, b: (0, order * nd + c),
                                       pipeline_mode=pl.Buffered(1))
    nspec = lambda order: pl.BlockSpec((1, DC), lambda c, b: (0, order * nd + c))
    return pl.pallas_call(
        functools.partial(_hyena_mix_kernel, seq_len=seq_len, chunk=chunk),
        out_shape=jax.ShapeDtypeStruct((nb * seq_len, d), BF16),
        grid=(nd, nb // HY_GROUP),
        in_specs=[pspec(0), pspec(1), pspec(2), wspec(0), wspec(1), wspec(2), bspec(0), bspec(1), bspec(2),
                  _resident((seq_len, 2 * seq_len)),
                  hspec(0), hspec(0), nspec(0), hspec(1), hspec(1), nspec(1),
                  pl.BlockSpec((2, DC), lambda c, b: (0, c))],
        out_specs=pl.BlockSpec((rows, DC), lambda c, b: (b, c)),
        scratch_shapes=[pltpu.VMEM((seq_len, HY_GROUP * DC), F32),
                        pltpu.VMEM((seq_len, HY_GROUP * DC), BF16),
                        pltpu.VMEM((2 * seq_len, HY_GROUP * DC), BF16)],
        compiler_params=_params(("arbitrary", "arbitrary")),
        name=f"hyena_mix_{seq_len}",
    )(p, p, p, conv_w, conv_w, conv_w, conv_b, conv_b, conv_b, cs,
      hr, hi, hn, hr, hi, hn, skip)


CONV_ROWS = 64
CONV_PAD = 16


def _stage_padded(src, pad_ref):
    n = src.shape[0]
    zeros = jnp.zeros((CONV_PAD, pad_ref.shape[1]), F32)
    pad_ref[0:CONV_PAD, :] = zeros
    pad_ref[CONV_PAD + n:2 * CONV_PAD + n, :] = zeros
    pad_ref[CONV_PAD:CONV_PAD + n, :] = src


def _dwconv_chunks(pad_ref, w_ref, n, ktaps, emit):
    half = ktaps // 2
    lo = (CONV_PAD - half) // SUBLANES * SUBLANES
    hi = -(-(CONV_PAD + half + CONV_ROWS) // SUBLANES) * SUBLANES
    rows = hi - lo

    def chunk(ci, carry):
        r0 = pl.multiple_of(ci * CONV_ROWS, CONV_ROWS)
        window = pad_ref[pl.ds(pl.multiple_of(r0 + lo, SUBLANES), rows), :]
        acc = None
        for k in range(ktaps):
            off = CONV_PAD - half + k - lo
            win = window if off == 0 else pltpu.roll(window, rows - off, 0)
            term = w_ref[k:k + 1, :] * win[0:CONV_ROWS, :]
            acc = term if acc is None else acc + term
        emit(r0, acc)
        return carry

    lax.fori_loop(0, n // CONV_ROWS, chunk, 0)


def _cf_conv_kernel(u_ref, w_ref, b_ref, o_ref, pad_ref):
    n = u_ref.shape[0]
    _stage_padded(u_ref[...].astype(F32), pad_ref)
    bias = b_ref[...]

    def emit(r0, acc):
        o_ref[pl.ds(r0, CONV_ROWS), :] = (acc + bias).astype(BF16)

    _dwconv_chunks(pad_ref, w_ref, n, w_ref.shape[0], emit)


def conformer_dwconv(geo, u, w, b, seq_len, row0, nb):
    d = geo.d
    dc = 256
    ktaps = w.shape[0]
    blk0 = row0 // seq_len
    return pl.pallas_call(
        _cf_conv_kernel,
        out_shape=jax.ShapeDtypeStruct((nb * seq_len, d), BF16),
        grid=(nb, d // dc),
        in_specs=[pl.BlockSpec((seq_len, dc), lambda b_, c: (blk0 + b_, c)),
                  pl.BlockSpec((ktaps, dc), lambda b_, c: (0, c)),
                  pl.BlockSpec((1, dc), lambda b_, c: (0, c))],
        out_specs=pl.BlockSpec((seq_len, dc), lambda b_, c: (b_, c)),
        scratch_shapes=[pltpu.VMEM((seq_len + 2 * CONV_PAD, dc), F32)],
        compiler_params=_params(("parallel", "parallel")),
        name=f"conformer_dwconv_{seq_len}",
    )(u, w, b.reshape(1, d))


def _sc_conv_kernel(q_ref, bg_ref, w_ref, o_ref, pad_ref):
    n = q_ref.shape[0]
    _stage_padded(q_ref[...].astype(F32), pad_ref)

    def emit(r0, acc):
        gate = bg_ref[pl.ds(r0, CONV_ROWS), :].astype(F32)
        o_ref[pl.ds(r0, CONV_ROWS), :] = (gate * acc).astype(BF16)

    _dwconv_chunks(pad_ref, w_ref, n, w_ref.shape[0], emit)


def shortconv_gate(geo, q, bg, w, seq_len, row0, nb):
    d = geo.d
    dc = 256
    ktaps = w.shape[0]
    blk0 = row0 // seq_len
    seq = pl.BlockSpec((seq_len, dc), lambda b_, c: (blk0 + b_, c))
    return pl.pallas_call(
        _sc_conv_kernel,
        out_shape=jax.ShapeDtypeStruct((nb * seq_len, d), BF16),
        grid=(nb, d // dc),
        in_specs=[seq, seq, pl.BlockSpec((ktaps, dc), lambda b_, c: (0, c))],
        out_specs=pl.BlockSpec((seq_len, dc), lambda b_, c: (b_, c)),
        scratch_shapes=[pltpu.VMEM((seq_len + 2 * CONV_PAD, dc), F32)],
        compiler_params=_params(("parallel", "parallel")),
        name=f"shortconv_gate_{seq_len}",
    )(q, bg, w)


def _mla_proj_kernel(x, mod_ref, gn_ref, wdq_ref, gq_ref, wun_ref, wup_ref, wus_ref,
                     wkc_ref, wkp_ref, wks_ref, gkv_ref, cosq_ref, sinq_ref,
                     q_ref, ckv_ref, kpe_ref, *, heads, nope, rope, qscale):
    h = _prenorm(x, mod_ref, gn_ref, 0)
    cq = (_rms(_dot(h, wdq_ref[...])) * gq_ref[...]).astype(BF16)
    qn = _dot(cq, wun_ref[...]) * qscale
    cos2, sin2 = cosq_ref[...], sinq_ref[...]
    cosq, sinq = cos2 * qscale, sin2 * qscale
    for pair in range(heads // 2):
        lo = pair * 2 * rope
        qp = (_dot(cq, wup_ref[:, lo:lo + 2 * rope]) * cosq
              + _dot(cq, wus_ref[:, lo:lo + 2 * rope]) * sinq)
        for j in range(2):
            hd = 2 * pair + j
            q_ref[hd, :, 0:nope] = qn[:, hd * nope:(hd + 1) * nope].astype(BF16)
            q_ref[hd, :, nope:nope + rope] = qp[:, j * rope:(j + 1) * rope].astype(BF16)
    ckv_ref[...] = _rms(_dot(h, wkc_ref[...])) * gkv_ref[...]
    kpe_ref[...] = (_dot(h, wkp_ref[...]) * cos2[:, 0:rope] + _dot(h, wks_ref[...]) * sin2[:, 0:rope])


def mla_project(geo, x, mod, gn, wdq, gq, wun, wup, wus, wkc, wkp, wks, gkv, cos2, sin2, heads, nope, rope,
                qscale):
    d = geo.d
    qr, kvr = wdq.shape[1], wkc.shape[1]
    row = lambda w: pl.BlockSpec((TM, w), lambda i: (i, 0))
    body = functools.partial(_mla_proj_kernel, heads=heads, nope=nope, rope=rope, qscale=qscale)
    return pl.pallas_call(
        _row_kernel(body, geo, x),
        out_shape=(jax.ShapeDtypeStruct((heads, geo.n, nope + rope), BF16),
                   jax.ShapeDtypeStruct((geo.n, kvr), F32),
                   jax.ShapeDtypeStruct((geo.n, rope), F32)),
        grid=(geo.n // TM,),
        in_specs=_row_specs(geo, x) + [_resident((d, qr)), _resident((1, qr)),
                                    _resident((qr, heads * nope)), _resident((qr, heads * rope)),
                                    _resident((qr, heads * rope)),
                                    _resident((d, kvr)), _resident((d, rope)), _resident((d, rope)),
                                    _resident((1, kvr)), row(2 * rope), row(2 * rope)],
        out_specs=(pl.BlockSpec((heads, TM, nope + rope), lambda i: (0, i, 0)), row(kvr), row(rope)),
        compiler_params=_params(("parallel",)),
        name="mla_project",
    )(*_stream_args(x), mod, gn, wdq, gq.reshape(1, qr), wun, wup, wus, wkc, wkp, wks, gkv.reshape(1, kvr),
      cos2, sin2)


def _mla_kv_kernel(ckv_ref, kpe_ref, wn_ref, wv_ref, k_ref, v_ref, *, heads, nope, rope):
    c = ckv_ref[...].astype(BF16)
    kn = _dot(c, wn_ref[...])
    v_ref[...] = _dot(c, wv_ref[...]).astype(BF16)
    kp = kpe_ref[...].astype(BF16)
    for hd in range(heads):
        k_ref[hd, :, 0:nope] = kn[:, hd * nope:(hd + 1) * nope].astype(BF16)
        k_ref[hd, :, nope:nope + rope] = kp


def mla_expand_kv(ckv, kpe, wn, wv, heads, nope, rope, vdim):
    nb, lk, r = ckv.shape
    tk = 768 if lk % 768 == 0 else 256
    assert lk % tk == 0
    return pl.pallas_call(
        functools.partial(_mla_kv_kernel, heads=heads, nope=nope, rope=rope),
        out_shape=(jax.ShapeDtypeStruct((nb, heads, lk, nope + rope), BF16),
                   jax.ShapeDtypeStruct((nb, lk, heads * vdim), BF16)),
        grid=(nb, lk // tk),
        in_specs=[pl.BlockSpec((None, tk, r), lambda b, t: (b, t, 0)),
                  pl.BlockSpec((None, tk, rope), lambda b, t: (b, t, 0)),
                  _resident((r, heads * nope)), _resident((r, heads * vdim))],
        out_specs=(pl.BlockSpec((None, heads, tk, nope + rope), lambda b, t: (b, 0, t, 0)),
                   pl.BlockSpec((None, tk, heads * vdim), lambda b, t: (b, t, 0))),
        compiler_params=_params(("parallel", "parallel")),
        name=f"mla_expand_kv_{lk}",
    )(ckv, kpe, wn, wv)


ATTN_ROWS = 256


def _attn_kernel(q_ref, k_ref, v_ref, o_ref, *, vdim):
    hg, tq, _ = q_ref.shape
    for h in range(hg):
        k = k_ref[h]
        v = v_ref[:, h * vdim:(h + 1) * vdim]
        for r in range(0, tq, ATTN_ROWS):
            s = lax.dot_general(q_ref[h, r:r + ATTN_ROWS, :], k, (((1,), (1,)), ((), ())),
                                preferred_element_type=F32)
            m = jnp.max(s, axis=-1, keepdims=True)
            p = jnp.exp2(s - m)
            l = jnp.sum(p, axis=-1, keepdims=True)
            o = _dot(p.astype(BF16), v)
            o_ref[r:r + ATTN_ROWS, h * vdim:(h + 1) * vdim] = (o * (1.0 / l)).astype(BF16)


def mla_attention(q, k, v, seq_len, row0, nb, vdim):
    heads, _, dq = q.shape
    lk = k.shape[2]
    hg = max(1, min(heads, 2048 // seq_len))
    blk0 = row0 // seq_len
    return pl.pallas_call(
        functools.partial(_attn_kernel, vdim=vdim),
        out_shape=jax.ShapeDtypeStruct((nb * seq_len, heads * vdim), BF16),
        grid=(nb, heads // hg),
        in_specs=[pl.BlockSpec((hg, seq_len, dq), lambda b, h: (h, blk0 + b, 0)),
                  pl.BlockSpec((None, hg, lk, dq), lambda b, h: (b, h, 0, 0)),
                  pl.BlockSpec((None, lk, hg * vdim), lambda b, h: (b, 0, h))],
        out_specs=pl.BlockSpec((seq_len, hg * vdim), lambda b, h: (b, h)),
        compiler_params=_params(("parallel", "parallel")),
        name=f"mla_attention_{seq_len}",
    )(q, k, v)


def rope_tables(geo, rope):
    nf = rope // 4
    rows = geo.l_lat // GRID_W
    row = jnp.repeat(jnp.arange(rows, dtype=F32), GRID_W)
    col = jnp.tile(jnp.arange(GRID_W, dtype=F32), rows)
    inv = jnp.exp(-math.log(ROPE_THETA) * jnp.arange(nf, dtype=F32) * (4.0 / rope))
    ar, ac = row[:, None] * inv, col[:, None] * inv
    cos = jnp.concatenate([jnp.cos(ar), jnp.cos(ar), jnp.cos(ac), jnp.cos(ac)], axis=-1)
    sin = jnp.concatenate([-jnp.sin(ar), jnp.sin(ar), -jnp.sin(ac), jnp.sin(ac)], axis=-1)
    cos = jnp.concatenate([jnp.ones((geo.n_ctx, rope), F32), jnp.tile(cos, (geo.nb_lat, 1))], axis=0)
    sin = jnp.concatenate([jnp.zeros((geo.n_ctx, rope), F32), jnp.tile(sin, (geo.nb_lat, 1))], axis=0)
    return jnp.tile(cos, (1, 2)), jnp.tile(sin, (1, 2))


def _swap_rope_halves(w, rope):
    nf = rope // 4
    shape = w.shape
    w = w.reshape(shape[:-1] + (shape[-1] // (2 * nf), 2, nf))
    return w[..., ::-1, :].reshape(shape)


def kernel(x_prompt, x_sample, c, cache_ckv, cache_kpe, c_ctx, w_mod, b_mod, norm_g, ffn_w_gate, ffn_w_up, ffn_w_down, hy_w_in, hy_b_in, hy_conv_w, hy_conv_b, hy_f_w1, hy_f_b1, hy_f_freq, hy_f_w2, hy_f_b2, hy_f_w3, hy_skip, hy_w_out, hy_b_out, cf_w_pw1, cf_b_pw1, cf_dw_w, cf_dw_b, cf_ln_g, cf_ln_b, cf_w_pw2, cf_b_pw2, sc_w_in, sc_conv_w, sc_w_out, mla_w_dq, mla_g_q, mla_w_uq, mla_w_dkv, mla_g_kv, mla_w_ukv, mla_w_o):
    nb_ctx, l_ctx, d = x_prompt.shape
    nb_lat, l_lat, _ = x_sample.shape
    geo = Geometry(nb_ctx, l_ctx, nb_lat, l_lat, d)
    depth = w_mod.shape[0]
    n_mixers = 4

    x = (x_prompt.reshape(geo.n_ctx, d), x_sample.reshape(geo.n_lat, d))
    cond = jnp.concatenate([c_ctx[None, :], c], axis=0)
    n_cond = cond.shape[0]
    cond = jnp.pad(cond, ((0, -n_cond % 8), (0, 0)))
    mod_all = modulation_all(cond, w_mod, b_mod)[:, :n_cond].reshape(depth, n_cond, 6, d)

    passes = ((l_ctx, 0, nb_ctx), (l_lat, geo.n_ctx, nb_lat))
    zero_bias = jnp.zeros((d,), F32)
    new_ckv, new_kpe = [], []

    for i in range(depth):
        m, j = i % n_mixers, i // n_mixers
        mod, gn = mod_all[i], norm_g[i]
        if m == 0:
            p = hyena_in_proj(geo, x, mod, gn, hy_w_in[j].astype(BF16), hy_b_in[j])
            parts = []
            for seq_len, row0, nb in passes:
                tables = dft_tables(seq_len)
                spectrum = hyena_filter_spectrum(seq_len, d, tables, hy_f_w1[j], hy_f_b1[j], hy_f_freq[j],
                                                 hy_f_w2[j], hy_f_b2[j], hy_f_w3[j])
                parts.append(hyena_mix(geo, p, hy_conv_w[j], hy_conv_b[j].reshape(1, -1), hy_skip[j],
                                       tables, spectrum, seq_len, row0, nb))
            x = out_proj_residual(geo, tuple(parts), x, mod, gn, hy_w_out[j].astype(BF16), hy_b_out[j])
        elif m == 1:
            u = conformer_in_proj(geo, x, mod, gn, cf_w_pw1[j].astype(BF16), cf_b_pw1[j])
            a = tuple(conformer_dwconv(geo, u, cf_dw_w[j], cf_dw_b[j], seq_len, row0, nb)
                      for seq_len, row0, nb in passes)
            x = out_proj_residual(geo, a, x, mod, gn, cf_w_pw2[j].astype(BF16), cf_b_pw2[j],
                                  ln=(cf_ln_g[j], cf_ln_b[j]))
        elif m == 2:
            bg, q = shortconv_in_proj(geo, x, mod, gn, sc_w_in[j].astype(BF16))
            a = tuple(shortconv_gate(geo, q, bg, sc_conv_w[j], seq_len, row0, nb)
                      for seq_len, row0, nb in passes)
            x = out_proj_residual(geo, a, x, mod, gn, sc_w_out[j].astype(BF16), zero_bias)
        else:
            kvr = mla_g_kv.shape[1]
            rope = mla_w_dkv.shape[2] - kvr
            vdim = 128
            heads = mla_w_o.shape[1] // vdim
            nope = mla_w_uq.shape[2] // heads - rope
            wuq = mla_w_uq[j].reshape(-1, heads, nope + rope)
            wun = wuq[:, :, :nope].reshape(-1, heads * nope).astype(BF16)
            wup = wuq[:, :, nope:].reshape(-1, heads * rope)
            wukv = mla_w_ukv[j].reshape(kvr, heads, nope + vdim)
            wkn = wukv[:, :, :nope].reshape(kvr, heads * nope).astype(BF16)
            wkv = wukv[:, :, nope:].reshape(kvr, heads * vdim).astype(BF16)
            wkc, wkp = mla_w_dkv[j][:, :kvr], mla_w_dkv[j][:, kvr:]
            cos2, sin2 = rope_tables(geo, rope)
            q, ckv, kpe = mla_project(geo, x, mod, gn, mla_w_dq[j].astype(BF16), mla_g_q[j], wun,
                                      wup.astype(BF16), _swap_rope_halves(wup, rope).astype(BF16),
                                      wkc.astype(BF16), wkp.astype(BF16),
                                      _swap_rope_halves(wkp, rope).astype(BF16), mla_g_kv[j],
                                      cos2, sin2, heads, nope, rope,
                                      (nope + rope) ** -0.5 * math.log2(math.e))
            ckv_ctx = ckv[:geo.n_ctx].reshape(nb_ctx, l_ctx, kvr)
            kpe_ctx = kpe[:geo.n_ctx].reshape(nb_ctx, l_ctx, rope)
            new_ckv.append(ckv_ctx)
            new_kpe.append(kpe_ctx)
            ckv_lat = jnp.concatenate([cache_ckv[:, j], ckv[geo.n_ctx:].reshape(nb_lat, l_lat, kvr)], axis=1)
            kpe_lat = jnp.concatenate([cache_kpe[:, j], kpe[geo.n_ctx:].reshape(nb_lat, l_lat, rope)], axis=1)
            parts = []
            for (seq_len, row0, nb), (kc, kp) in zip(passes, ((ckv_ctx, kpe_ctx), (ckv_lat, kpe_lat))):
                kk, vv = mla_expand_kv(kc, kp, wkn, wkv, heads, nope, rope, vdim)
                parts.append(mla_attention(q, kk, vv, seq_len, row0, nb, vdim))
            x = out_proj_residual(geo, tuple(parts), x, mod, gn, mla_w_o[j].astype(BF16), zero_bias)
        x = ffn_residual(geo, x, mod, gn, ffn_w_gate[i].astype(BF16), ffn_w_up[i].astype(BF16),
                         ffn_w_down[i].astype(BF16), split=(i == depth - 1))

    y_prompt = x[0].reshape(nb_ctx, l_ctx, d)
    y_sample = x[1].reshape(nb_lat, l_lat, d)
    return (y_prompt, y_sample, jnp.stack(new_ckv, axis=1), jnp.stack(new_kpe, axis=1))
```

```python
import functools
import math

import jax
import jax.numpy as jnp
from jax import lax
from jax.experimental import pallas as pl
from jax.experimental.pallas import tpu as pltpu

F32 = jnp.float32
BF16 = jnp.bfloat16

NORM_EPS = 1e-6
GRID_W = 64
ROPE_THETA = 10000.0
HY_DECAY_TARGET = 1e-2
HY_FAST_PCT = 0.3
HY_SLOW_PCT = 1.5

V7X_VMEM_LIMIT_BYTES = 60000 * 1024
LANE = 128
SUBLANES = 8

TM = 512
DC = 256


def _dot(a, b):
    return jnp.dot(a, b, preferred_element_type=F32)


def _dot_hi(a, b):
    return jnp.dot(a, b, preferred_element_type=F32, precision=lax.Precision.HIGHEST)


def _rms(x):
    return x * lax.rsqrt(jnp.mean(x * x, axis=-1, keepdims=True) + NORM_EPS)


def _sigmoid(x):
    return 1.0 / (1.0 + jnp.exp(-x))


def _silu(x):
    return x * _sigmoid(x)


def _params(semantics):
    return pltpu.CompilerParams(dimension_semantics=semantics,
                                vmem_limit_bytes=V7X_VMEM_LIMIT_BYTES)


def _resident(shape):
    zeros = (0,) * len(shape)
    return pl.BlockSpec(shape, lambda *_: zeros, pipeline_mode=pl.Buffered(1))


class Geometry:
    def __init__(self, nb_ctx, l_ctx, nb_lat, l_lat, d):
        self.nb_ctx, self.l_ctx, self.nb_lat, self.l_lat, self.d = nb_ctx, l_ctx, nb_lat, l_lat, d
        self.n_ctx = nb_ctx * l_ctx
        self.n_lat = nb_lat * l_lat
        self.n = self.n_ctx + self.n_lat
        assert self.n_ctx % TM == 0 and l_lat % TM == 0 and self.n_ctx % l_lat == 0

    def cond_of_tile(self, i):
        row = i * TM
        return jnp.where(row < self.n_ctx, 0, 1 + (row - self.n_ctx) // self.l_lat)


def _mod_kernel(c_ref, w_ref, b_ref, o_ref):
    a = _silu(c_ref[...]).astype(BF16)
    o_ref[...] = _dot(a, w_ref[...].astype(BF16)) + b_ref[...]


def modulation_all(cond, w_mod, b_mod):
    depth, d, d6 = w_mod.shape
    r = cond.shape[0]
    tn = 1536
    return pl.pallas_call(
        _mod_kernel,
        out_shape=jax.ShapeDtypeStruct((depth, r, d6), F32),
        grid=(depth, d6 // tn),
        in_specs=[pl.BlockSpec((r, d), lambda l, j: (0, 0)),
                  pl.BlockSpec((None, d, tn), lambda l, j: (l, 0, j)),
                  pl.BlockSpec((None, 1, tn), lambda l, j: (l, 0, j))],
        out_specs=pl.BlockSpec((None, r, tn), lambda l, j: (l, 0, j)),
        compiler_params=_params(("parallel", "parallel")),
        name="modulation",
    )(cond, w_mod, b_mod.reshape(depth, 1, d6))


def _stream_args(op):
    return list(op) if isinstance(op, tuple) else [op]


def _stream_specs(geo, op, width):
    if not isinstance(op, tuple):
        return [pl.BlockSpec((TM, width), lambda i: (i, 0))]
    nct = geo.n_ctx // TM
    return [pl.BlockSpec((TM, width), lambda i: (jnp.minimum(i, nct - 1), 0)),
            pl.BlockSpec((TM, width), lambda i: (jnp.maximum(i - nct, 0), 0))]


def _stream_tile(refs, nct):
    if len(refs) == 1:
        return refs[0][...]
    return jnp.where(pl.program_id(0) < nct, refs[0][...], refs[1][...])


def _prenorm(x, mod_ref, gn_ref, which):
    g = gn_ref[2 * which:2 * which + 1, :]
    sh = mod_ref[3 * which:3 * which + 1, :]
    sc = mod_ref[3 * which + 1:3 * which + 2, :]
    return ((_rms(x) * g) * (1.0 + sc) + sh).astype(BF16)


def _row_specs(geo, x):
    d = geo.d
    return _stream_specs(geo, x, d) + [
        pl.BlockSpec((None, 6, d), lambda i: (geo.cond_of_tile(i), 0, 0)),
        pl.BlockSpec((4, d), lambda i: (0, 0))]


def _row_kernel(body, geo, x):
    nx, nct = len(_stream_args(x)), geo.n_ctx // TM

    def kern(*refs):
        body(_stream_tile(refs[:nx], nct), *refs[nx:])

    return kern


def _hy_proj_kernel(x, mod_ref, gn_ref, w_ref, b_ref, o_ref):
    h = _prenorm(x, mod_ref, gn_ref, 0)
    step = 512
    for c in range(0, o_ref.shape[1], step):
        o_ref[:, c:c + step] = (_dot(h, w_ref[:, c:c + step]) + b_ref[:, c:c + step]).astype(BF16)


def hyena_in_proj(geo, x, mod, gn, w_in, b_in):
    d, d3 = w_in.shape
    return pl.pallas_call(
        _row_kernel(_hy_proj_kernel, geo, x),
        out_shape=jax.ShapeDtypeStruct((geo.n, d3), BF16),
        grid=(geo.n // TM,),
        in_specs=_row_specs(geo, x) + [_resident((d, d3)), _resident((1, d3))],
        out_specs=pl.BlockSpec((TM, d3), lambda i: (i, 0)),
        compiler_params=_params(("parallel",)),
        name="hyena_in_proj",
    )(*_stream_args(x), mod, gn, w_in, b_in.reshape(1, d3))


def _cf_proj_kernel(x, mod_ref, gn_ref, w_ref, b_ref, o_ref):
    h = _prenorm(x, mod_ref, gn_ref, 0)
    d = o_ref.shape[1]
    step = 512
    for c in range(0, d, step):
        a = _dot(h, w_ref[:, c:c + step]) + b_ref[:, c:c + step]
        g = _dot(h, w_ref[:, d + c:d + c + step]) + b_ref[:, d + c:d + c + step]
        o_ref[:, c:c + step] = (a * _sigmoid(g)).astype(BF16)


def conformer_in_proj(geo, x, mod, gn, w_pw1, b_pw1):
    d, d2 = w_pw1.shape
    return pl.pallas_call(
        _row_kernel(_cf_proj_kernel, geo, x),
        out_shape=jax.ShapeDtypeStruct((geo.n, d), BF16),
        grid=(geo.n // TM,),
        in_specs=_row_specs(geo, x) + [_resident((d, d2)), _resident((1, d2))],
        out_specs=pl.BlockSpec((TM, d), lambda i: (i, 0)),
        compiler_params=_params(("parallel",)),
        name="conformer_in_proj",
    )(*_stream_args(x), mod, gn, w_pw1, b_pw1.reshape(1, d2))


def _sc_proj_kernel(x, mod_ref, gn_ref, w_ref, bg_ref, q_ref):
    h = _prenorm(x, mod_ref, gn_ref, 0)
    d = bg_ref.shape[1]
    step = 512
    for c in range(0, d, step):
        bg_ref[:, c:c + step] = _dot(h, w_ref[:, c:c + step]).astype(BF16)
        cg = _dot(h, w_ref[:, d + c:d + c + step])
        hv = _dot(h, w_ref[:, 2 * d + c:2 * d + c + step])
        q_ref[:, c:c + step] = (cg * hv).astype(BF16)


def shortconv_in_proj(geo, x, mod, gn, w_in):
    d, d3 = w_in.shape
    out = jax.ShapeDtypeStruct((geo.n, d), BF16)
    return pl.pallas_call(
        _row_kernel(_sc_proj_kernel, geo, x),
        out_shape=(out, out),
        grid=(geo.n // TM,),
        in_specs=_row_specs(geo, x) + [_resident((d, d3))],
        out_specs=(pl.BlockSpec((TM, d), lambda i: (i, 0)), pl.BlockSpec((TM, d), lambda i: (i, 0))),
        compiler_params=_params(("parallel",)),
        name="shortconv_in_proj",
    )(*_stream_args(x), mod, gn, w_in)


def _layer_tail_kernel(a, x, mod_ref, gn_ref, wo_ref, bo_ref, lg_ref, lb_ref, wg_ref, wu_ref, wd_ref,
                       *rest, nct, layer_norm):
    act_ref = rest[-1]
    if layer_norm:
        u = a.astype(F32)
        mu = jnp.mean(u, axis=-1, keepdims=True)
        uc = u - mu
        var = jnp.mean(uc * uc, axis=-1, keepdims=True)
        a = _silu((uc * lax.rsqrt(var + NORM_EPS)) * lg_ref[...] + lb_ref[...]).astype(BF16)
    y = _dot(a, wo_ref[...]) + bo_ref[...]
    x = x + mod_ref[2:3, :] * (_rms(y) * gn_ref[1:2, :])
    h = _prenorm(x, mod_ref, gn_ref, 1)
    step = 256
    for c in range(0, act_ref.shape[1], step):
        g = _dot(h, wg_ref[:, c:c + step])
        u = _dot(h, wu_ref[:, c:c + step])
        act_ref[:, c:c + step] = (_silu(g) * u).astype(BF16)
    y = _dot(act_ref[...], wd_ref[...])
    out = x + mod_ref[5:6, :] * (_rms(y) * gn_ref[3:4, :])
    if len(rest) == 2:
        rest[0][...] = out
    else:
        i = pl.program_id(0)

        @pl.when(i < nct)
        def _():
            rest[0][...] = out

        @pl.when(i >= nct)
        def _():
            rest[1][...] = out


def layer_tail(geo, a, x, mod, gn, w_out, b_out, ln, wg, wu, wd, split=False):
    d, dff = wg.shape
    na, nx, nct = len(_stream_args(a)), len(_stream_args(x)), geo.n_ctx // TM
    body = functools.partial(_layer_tail_kernel, nct=nct, layer_norm=ln is not None)

    def kern(*refs):
        body(_stream_tile(refs[:na], nct), _stream_tile(refs[na:na + nx], nct), *refs[na + nx:])

    if ln is None:
        ln = (jnp.ones((d,), F32), jnp.zeros((d,), F32))
    if split:
        out_shape = (jax.ShapeDtypeStruct((geo.n_ctx, d), F32), jax.ShapeDtypeStruct((geo.n_lat, d), F32))
        out_specs = tuple(_stream_specs(geo, (None, None), d))
    else:
        out_shape = jax.ShapeDtypeStruct((geo.n, d), F32)
        out_specs = pl.BlockSpec((TM, d), lambda i: (i, 0))
    return pl.pallas_call(
        kern,
        out_shape=out_shape,
        grid=(geo.n // TM,),
        in_specs=_stream_specs(geo, a, d) + _row_specs(geo, x) + [
            _resident((d, d)), _resident((1, d)), _resident((1, d)), _resident((1, d)),
            _resident((d, dff)), _resident((d, dff)), _resident((dff, d))],
        out_specs=out_specs,
        scratch_shapes=[pltpu.VMEM((TM, dff), BF16)],
        compiler_params=_params(("arbitrary",) if split else ("parallel",)),
        name="layer_tail",
    )(*_stream_args(a), *_stream_args(x), mod, gn, w_out, b_out.reshape(1, d), ln[0].reshape(1, d),
      ln[1].reshape(1, d), wg, wu, wd)


DFT_ROWS = 256


def _dft_table_kernel(cb_ref, sb_ref, o_ref, *, seq_len):
    k0 = pl.program_id(0) * DFT_ROWS
    s = lax.broadcasted_iota(jnp.int32, (1, seq_len), 1)
    ang = ((k0 * s) & (2 * seq_len - 1)).astype(F32) * (math.pi / seq_len)
    ca, sa = jnp.cos(ang), jnp.sin(ang)
    cb, sb = cb_ref[...], sb_ref[...]
    o_ref[:, 0:seq_len] = (ca * cb - sa * sb).astype(BF16)
    o_ref[:, seq_len:2 * seq_len] = (sa * cb + ca * sb).astype(BF16)


def dft_tables(seq_len):
    assert seq_len & (seq_len - 1) == 0 and seq_len % DFT_ROWS == 0
    k = jnp.arange(seq_len, dtype=jnp.int32)
    idx = (k[:DFT_ROWS, None] * k[None, :]) % (2 * seq_len)
    ang = idx.astype(F32) * (math.pi / seq_len)
    sign = (1 - 2 * (k % 2)).astype(F32)[:, None]
    cvec = jnp.where(k == 0, 0.5 / seq_len, 1.0 / seq_len).astype(F32)[:, None]
    base = pl.BlockSpec((DFT_ROWS, seq_len), lambda i: (0, 0))
    cs = pl.pallas_call(
        functools.partial(_dft_table_kernel, seq_len=seq_len),
        out_shape=jax.ShapeDtypeStruct((seq_len, 2 * seq_len), BF16),
        grid=(seq_len // DFT_ROWS,),
        in_specs=[base, base],
        out_specs=pl.BlockSpec((DFT_ROWS, 2 * seq_len), lambda i: (i, 0)),
        compiler_params=_params(("parallel",)),
        name=f"dft_table_{seq_len}",
    )(jnp.cos(ang), jnp.sin(ang))
    return cs, sign, cvec


def _filter_kernel(zp_ref, zm_ref, tp_ref, tm_ref, keep_ref, w1_ref, b1_ref, fr_ref, w2_ref, b2_ref,
                   w3p_ref, w3m_ref, dl_ref, sign_ref, cvec_ref, cs_ref, hr_ref, hi_ref, hn_ref,
                   hp_ref, hm_ref):
    blk = zp_ref.shape[0]
    absdl = jnp.abs(dl_ref[...])

    @pl.when(pl.program_id(1) == 0)
    def _():
        for z_ref, h_ref in ((zp_ref, hp_ref), (zm_ref, hm_ref)):
            h = jnp.sin(fr_ref[0:1, :] * (_dot_hi(z_ref[...], w1_ref[...]) + b1_ref[...]))
            h_ref[...] = jnp.sin(fr_ref[1:2, :] * (_dot_hi(h, w2_ref[...]) + b2_ref[...]))

    def taps(h_ref, t_ref, w3_ref):
        return _dot_hi(h_ref[...], w3_ref[...]) * jnp.exp(-t_ref[...] * absdl)

    gp = taps(hp_ref, tp_ref, w3p_ref)
    gm = taps(hm_ref, tm_ref, w3m_ref) * keep_ref[...]
    gs = gp + gm
    gd = gp - gm
    cv = cvec_ref[...]
    hr_ref[...] = cv * _dot(cs_ref[:, 0:blk], gs.astype(BF16))
    hi_ref[...] = cv * _dot(cs_ref[:, blk:2 * blk], gd.astype(BF16))
    hn_ref[...] = jnp.sum(gs * sign_ref[...], axis=0, keepdims=True) * (0.5 / blk)


def hyena_block_spectra(seq_len, blk, d, tables, w1, b1, freq, w2, b2, w3):
    cs, sign, cvec = tables
    emb, fw = w1.shape
    bands = (emb - 1) // 2
    nblk = seq_len // blk
    ndl = 2 * nblk - 1
    dd = jnp.arange(ndl, dtype=jnp.int32)[:, None] - (nblk - 1)
    m = jnp.arange(blk, dtype=jnp.int32)[None, :]
    pos_p = jnp.where(dd >= 0, dd * blk + m, -dd * blk - m)
    pos_m = jnp.where(dd >= 1, dd * blk - m, -dd * blk + m)
    keep = jnp.where((m == 0) & (dd != 0), 0.0, 1.0).astype(F32)

    def features(pos):
        pos = pos.reshape(-1, 1).astype(F32)
        t = pos / (seq_len - 1)
        w = 2.0 * math.pi * pos / seq_len
        f = jnp.linspace(1e-4, bands - 1, bands, dtype=F32)[None, :]
        z = jnp.concatenate([t, jnp.cos(f * w), -jnp.sin(f * w)], axis=-1)
        return jnp.pad(z, ((0, 0), (0, LANE - emb))), t

    zp, tp = features(pos_p)
    zm, tm = features(pos_m)
    w1p = jnp.pad(w1, ((0, LANE - emb), (0, 0)))
    deltas = jnp.linspace(math.log(HY_DECAY_TARGET) / HY_SLOW_PCT,
                          math.log(HY_DECAY_TARGET) / HY_FAST_PCT, d, dtype=F32)[None, :]
    nd = d // DC
    full = lambda shape: pl.BlockSpec(shape, lambda i, j: (0,) * len(shape))
    rows = lambda width: pl.BlockSpec((blk, width), lambda i, j: (i, 0))
    w3_p = pl.BlockSpec((fw, DC), lambda i, j: (0, jnp.where(i >= nblk - 1, 0, 2 * nd) + j))
    w3_m = pl.BlockSpec((fw, DC), lambda i, j: (0, jnp.where(i >= nblk, 0, 2 * nd) + j))
    spec_out = pl.BlockSpec((blk, DC), lambda i, j: (i, j))
    return pl.pallas_call(
        _filter_kernel,
        out_shape=(jax.ShapeDtypeStruct((ndl * blk, 2 * d), F32),
                   jax.ShapeDtypeStruct((ndl * blk, 2 * d), F32),
                   jax.ShapeDtypeStruct((ndl, 1, 2 * d), F32)),
        grid=(ndl, 2 * nd),
        in_specs=[rows(LANE), rows(LANE), rows(1), rows(1), rows(1),
                  full((LANE, fw)), full((1, fw)), full((2, fw)), full((fw, fw)), full((1, fw)),
                  w3_p, w3_m,
                  pl.BlockSpec((1, DC), lambda i, j: (0, j % nd)),
                  full((blk, 1)), full((blk, 1)), full((blk, 2 * blk))],
        out_specs=(spec_out, spec_out, pl.BlockSpec((None, 1, DC), lambda i, j: (i, 0, j))),
        scratch_shapes=[pltpu.VMEM((blk, fw), F32), pltpu.VMEM((blk, fw), F32)],
        compiler_params=_params(("parallel", "arbitrary")),
        name=f"hyena_filter_{seq_len}",
    )(zp, zm, tp, tm, keep.reshape(-1, 1), w1p, b1.reshape(1, fw), freq, w2, b2.reshape(1, fw), w3, w3,
      deltas, sign, cvec, cs)


HY_GROUP = 2
HY_BLOCK = 512
HY_STRIP = 32
HY_HALO = 16


def _dwconv3_rows(p_ref, base, r, nrows, seq_len, w_ref, b_ref):
    lo, hi = max(r - HY_HALO, 0), min(r + nrows + HY_HALO, seq_len)
    win = p_ref[base + lo:base + hi, :].astype(F32)
    n = hi - lo
    prev = pltpu.roll(win, 1, 0)
    nxt = pltpu.roll(win, n - 1, 0)
    rows = lax.broadcasted_iota(jnp.int32, (n, 1), 0) + lo
    prev = jnp.where(rows == 0, 0.0, prev)
    nxt = jnp.where(rows == seq_len - 1, 0.0, nxt)
    y = w_ref[0:1, :] * prev + w_ref[1:2, :] * win + w_ref[2:3, :] * nxt + b_ref[...]
    return y[r - lo:r - lo + nrows, :]


def _hyena_mix_kernel(pv_ref, p1_ref, p2_ref, wv_ref, w1_ref, w2_ref, bv_ref, b1_ref, b2_ref,
                      cs_ref, hr1_ref, hi1_ref, hn1_ref, hr2_ref, hi2_ref, hn2_ref, skip_ref,
                      o_ref, z_ref, zb_ref, fa_ref, fb_ref, y_ref, *, seq_len, blk):
    dc = pv_ref.shape[1]
    groups = pv_ref.shape[0] // seq_len
    nblk = seq_len // blk
    tile = lambda a: jnp.concatenate([a] * groups, axis=1)
    rows = lax.broadcasted_iota(jnp.int32, (blk, 1), 0)
    sign = (1 - 2 * (rows % 2)).astype(F32)

    def stage(r, val):
        z_ref[r:r + blk, :] = val
        zb_ref[r:r + blk, :] = val.astype(BF16)
        return jnp.sum(val * sign, axis=0, keepdims=True)

    def block_products(i_out, hr_ref, hi_ref):
        def strip(si, carry):
            r0 = si * HY_STRIP
            for g in range(groups):
                cols = slice(g * dc, (g + 1) * dc)
                yr = yi = None
                for j in range(nblk):
                    h0 = (i_out - j + nblk - 1) * blk
                    hr = hr_ref[pl.ds(h0 + r0, HY_STRIP), :]
                    hi = hi_ref[pl.ds(h0 + r0, HY_STRIP), :]
                    a = fa_ref[pl.ds(j * blk + r0, HY_STRIP), cols]
                    b = fb_ref[pl.ds(j * blk + r0, HY_STRIP), cols]
                    tr, ti = a * hr - b * hi, a * hi + b * hr
                    yr, yi = (tr, ti) if yr is None else (yr + tr, yi + ti)
                y_ref[pl.ds(r0, HY_STRIP), cols] = yr.astype(BF16)
                y_ref[pl.ds(blk + r0, HY_STRIP), cols] = yi.astype(BF16)
            return carry

        for si in range(blk // HY_STRIP):
            strip(si, 0)

    def transform(hr_ref, hi_ref, hn_ref, skip, mult_ref, mw_ref, mb_ref, nyqs, emit):
        for j in range(nblk):
            r = j * blk
            fa_ref[r:r + blk, :] = _dot(cs_ref[:, 0:blk], zb_ref[r:r + blk, :])
            fb_ref[r:r + blk, :] = _dot(cs_ref[:, blk:2 * blk], zb_ref[r:r + blk, :])
        skip_t = tile(skip)
        out = []
        for i in range(nblk):
            r = i * blk
            block_products(i, hr_ref, hi_ref)
            nyq = None
            for j in range(nblk):
                term = nyqs[j] * tile(hn_ref[i - j + nblk - 1])
                nyq = term if nyq is None else nyq + term
            y = _dot(cs_ref[...], y_ref[...]) + sign * nyq + z_ref[r:r + blk, :] * skip_t
            mult = jnp.concatenate([_dwconv3_rows(mult_ref, g * seq_len, r, blk, seq_len, mw_ref, mb_ref)
                                    for g in range(groups)], axis=1)
            out.append(emit(r, mult * y))
        return out

    nyqs = []
    for j in range(nblk):
        v = jnp.concatenate([_dwconv3_rows(pv_ref, g * seq_len, j * blk, blk, seq_len, wv_ref, bv_ref)
                             for g in range(groups)], axis=1)
        nyqs.append(stage(j * blk, v))

    nyqs = transform(hr1_ref, hi1_ref, hn1_ref, skip_ref[0:1, :], p1_ref, w1_ref, b1_ref, nyqs, stage)

    def write_out(r, val):
        for g in range(groups):
            o_ref[g * seq_len + r:g * seq_len + r + blk, :] = val[:, g * dc:(g + 1) * dc].astype(BF16)

    transform(hr2_ref, hi2_ref, hn2_ref, skip_ref[1:2, :], p2_ref, w2_ref, b2_ref, nyqs, write_out)


def hyena_mix(geo, p, conv_w, conv_b, skip, tables, spectrum, seq_len, row0, nb):
    d = geo.d
    cs = tables[0]
    hr, hi, hn = spectrum
    blk = cs.shape[0]
    ndl = 2 * (seq_len // blk) - 1
    nd = d // DC
    rows = HY_GROUP * seq_len
    width = HY_GROUP * DC
    assert nb % HY_GROUP == 0 and row0 % rows == 0 and seq_len % blk == 0 and blk % HY_STRIP == 0
    blk0 = row0 // rows
    pspec = lambda part: pl.BlockSpec((rows, DC), lambda c, b: (blk0 + b, part * nd + c))
    wspec = lambda part: pl.BlockSpec((3, DC), lambda c, b: (0, part * nd + c))
    bspec = lambda part: pl.BlockSpec((1, DC), lambda c, b: (0, part * nd + c))
    hspec = lambda order: pl.BlockSpec((ndl * blk, DC), lambda c, b: (0, order * nd + c),
                                       pipeline_mode=pl.Buffered(1))
    nspec = lambda order: pl.BlockSpec((ndl, 1, DC), lambda c, b: (0, 0, order * nd + c))
    return pl.pallas_call(
        functools.partial(_hyena_mix_kernel, seq_len=seq_len, blk=blk),
        out_shape=jax.ShapeDtypeStruct((nb * seq_len, d), BF16),
        grid=(nd, nb // HY_GROUP),
        in_specs=[pspec(0), pspec(1), pspec(2), wspec(0), wspec(1), wspec(2), bspec(0), bspec(1), bspec(2),
                  _resident((blk, 2 * blk)),
                  hspec(0), hspec(0), nspec(0), hspec(1), hspec(1), nspec(1),
                  pl.BlockSpec((2, DC), lambda c, b: (0, c))],
        out_specs=pl.BlockSpec((rows, DC), lambda c, b: (b, c)),
        scratch_shapes=[pltpu.VMEM((seq_len, width), F32), pltpu.VMEM((seq_len, width), BF16),
                        pltpu.VMEM((seq_len, width), F32), pltpu.VMEM((seq_len, width), F32),
                        pltpu.VMEM((2 * blk, width), BF16)],
        compiler_params=_params(("arbitrary", "arbitrary")),
        name=f"hyena_mix_{seq_len}",
    )(p, p, p, conv_w, conv_w, conv_w, conv_b, conv_b, conv_b, cs,
      hr, hi, hn, hr, hi, hn, skip)


CONV_ROWS = 64
CONV_PAD = 16


def _stage_padded(src, pad_ref):
    n = src.shape[0]
    zeros = jnp.zeros((CONV_PAD, pad_ref.shape[1]), F32)
    pad_ref[0:CONV_PAD, :] = zeros
    pad_ref[CONV_PAD + n:2 * CONV_PAD + n, :] = zeros
    pad_ref[CONV_PAD:CONV_PAD + n, :] = src


def _dwconv_chunks(pad_ref, w_ref, n, ktaps, emit):
    half = ktaps // 2
    lo = (CONV_PAD - half) // SUBLANES * SUBLANES
    hi = -(-(CONV_PAD + half + CONV_ROWS) // SUBLANES) * SUBLANES
    rows = hi - lo

    def chunk(ci, carry):
        r0 = pl.multiple_of(ci * CONV_ROWS, CONV_ROWS)
        window = pad_ref[pl.ds(pl.multiple_of(r0 + lo, SUBLANES), rows), :]
        acc = None
        for k in range(ktaps):
            off = CONV_PAD - half + k - lo
            win = window if off == 0 else pltpu.roll(window, rows - off, 0)
            term = w_ref[k:k + 1, :] * win[0:CONV_ROWS, :]
            acc = term if acc is None else acc + term
        emit(r0, acc)
        return carry

    lax.fori_loop(0, n // CONV_ROWS, chunk, 0)


def _cf_conv_kernel(u_ref, w_ref, b_ref, o_ref, pad_ref):
    n = u_ref.shape[0]
    _stage_padded(u_ref[...].astype(F32), pad_ref)
    bias = b_ref[...]

    def emit(r0, acc):
        o_ref[pl.ds(r0, CONV_ROWS), :] = (acc + bias).astype(BF16)

    _dwconv_chunks(pad_ref, w_ref, n, w_ref.shape[0], emit)


def conformer_dwconv(geo, u, w, b, seq_len, row0, nb):
    d = geo.d
    dc = 256
    ktaps = w.shape[0]
    blk0 = row0 // seq_len
    return pl.pallas_call(
        _cf_conv_kernel,
        out_shape=jax.ShapeDtypeStruct((nb * seq_len, d), BF16),
        grid=(nb, d // dc),
        in_specs=[pl.BlockSpec((seq_len, dc), lambda b_, c: (blk0 + b_, c)),
                  pl.BlockSpec((ktaps, dc), lambda b_, c: (0, c)),
                  pl.BlockSpec((1, dc), lambda b_, c: (0, c))],
        out_specs=pl.BlockSpec((seq_len, dc), lambda b_, c: (b_, c)),
        scratch_shapes=[pltpu.VMEM((seq_len + 2 * CONV_PAD, dc), F32)],
        compiler_params=_params(("parallel", "parallel")),
        name=f"conformer_dwconv_{seq_len}",
    )(u, w, b.reshape(1, d))


def _sc_conv_kernel(q_ref, bg_ref, w_ref, o_ref, pad_ref):
    n = q_ref.shape[0]
    _stage_padded(q_ref[...].astype(F32), pad_ref)

    def emit(r0, acc):
        gate = bg_ref[pl.ds(r0, CONV_ROWS), :].astype(F32)
        o_ref[pl.ds(r0, CONV_ROWS), :] = (gate * acc).astype(BF16)

    _dwconv_chunks(pad_ref, w_ref, n, w_ref.shape[0], emit)


def shortconv_gate(geo, q, bg, w, seq_len, row0, nb):
    d = geo.d
    dc = 256
    ktaps = w.shape[0]
    blk0 = row0 // seq_len
    seq = pl.BlockSpec((seq_len, dc), lambda b_, c: (blk0 + b_, c))
    return pl.pallas_call(
        _sc_conv_kernel,
        out_shape=jax.ShapeDtypeStruct((nb * seq_len, d), BF16),
        grid=(nb, d // dc),
        in_specs=[seq, seq, pl.BlockSpec((ktaps, dc), lambda b_, c: (0, c))],
        out_specs=pl.BlockSpec((seq_len, dc), lambda b_, c: (b_, c)),
        scratch_shapes=[pltpu.VMEM((seq_len + 2 * CONV_PAD, dc), F32)],
        compiler_params=_params(("parallel", "parallel")),
        name=f"shortconv_gate_{seq_len}",
    )(q, bg, w)


def _mla_proj_kernel(x, mod_ref, gn_ref, wdq_ref, gq_ref, wun_ref, wup_ref, wus_ref,
                     wkc_ref, wkp_ref, wks_ref, gkv_ref, cosq_ref, sinq_ref,
                     q_ref, ckv_ref, kpe_ref, *, heads, nope, rope, qscale):
    h = _prenorm(x, mod_ref, gn_ref, 0)
    cq = (_rms(_dot(h, wdq_ref[...])) * gq_ref[...]).astype(BF16)
    qn = _dot(cq, wun_ref[...]) * qscale
    cos2, sin2 = cosq_ref[...], sinq_ref[...]
    cosq, sinq = cos2 * qscale, sin2 * qscale
    for pair in range(heads // 2):
        lo = pair * 2 * rope
        qp = (_dot(cq, wup_ref[:, lo:lo + 2 * rope]) * cosq
              + _dot(cq, wus_ref[:, lo:lo + 2 * rope]) * sinq)
        for j in range(2):
            hd = 2 * pair + j
            q_ref[hd, :, 0:nope] = qn[:, hd * nope:(hd + 1) * nope].astype(BF16)
            q_ref[hd, :, nope:nope + rope] = qp[:, j * rope:(j + 1) * rope].astype(BF16)
    ckv_ref[...] = _rms(_dot(h, wkc_ref[...])) * gkv_ref[...]
    kpe_ref[...] = (_dot(h, wkp_ref[...]) * cos2[:, 0:rope] + _dot(h, wks_ref[...]) * sin2[:, 0:rope])


def mla_project(geo, x, mod, gn, wdq, gq, wun, wup, wus, wkc, wkp, wks, gkv, cos2, sin2, heads, nope, rope,
                qscale):
    d = geo.d
    qr, kvr = wdq.shape[1], wkc.shape[1]
    row = lambda w: pl.BlockSpec((TM, w), lambda i: (i, 0))
    body = functools.partial(_mla_proj_kernel, heads=heads, nope=nope, rope=rope, qscale=qscale)
    return pl.pallas_call(
        _row_kernel(body, geo, x),
        out_shape=(jax.ShapeDtypeStruct((heads, geo.n, nope + rope), BF16),
                   jax.ShapeDtypeStruct((geo.n, kvr), F32),
                   jax.ShapeDtypeStruct((geo.n, rope), F32)),
        grid=(geo.n // TM,),
        in_specs=_row_specs(geo, x) + [_resident((d, qr)), _resident((1, qr)),
                                    _resident((qr, heads * nope)), _resident((qr, heads * rope)),
                                    _resident((qr, heads * rope)),
                                    _resident((d, kvr)), _resident((d, rope)), _resident((d, rope)),
                                    _resident((1, kvr)), row(2 * rope), row(2 * rope)],
        out_specs=(pl.BlockSpec((heads, TM, nope + rope), lambda i: (0, i, 0)), row(kvr), row(rope)),
        compiler_params=_params(("parallel",)),
        name="mla_project",
    )(*_stream_args(x), mod, gn, wdq, gq.reshape(1, qr), wun, wup, wus, wkc, wkp, wks, gkv.reshape(1, kvr),
      cos2, sin2)


KV_ROWS = 256


def _mla_kv_kernel(*refs, heads, nope, rope, past_tiles):
    ckv_ref, kpe_ref, wn_ref, wv_ref, k_ref, v_ref = refs[-6:]
    ckv, kpe = ckv_ref[...], kpe_ref[...]
    if past_tiles:
        cached = pl.program_id(1) < past_tiles
        ckv = jnp.where(cached, refs[0][...], ckv)
        kpe = jnp.where(cached, refs[1][...], kpe)
    c = ckv.astype(BF16)
    kn = _dot(c, wn_ref[...])
    v_ref[...] = _dot(c, wv_ref[...]).astype(BF16)
    kp = kpe.astype(BF16)
    for hd in range(heads):
        k_ref[hd, :, 0:nope] = kn[:, hd * nope:(hd + 1) * nope].astype(BF16)
        k_ref[hd, :, nope:nope + rope] = kp


def mla_expand_kv(ckv, kpe, cache, layer, seq_len, row0, nb, wn, wv, heads, nope, rope, vdim):
    r = ckv.shape[1]
    past = 0 if cache is None else cache[0].shape[2]
    assert past % KV_ROWS == 0 and seq_len % KV_ROWS == 0 and row0 % KV_ROWS == 0
    pt, st, blk0 = past // KV_ROWS, seq_len // KV_ROWS, row0 // KV_ROWS
    lk = past + seq_len
    new_map = lambda b, t: (blk0 + b * st + jnp.maximum(t - pt, 0), 0)
    specs = [pl.BlockSpec((KV_ROWS, r), new_map), pl.BlockSpec((KV_ROWS, rope), new_map),
             _resident((r, heads * nope)), _resident((r, heads * vdim))]
    args = [ckv, kpe, wn, wv]
    if cache is not None:
        old_map = lambda b, t: (b, layer, jnp.minimum(t, pt - 1), 0)
        specs = [pl.BlockSpec((None, None, KV_ROWS, r), old_map),
                 pl.BlockSpec((None, None, KV_ROWS, rope), old_map)] + specs
        args = list(cache) + args
    return pl.pallas_call(
        functools.partial(_mla_kv_kernel, heads=heads, nope=nope, rope=rope, past_tiles=pt),
        out_shape=(jax.ShapeDtypeStruct((nb, heads, lk, nope + rope), BF16),
                   jax.ShapeDtypeStruct((nb, lk, heads * vdim), BF16)),
        grid=(nb, lk // KV_ROWS),
        in_specs=specs,
        out_specs=(pl.BlockSpec((None, heads, KV_ROWS, nope + rope), lambda b, t: (b, 0, t, 0)),
                   pl.BlockSpec((None, KV_ROWS, heads * vdim), lambda b, t: (b, t, 0))),
        compiler_params=_params(("parallel", "arbitrary")),
        name=f"mla_expand_kv_{lk}",
    )(*args)


ATTN_ROWS = 256


def _attn_kernel(q_ref, k_ref, v_ref, o_ref, *, vdim):
    hg, tq, _ = q_ref.shape
    for h in range(hg):
        k = k_ref[h]
        v = v_ref[:, h * vdim:(h + 1) * vdim]
        for r in range(0, tq, ATTN_ROWS):
            s = lax.dot_general(q_ref[h, r:r + ATTN_ROWS, :], k, (((1,), (1,)), ((), ())),
                                preferred_element_type=F32)
            m = jnp.max(s, axis=-1, keepdims=True)
            p = jnp.exp2(s - m)
            l = jnp.sum(p, axis=-1, keepdims=True)
            o = _dot(p.astype(BF16), v)
            o_ref[r:r + ATTN_ROWS, h * vdim:(h + 1) * vdim] = (o * (1.0 / l)).astype(BF16)


def mla_attention(q, k, v, seq_len, row0, nb, vdim):
    heads, _, dq = q.shape
    lk = k.shape[2]
    hg = max(1, min(heads, 2048 // seq_len))
    blk0 = row0 // seq_len
    return pl.pallas_call(
        functools.partial(_attn_kernel, vdim=vdim),
        out_shape=jax.ShapeDtypeStruct((nb * seq_len, heads * vdim), BF16),
        grid=(nb, heads // hg),
        in_specs=[pl.BlockSpec((hg, seq_len, dq), lambda b, h: (h, blk0 + b, 0)),
                  pl.BlockSpec((None, hg, lk, dq), lambda b, h: (b, h, 0, 0)),
                  pl.BlockSpec((None, lk, hg * vdim), lambda b, h: (b, 0, h))],
        out_specs=pl.BlockSpec((seq_len, hg * vdim), lambda b, h: (b, h)),
        compiler_params=_params(("parallel", "parallel")),
        name=f"mla_attention_{seq_len}",
    )(q, k, v)


def rope_tables(geo, rope):
    nf = rope // 4
    rows = geo.l_lat // GRID_W
    row = jnp.repeat(jnp.arange(rows, dtype=F32), GRID_W)
    col = jnp.tile(jnp.arange(GRID_W, dtype=F32), rows)
    inv = jnp.exp(-math.log(ROPE_THETA) * jnp.arange(nf, dtype=F32) * (4.0 / rope))
    ar, ac = row[:, None] * inv, col[:, None] * inv
    cos = jnp.concatenate([jnp.cos(ar), jnp.cos(ar), jnp.cos(ac), jnp.cos(ac)], axis=-1)
    sin = jnp.concatenate([-jnp.sin(ar), jnp.sin(ar), -jnp.sin(ac), jnp.sin(ac)], axis=-1)
    cos = jnp.concatenate([jnp.ones((geo.n_ctx, rope), F32), jnp.tile(cos, (geo.nb_lat, 1))], axis=0)
    sin = jnp.concatenate([jnp.zeros((geo.n_ctx, rope), F32), jnp.tile(sin, (geo.nb_lat, 1))], axis=0)
    return jnp.tile(cos, (1, 2)), jnp.tile(sin, (1, 2))


def _swap_rope_halves(w, rope):
    nf = rope // 4
    shape = w.shape
    w = w.reshape(shape[:-1] + (shape[-1] // (2 * nf), 2, nf))
    return w[..., ::-1, :].reshape(shape)


def kernel(x_prompt, x_sample, c, cache_ckv, cache_kpe, c_ctx, w_mod, b_mod, norm_g, ffn_w_gate, ffn_w_up, ffn_w_down, hy_w_in, hy_b_in, hy_conv_w, hy_conv_b, hy_f_w1, hy_f_b1, hy_f_freq, hy_f_w2, hy_f_b2, hy_f_w3, hy_skip, hy_w_out, hy_b_out, cf_w_pw1, cf_b_pw1, cf_dw_w, cf_dw_b, cf_ln_g, cf_ln_b, cf_w_pw2, cf_b_pw2, sc_w_in, sc_conv_w, sc_w_out, mla_w_dq, mla_g_q, mla_w_uq, mla_w_dkv, mla_g_kv, mla_w_ukv, mla_w_o):
    nb_ctx, l_ctx, d = x_prompt.shape
    nb_lat, l_lat, _ = x_sample.shape
    geo = Geometry(nb_ctx, l_ctx, nb_lat, l_lat, d)
    depth = w_mod.shape[0]
    n_mixers = 4

    x = (x_prompt.reshape(geo.n_ctx, d), x_sample.reshape(geo.n_lat, d))
    cond = jnp.concatenate([c_ctx[None, :], c], axis=0)
    n_cond = cond.shape[0]
    cond = jnp.pad(cond, ((0, -n_cond % 8), (0, 0)))
    mod_all = modulation_all(cond, w_mod, b_mod)[:, :n_cond].reshape(depth, n_cond, 6, d)

    passes = ((l_ctx, 0, nb_ctx), (l_lat, geo.n_ctx, nb_lat))
    zero_bias = jnp.zeros((d,), F32)
    new_ckv, new_kpe = [], []

    for i in range(depth):
        m, j = i % n_mixers, i // n_mixers
        mod, gn = mod_all[i], norm_g[i]
        if m == 0:
            p = hyena_in_proj(geo, x, mod, gn, hy_w_in[j].astype(BF16), hy_b_in[j])
            parts = []
            for seq_len, row0, nb in passes:
                blk = min(seq_len, HY_BLOCK)
                tables = dft_tables(blk)
                spectrum = hyena_block_spectra(seq_len, blk, d, tables, hy_f_w1[j], hy_f_b1[j], hy_f_freq[j],
                                               hy_f_w2[j], hy_f_b2[j], hy_f_w3[j])
                parts.append(hyena_mix(geo, p, hy_conv_w[j], hy_conv_b[j].reshape(1, -1), hy_skip[j],
                                       tables, spectrum, seq_len, row0, nb))
            a, w_out, b_out, ln = tuple(parts), hy_w_out[j], hy_b_out[j], None
        elif m == 1:
            u = conformer_in_proj(geo, x, mod, gn, cf_w_pw1[j].astype(BF16), cf_b_pw1[j])
            a = tuple(conformer_dwconv(geo, u, cf_dw_w[j], cf_dw_b[j], seq_len, row0, nb)
                      for seq_len, row0, nb in passes)
            w_out, b_out, ln = cf_w_pw2[j], cf_b_pw2[j], (cf_ln_g[j], cf_ln_b[j])
        elif m == 2:
            bg, q = shortconv_in_proj(geo, x, mod, gn, sc_w_in[j].astype(BF16))
            a = tuple(shortconv_gate(geo, q, bg, sc_conv_w[j], seq_len, row0, nb)
                      for seq_len, row0, nb in passes)
            w_out, b_out, ln = sc_w_out[j], zero_bias, None
        else:
            kvr = mla_g_kv.shape[1]
            rope = mla_w_dkv.shape[2] - kvr
            vdim = 128
            heads = mla_w_o.shape[1] // vdim
            nope = mla_w_uq.shape[2] // heads - rope
            wuq = mla_w_uq[j].reshape(-1, heads, nope + rope)
            wun = wuq[:, :, :nope].reshape(-1, heads * nope).astype(BF16)
            wup = wuq[:, :, nope:].reshape(-1, heads * rope)
            wukv = mla_w_ukv[j].reshape(kvr, heads, nope + vdim)
            wkn = wukv[:, :, :nope].reshape(kvr, heads * nope).astype(BF16)
            wkv = wukv[:, :, nope:].reshape(kvr, heads * vdim).astype(BF16)
            wkc, wkp = mla_w_dkv[j][:, :kvr], mla_w_dkv[j][:, kvr:]
            cos2, sin2 = rope_tables(geo, rope)
            q, ckv, kpe = mla_project(geo, x, mod, gn, mla_w_dq[j].astype(BF16), mla_g_q[j], wun,
                                      wup.astype(BF16), _swap_rope_halves(wup, rope).astype(BF16),
                                      wkc.astype(BF16), wkp.astype(BF16),
                                      _swap_rope_halves(wkp, rope).astype(BF16), mla_g_kv[j],
                                      cos2, sin2, heads, nope, rope,
                                      (nope + rope) ** -0.5 * math.log2(math.e))
            new_ckv.append(ckv[:geo.n_ctx].reshape(nb_ctx, l_ctx, kvr))
            new_kpe.append(kpe[:geo.n_ctx].reshape(nb_ctx, l_ctx, rope))
            parts = []
            for (seq_len, row0, nb), cache in zip(passes, (None, (cache_ckv, cache_kpe))):
                kk, vv = mla_expand_kv(ckv, kpe, cache, j, seq_len, row0, nb, wkn, wkv, heads, nope, rope, vdim)
                parts.append(mla_attention(q, kk, vv, seq_len, row0, nb, vdim))
            a, w_out, b_out, ln = tuple(parts), mla_w_o[j], zero_bias, None
        x = layer_tail(geo, a, x, mod, gn, w_out.astype(BF16), b_out, ln, ffn_w_gate[i].astype(BF16),
                       ffn_w_up[i].astype(BF16), ffn_w_down[i].astype(BF16), split=(i == depth - 1))

    y_prompt = x[0].reshape(nb_ctx, l_ctx, d)
    y_sample = x[1].reshape(nb_lat, l_lat, d)
    return (y_prompt, y_sample, jnp.stack(new_ckv, axis=1), jnp.stack(new_kpe, axis=1))
```

```python
import functools
import math

import jax
import jax.numpy as jnp
from jax import lax
from jax.experimental import pallas as pl
from jax.experimental.pallas import tpu as pltpu

F32 = jnp.float32
BF16 = jnp.bfloat16

NORM_EPS = 1e-6
GRID_W = 64
ROPE_THETA = 10000.0
HY_DECAY_TARGET = 1e-2
HY_FAST_PCT = 0.3
HY_SLOW_PCT = 1.5

V7X_VMEM_LIMIT_BYTES = 60000 * 1024
LANE = 128
SUBLANES = 8

TM = 512
DC = 256


def _dot(a, b):
    return jnp.dot(a, b, preferred_element_type=F32)


def _dot_hi(a, b):
    return jnp.dot(a, b, preferred_element_type=F32, precision=lax.Precision.HIGHEST)


def _rms(x):
    return x * lax.rsqrt(jnp.mean(x * x, axis=-1, keepdims=True) + NORM_EPS)


def _sigmoid(x):
    return 1.0 / (1.0 + jnp.exp(-x))


def _silu(x):
    return x * _sigmoid(x)


def _params(semantics):
    return pltpu.CompilerParams(dimension_semantics=semantics,
                                vmem_limit_bytes=V7X_VMEM_LIMIT_BYTES)


def _resident(shape):
    zeros = (0,) * len(shape)
    return pl.BlockSpec(shape, lambda *_: zeros, pipeline_mode=pl.Buffered(1))


class Geometry:
    def __init__(self, nb_ctx, l_ctx, nb_lat, l_lat, d):
        self.nb_ctx, self.l_ctx, self.nb_lat, self.l_lat, self.d = nb_ctx, l_ctx, nb_lat, l_lat, d
        self.n_ctx = nb_ctx * l_ctx
        self.n_lat = nb_lat * l_lat
        self.n = self.n_ctx + self.n_lat
        assert self.n_ctx % TM == 0 and l_lat % TM == 0 and self.n_ctx % l_lat == 0

    def cond_of_tile(self, i):
        row = i * TM
        return jnp.where(row < self.n_ctx, 0, 1 + (row - self.n_ctx) // self.l_lat)


def _mod_kernel(c_ref, w_ref, b_ref, o_ref):
    a = _silu(c_ref[...]).astype(BF16)
    o_ref[...] = _dot(a, w_ref[...].astype(BF16)) + b_ref[...]


def modulation_all(cond, w_mod, b_mod):
    depth, d, d6 = w_mod.shape
    r = cond.shape[0]
    tn = 1536
    return pl.pallas_call(
        _mod_kernel,
        out_shape=jax.ShapeDtypeStruct((depth, r, d6), F32),
        grid=(depth, d6 // tn),
        in_specs=[pl.BlockSpec((r, d), lambda l, j: (0, 0)),
                  pl.BlockSpec((None, d, tn), lambda l, j: (l, 0, j)),
                  pl.BlockSpec((None, 1, tn), lambda l, j: (l, 0, j))],
        out_specs=pl.BlockSpec((None, r, tn), lambda l, j: (l, 0, j)),
        compiler_params=_params(("parallel", "parallel")),
        name="modulation",
    )(cond, w_mod, b_mod.reshape(depth, 1, d6))


def _stream_args(op):
    return list(op) if isinstance(op, tuple) else [op]


def _stream_specs(geo, op, width):
    if not isinstance(op, tuple):
        return [pl.BlockSpec((TM, width), lambda i: (i, 0))]
    nct = geo.n_ctx // TM
    return [pl.BlockSpec((TM, width), lambda i: (jnp.minimum(i, nct - 1), 0)),
            pl.BlockSpec((TM, width), lambda i: (jnp.maximum(i - nct, 0), 0))]


def _stream_tile(refs, nct):
    if len(refs) == 1:
        return refs[0][...]
    return jnp.where(pl.program_id(0) < nct, refs[0][...], refs[1][...])


def _prenorm(x, mod_ref, gn_ref, which):
    g = gn_ref[2 * which:2 * which + 1, :]
    sh = mod_ref[3 * which:3 * which + 1, :]
    sc = mod_ref[3 * which + 1:3 * which + 2, :]
    return ((_rms(x) * g) * (1.0 + sc) + sh).astype(BF16)


def _row_specs(geo, x):
    d = geo.d
    return _stream_specs(geo, x, d) + [
        pl.BlockSpec((None, 6, d), lambda i: (geo.cond_of_tile(i), 0, 0)),
        pl.BlockSpec((4, d), lambda i: (0, 0))]


def _row_kernel(body, geo, x):
    nx, nct = len(_stream_args(x)), geo.n_ctx // TM

    def kern(*refs):
        body(_stream_tile(refs[:nx], nct), *refs[nx:])

    return kern


def _hy_proj_kernel(x, mod_ref, gn_ref, w_ref, b_ref, o_ref):
    h = _prenorm(x, mod_ref, gn_ref, 0)
    step = 512
    for c in range(0, o_ref.shape[1], step):
        o_ref[:, c:c + step] = (_dot(h, w_ref[:, c:c + step]) + b_ref[:, c:c + step]).astype(BF16)


def hyena_in_proj(geo, x, mod, gn, w_in, b_in):
    d, d3 = w_in.shape
    return pl.pallas_call(
        _row_kernel(_hy_proj_kernel, geo, x),
        out_shape=jax.ShapeDtypeStruct((geo.n, d3), BF16),
        grid=(geo.n // TM,),
        in_specs=_row_specs(geo, x) + [_resident((d, d3)), _resident((1, d3))],
        out_specs=pl.BlockSpec((TM, d3), lambda i: (i, 0)),
        compiler_params=_params(("parallel",)),
        name="hyena_in_proj",
    )(*_stream_args(x), mod, gn, w_in, b_in.reshape(1, d3))


def _cf_proj_kernel(x, mod_ref, gn_ref, w_ref, b_ref, o_ref):
    h = _prenorm(x, mod_ref, gn_ref, 0)
    d = o_ref.shape[1]
    step = 512
    for c in range(0, d, step):
        a = _dot(h, w_ref[:, c:c + step]) + b_ref[:, c:c + step]
        g = _dot(h, w_ref[:, d + c:d + c + step]) + b_ref[:, d + c:d + c + step]
        o_ref[:, c:c + step] = (a * _sigmoid(g)).astype(BF16)


def conformer_in_proj(geo, x, mod, gn, w_pw1, b_pw1):
    d, d2 = w_pw1.shape
    return pl.pallas_call(
        _row_kernel(_cf_proj_kernel, geo, x),
        out_shape=jax.ShapeDtypeStruct((geo.n, d), BF16),
        grid=(geo.n // TM,),
        in_specs=_row_specs(geo, x) + [_resident((d, d2)), _resident((1, d2))],
        out_specs=pl.BlockSpec((TM, d), lambda i: (i, 0)),
        compiler_params=_params(("parallel",)),
        name="conformer_in_proj",
    )(*_stream_args(x), mod, gn, w_pw1, b_pw1.reshape(1, d2))


def _sc_proj_kernel(x, mod_ref, gn_ref, w_ref, bg_ref, q_ref):
    h = _prenorm(x, mod_ref, gn_ref, 0)
    d = bg_ref.shape[1]
    step = 512
    for c in range(0, d, step):
        bg_ref[:, c:c + step] = _dot(h, w_ref[:, c:c + step]).astype(BF16)
        cg = _dot(h, w_ref[:, d + c:d + c + step])
        hv = _dot(h, w_ref[:, 2 * d + c:2 * d + c + step])
        q_ref[:, c:c + step] = (cg * hv).astype(BF16)


def shortconv_in_proj(geo, x, mod, gn, w_in):
    d, d3 = w_in.shape
    out = jax.ShapeDtypeStruct((geo.n, d), BF16)
    return pl.pallas_call(
        _row_kernel(_sc_proj_kernel, geo, x),
        out_shape=(out, out),
        grid=(geo.n // TM,),
        in_specs=_row_specs(geo, x) + [_resident((d, d3))],
        out_specs=(pl.BlockSpec((TM, d), lambda i: (i, 0)), pl.BlockSpec((TM, d), lambda i: (i, 0))),
        compiler_params=_params(("parallel",)),
        name="shortconv_in_proj",
    )(*_stream_args(x), mod, gn, w_in)


TAIL_CHAINS = 2


def _layer_tail_kernel(a, x, mod_ref, gn_ref, wo_ref, bo_ref, lg_ref, lb_ref, wg_ref, wu_ref, wd_ref,
                       *rest, nct, layer_norm):
    acts = rest[-TAIL_CHAINS:]
    outs = rest[:-TAIL_CHAINS]
    rows = a.shape[0] // TAIL_CHAINS
    groups = [slice(i * rows, (i + 1) * rows) for i in range(TAIL_CHAINS)]
    if layer_norm:
        u = a.astype(F32)
        mu = jnp.mean(u, axis=-1, keepdims=True)
        uc = u - mu
        var = jnp.mean(uc * uc, axis=-1, keepdims=True)
        a = _silu((uc * lax.rsqrt(var + NORM_EPS)) * lg_ref[...] + lb_ref[...]).astype(BF16)
    ys = [_dot(a[g, :], wo_ref[...]) + bo_ref[...] for g in groups]
    xs = [x[g, :] + mod_ref[2:3, :] * (_rms(y) * gn_ref[1:2, :]) for g, y in zip(groups, ys)]
    hs = [_prenorm(xg, mod_ref, gn_ref, 1) for xg in xs]
    step = 256
    for c in range(0, acts[0].shape[1], step):
        for h, act_ref in zip(hs, acts):
            gate = _dot(h, wg_ref[:, c:c + step])
            up = _dot(h, wu_ref[:, c:c + step])
            act_ref[:, c:c + step] = (_silu(gate) * up).astype(BF16)
    ys = [_dot(act_ref[...], wd_ref[...]) for act_ref in acts]
    res = [xg + mod_ref[5:6, :] * (_rms(y) * gn_ref[3:4, :]) for xg, y in zip(xs, ys)]

    def write(o_ref):
        for g, val in zip(groups, res):
            o_ref[g, :] = val

    if len(outs) == 1:
        write(outs[0])
    else:
        i = pl.program_id(0)
        pl.when(i < nct)(lambda: write(outs[0]))
        pl.when(i >= nct)(lambda: write(outs[1]))


def layer_tail(geo, a, x, mod, gn, w_out, b_out, ln, wg, wu, wd, layer, split=False):
    _, d, dff = wg.shape
    layer_w = lambda r, c: pl.BlockSpec((None, r, c), lambda i: (layer, 0, 0), pipeline_mode=pl.Buffered(1))
    na, nx, nct = len(_stream_args(a)), len(_stream_args(x)), geo.n_ctx // TM
    body = functools.partial(_layer_tail_kernel, nct=nct, layer_norm=ln is not None)

    def kern(*refs):
        body(_stream_tile(refs[:na], nct), _stream_tile(refs[na:na + nx], nct), *refs[na + nx:])

    if ln is None:
        ln = (jnp.ones((d,), F32), jnp.zeros((d,), F32))
    if split:
        out_shape = (jax.ShapeDtypeStruct((geo.n_ctx, d), F32), jax.ShapeDtypeStruct((geo.n_lat, d), F32))
        out_specs = tuple(_stream_specs(geo, (None, None), d))
    else:
        out_shape = jax.ShapeDtypeStruct((geo.n, d), F32)
        out_specs = pl.BlockSpec((TM, d), lambda i: (i, 0))
    return pl.pallas_call(
        kern,
        out_shape=out_shape,
        grid=(geo.n // TM,),
        in_specs=_stream_specs(geo, a, d) + _row_specs(geo, x) + [
            _resident((d, d)), _resident((1, d)), _resident((1, d)), _resident((1, d)),
            layer_w(d, dff), layer_w(d, dff), layer_w(dff, d)],
        out_specs=out_specs,
        scratch_shapes=[pltpu.VMEM((TM // TAIL_CHAINS, dff), BF16)] * TAIL_CHAINS,
        compiler_params=_params(("arbitrary",) if split else ("parallel",)),
        name="layer_tail",
    )(*_stream_args(a), *_stream_args(x), mod, gn, w_out, b_out.reshape(1, d), ln[0].reshape(1, d),
      ln[1].reshape(1, d), wg, wu, wd)


DFT_ROWS = 256


def _dft_table_kernel(cb_ref, sb_ref, o_ref, *, seq_len):
    k0 = pl.program_id(0) * DFT_ROWS
    s = lax.broadcasted_iota(jnp.int32, (1, seq_len), 1)
    ang = ((k0 * s) & (2 * seq_len - 1)).astype(F32) * (math.pi / seq_len)
    ca, sa = jnp.cos(ang), jnp.sin(ang)
    cb, sb = cb_ref[...], sb_ref[...]
    o_ref[:, 0:seq_len] = (ca * cb - sa * sb).astype(BF16)
    o_ref[:, seq_len:2 * seq_len] = (sa * cb + ca * sb).astype(BF16)


def dft_tables(seq_len):
    assert seq_len & (seq_len - 1) == 0 and seq_len % DFT_ROWS == 0
    k = jnp.arange(seq_len, dtype=jnp.int32)
    idx = (k[:DFT_ROWS, None] * k[None, :]) % (2 * seq_len)
    ang = idx.astype(F32) * (math.pi / seq_len)
    sign = (1 - 2 * (k % 2)).astype(F32)[:, None]
    cvec = jnp.where(k == 0, 0.5 / seq_len, 1.0 / seq_len).astype(F32)[:, None]
    base = pl.BlockSpec((DFT_ROWS, seq_len), lambda i: (0, 0))
    cs = pl.pallas_call(
        functools.partial(_dft_table_kernel, seq_len=seq_len),
        out_shape=jax.ShapeDtypeStruct((seq_len, 2 * seq_len), BF16),
        grid=(seq_len // DFT_ROWS,),
        in_specs=[base, base],
        out_specs=pl.BlockSpec((DFT_ROWS, 2 * seq_len), lambda i: (i, 0)),
        compiler_params=_params(("parallel",)),
        name=f"dft_table_{seq_len}",
    )(jnp.cos(ang), jnp.sin(ang))
    return cs, sign, cvec


def _filter_kernel(zp_ref, zm_ref, tp_ref, tm_ref, keep_ref, w1_ref, b1_ref, fr_ref, w2_ref, b2_ref,
                   w3p_ref, w3m_ref, dl_ref, sign_ref, cvec_ref, cs_ref, hr_ref, hi_ref, hn_ref,
                   hp_ref, hm_ref):
    blk = zp_ref.shape[0]
    absdl = jnp.abs(dl_ref[...])

    @pl.when(pl.program_id(1) == 0)
    def _():
        for z_ref, h_ref in ((zp_ref, hp_ref), (zm_ref, hm_ref)):
            h = jnp.sin(fr_ref[0:1, :] * (_dot_hi(z_ref[...], w1_ref[...]) + b1_ref[...]))
            h_ref[...] = jnp.sin(fr_ref[1:2, :] * (_dot_hi(h, w2_ref[...]) + b2_ref[...]))

    def taps(h_ref, t_ref, w3_ref):
        return _dot_hi(h_ref[...], w3_ref[...]) * jnp.exp(-t_ref[...] * absdl)

    gp = taps(hp_ref, tp_ref, w3p_ref)
    gm = taps(hm_ref, tm_ref, w3m_ref) * keep_ref[...]
    gs = gp + gm
    gd = gp - gm
    cv = cvec_ref[...]
    hr_ref[...] = cv * _dot(cs_ref[:, 0:blk], gs.astype(BF16))
    hi_ref[...] = cv * _dot(cs_ref[:, blk:2 * blk], gd.astype(BF16))
    hn_ref[...] = jnp.sum(gs * sign_ref[...], axis=0, keepdims=True) * (0.5 / blk)


def hyena_block_spectra(seq_len, blk, d, tables, w1, b1, freq, w2, b2, w3):
    cs, sign, cvec = tables
    emb, fw = w1.shape
    bands = (emb - 1) // 2
    nblk = seq_len // blk
    ndl = 2 * nblk - 1
    dd = jnp.arange(ndl, dtype=jnp.int32)[:, None] - (nblk - 1)
    m = jnp.arange(blk, dtype=jnp.int32)[None, :]
    pos_p = jnp.where(dd >= 0, dd * blk + m, -dd * blk - m)
    pos_m = jnp.where(dd >= 1, dd * blk - m, -dd * blk + m)
    keep = jnp.where((m == 0) & (dd != 0), 0.0, 1.0).astype(F32)

    def features(pos):
        pos = pos.reshape(-1, 1).astype(F32)
        t = pos / (seq_len - 1)
        w = 2.0 * math.pi * pos / seq_len
        f = jnp.linspace(1e-4, bands - 1, bands, dtype=F32)[None, :]
        z = jnp.concatenate([t, jnp.cos(f * w), -jnp.sin(f * w)], axis=-1)
        return jnp.pad(z, ((0, 0), (0, LANE - emb))), t

    zp, tp = features(pos_p)
    zm, tm = features(pos_m)
    w1p = jnp.pad(w1, ((0, LANE - emb), (0, 0)))
    deltas = jnp.linspace(math.log(HY_DECAY_TARGET) / HY_SLOW_PCT,
                          math.log(HY_DECAY_TARGET) / HY_FAST_PCT, d, dtype=F32)[None, :]
    fdc = 2 * DC
    nd = d // fdc
    full = lambda shape: pl.BlockSpec(shape, lambda i, j: (0,) * len(shape))
    rows = lambda width: pl.BlockSpec((blk, width), lambda i, j: (i, 0))
    w3_p = pl.BlockSpec((fw, fdc), lambda i, j: (0, jnp.where(i >= nblk - 1, 0, 2 * nd) + j))
    w3_m = pl.BlockSpec((fw, fdc), lambda i, j: (0, jnp.where(i >= nblk, 0, 2 * nd) + j))
    spec_out = pl.BlockSpec((blk, fdc), lambda i, j: (i, j))
    return pl.pallas_call(
        _filter_kernel,
        out_shape=(jax.ShapeDtypeStruct((ndl * blk, 2 * d), F32),
                   jax.ShapeDtypeStruct((ndl * blk, 2 * d), F32),
                   jax.ShapeDtypeStruct((ndl, 1, 2 * d), F32)),
        grid=(ndl, 2 * nd),
        in_specs=[rows(LANE), rows(LANE), rows(1), rows(1), rows(1),
                  full((LANE, fw)), full((1, fw)), full((2, fw)), full((fw, fw)), full((1, fw)),
                  w3_p, w3_m,
                  pl.BlockSpec((1, fdc), lambda i, j: (0, j % nd)),
                  full((blk, 1)), full((blk, 1)), full((blk, 2 * blk))],
        out_specs=(spec_out, spec_out, pl.BlockSpec((None, 1, fdc), lambda i, j: (i, 0, j))),
        scratch_shapes=[pltpu.VMEM((blk, fw), F32), pltpu.VMEM((blk, fw), F32)],
        compiler_params=_params(("parallel", "arbitrary")),
        name=f"hyena_filter_{seq_len}",
    )(zp, zm, tp, tm, keep.reshape(-1, 1), w1p, b1.reshape(1, fw), freq, w2, b2.reshape(1, fw), w3, w3,
      deltas, sign, cvec, cs)


HY_GROUP = 2
HY_BLOCK = 512
HY_STRIP = 16
HY_HALO = 16


def _dwconv3_rows(p_ref, base, r, nrows, seq_len, w_ref, b_ref):
    lo, hi = max(r - HY_HALO, 0), min(r + nrows + HY_HALO, seq_len)
    win = p_ref[base + lo:base + hi, :].astype(F32)
    n = hi - lo
    prev = pltpu.roll(win, 1, 0)
    nxt = pltpu.roll(win, n - 1, 0)
    rows = lax.broadcasted_iota(jnp.int32, (n, 1), 0) + lo
    prev = jnp.where(rows == 0, 0.0, prev)
    nxt = jnp.where(rows == seq_len - 1, 0.0, nxt)
    y = w_ref[0:1, :] * prev + w_ref[1:2, :] * win + w_ref[2:3, :] * nxt + b_ref[...]
    return y[r - lo:r - lo + nrows, :]


def _hyena_mix_kernel(pv_ref, p1_ref, p2_ref, wv_ref, w1_ref, w2_ref, bv_ref, b1_ref, b2_ref,
                      cs_ref, hr1_ref, hi1_ref, hn1_ref, hr2_ref, hi2_ref, hn2_ref, skip_ref,
                      o_ref, z_ref, zb_ref, fa_ref, fb_ref, y_ref, *, seq_len, blk):
    dc = pv_ref.shape[1]
    groups = pv_ref.shape[0] // seq_len
    nblk = seq_len // blk
    tile = lambda a: jnp.concatenate([a] * groups, axis=1)
    rows = lax.broadcasted_iota(jnp.int32, (blk, 1), 0)
    sign = (1 - 2 * (rows % 2)).astype(F32)

    def stage(r, val):
        z_ref[r:r + blk, :] = val
        zb_ref[r:r + blk, :] = val.astype(BF16)
        return jnp.sum(val * sign, axis=0, keepdims=True)

    def block_products(i_out, hr_ref, hi_ref):
        def strip(r0):
            yr, yi = [None] * groups, [None] * groups
            for j in range(nblk):
                h0 = (i_out - j + nblk - 1) * blk
                hr = hr_ref[h0 + r0:h0 + r0 + HY_STRIP, :]
                hi = hi_ref[h0 + r0:h0 + r0 + HY_STRIP, :]
                for g in range(groups):
                    a = fa_ref[j * blk + r0:j * blk + r0 + HY_STRIP, g * dc:(g + 1) * dc]
                    b = fb_ref[j * blk + r0:j * blk + r0 + HY_STRIP, g * dc:(g + 1) * dc]
                    tr, ti = a * hr - b * hi, a * hi + b * hr
                    yr[g], yi[g] = (tr, ti) if yr[g] is None else (yr[g] + tr, yi[g] + ti)
            for g in range(groups):
                y_ref[r0:r0 + HY_STRIP, g * dc:(g + 1) * dc] = yr[g].astype(BF16)
                y_ref[blk + r0:blk + r0 + HY_STRIP, g * dc:(g + 1) * dc] = yi[g].astype(BF16)

        for r0 in range(0, blk, HY_STRIP):
            strip(r0)

    def transform(hr_ref, hi_ref, hn_ref, skip, mult_ref, mw_ref, mb_ref, nyqs, emit):
        for j in range(nblk):
            r = j * blk
            fa_ref[r:r + blk, :] = _dot(cs_ref[:, 0:blk], zb_ref[r:r + blk, :])
            fb_ref[r:r + blk, :] = _dot(cs_ref[:, blk:2 * blk], zb_ref[r:r + blk, :])
        skip_t = tile(skip)
        out = []
        for i in range(nblk):
            r = i * blk
            block_products(i, hr_ref, hi_ref)
            nyq = None
            for j in range(nblk):
                term = nyqs[j] * tile(hn_ref[i - j + nblk - 1])
                nyq = term if nyq is None else nyq + term
            y = _dot(cs_ref[...], y_ref[...]) + sign * nyq + z_ref[r:r + blk, :] * skip_t
            mult = jnp.concatenate([_dwconv3_rows(mult_ref, g * seq_len, r, blk, seq_len, mw_ref, mb_ref)
                                    for g in range(groups)], axis=1)
            out.append(emit(r, mult * y))
        return out

    nyqs = []
    for j in range(nblk):
        v = jnp.concatenate([_dwconv3_rows(pv_ref, g * seq_len, j * blk, blk, seq_len, wv_ref, bv_ref)
                             for g in range(groups)], axis=1)
        nyqs.append(stage(j * blk, v))

    nyqs = transform(hr1_ref, hi1_ref, hn1_ref, skip_ref[0:1, :], p1_ref, w1_ref, b1_ref, nyqs, stage)

    def write_out(r, val):
        for g in range(groups):
            o_ref[g * seq_len + r:g * seq_len + r + blk, :] = val[:, g * dc:(g + 1) * dc].astype(BF16)

    transform(hr2_ref, hi2_ref, hn2_ref, skip_ref[1:2, :], p2_ref, w2_ref, b2_ref, nyqs, write_out)


def hyena_mix(geo, p, conv_w, conv_b, skip, tables, spectrum, seq_len, row0, nb):
    d = geo.d
    cs = tables[0]
    hr, hi, hn = spectrum
    blk = cs.shape[0]
    ndl = 2 * (seq_len // blk) - 1
    nd = d // DC
    group = max(HY_GROUP, 2048 // seq_len)
    rows = group * seq_len
    width = group * DC
    assert nb % group == 0 and row0 % rows == 0 and seq_len % blk == 0 and blk % HY_STRIP == 0
    blk0 = row0 // rows
    pspec = lambda part: pl.BlockSpec((rows, DC), lambda c, b: (blk0 + b, part * nd + c))
    wspec = lambda part: pl.BlockSpec((3, DC), lambda c, b: (0, part * nd + c))
    bspec = lambda part: pl.BlockSpec((1, DC), lambda c, b: (0, part * nd + c))
    hspec = lambda order: pl.BlockSpec((ndl * blk, DC), lambda c, b: (0, order * nd + c),
                                       pipeline_mode=pl.Buffered(1))
    nspec = lambda order: pl.BlockSpec((ndl, 1, DC), lambda c, b: (0, 0, order * nd + c))
    return pl.pallas_call(
        functools.partial(_hyena_mix_kernel, seq_len=seq_len, blk=blk),
        out_shape=jax.ShapeDtypeStruct((nb * seq_len, d), BF16),
        grid=(nd, nb // group),
        in_specs=[pspec(0), pspec(1), pspec(2), wspec(0), wspec(1), wspec(2), bspec(0), bspec(1), bspec(2),
                  _resident((blk, 2 * blk)),
                  hspec(0), hspec(0), nspec(0), hspec(1), hspec(1), nspec(1),
                  pl.BlockSpec((2, DC), lambda c, b: (0, c))],
        out_specs=pl.BlockSpec((rows, DC), lambda c, b: (b, c)),
        scratch_shapes=[pltpu.VMEM((seq_len, width), F32), pltpu.VMEM((seq_len, width), BF16),
                        pltpu.VMEM((seq_len, width), F32), pltpu.VMEM((seq_len, width), F32),
                        pltpu.VMEM((2 * blk, width), BF16)],
        compiler_params=_params(("arbitrary", "arbitrary")),
        name=f"hyena_mix_{seq_len}",
    )(p, p, p, conv_w, conv_w, conv_w, conv_b, conv_b, conv_b, cs,
      hr, hi, hn, hr, hi, hn, skip)


CONV_ROWS = 64
CONV_PAD = 16


def _stage_padded(src, pad_ref):
    n = src.shape[0]
    zeros = jnp.zeros((CONV_PAD, pad_ref.shape[1]), F32)
    pad_ref[0:CONV_PAD, :] = zeros
    pad_ref[CONV_PAD + n:2 * CONV_PAD + n, :] = zeros
    pad_ref[CONV_PAD:CONV_PAD + n, :] = src


def _dwconv_chunks(pad_ref, w_ref, n, ktaps, emit):
    half = ktaps // 2
    lo = (CONV_PAD - half) // SUBLANES * SUBLANES
    hi = -(-(CONV_PAD + half + CONV_ROWS) // SUBLANES) * SUBLANES
    rows = hi - lo

    def chunk(ci, carry):
        r0 = pl.multiple_of(ci * CONV_ROWS, CONV_ROWS)
        window = pad_ref[pl.ds(pl.multiple_of(r0 + lo, SUBLANES), rows), :]
        acc = None
        for k in range(ktaps):
            off = CONV_PAD - half + k - lo
            win = window if off == 0 else pltpu.roll(window, rows - off, 0)
            term = w_ref[k:k + 1, :] * win[0:CONV_ROWS, :]
            acc = term if acc is None else acc + term
        emit(r0, acc)
        return carry

    lax.fori_loop(0, n // CONV_ROWS, chunk, 0)


def _cf_conv_kernel(u_ref, w_ref, b_ref, o_ref, pad_ref):
    n = u_ref.shape[0]
    _stage_padded(u_ref[...].astype(F32), pad_ref)
    bias = b_ref[...]

    def emit(r0, acc):
        o_ref[pl.ds(r0, CONV_ROWS), :] = (acc + bias).astype(BF16)

    _dwconv_chunks(pad_ref, w_ref, n, w_ref.shape[0], emit)


def conformer_dwconv(geo, u, w, b, seq_len, row0, nb):
    d = geo.d
    dc = 256
    ktaps = w.shape[0]
    blk0 = row0 // seq_len
    return pl.pallas_call(
        _cf_conv_kernel,
        out_shape=jax.ShapeDtypeStruct((nb * seq_len, d), BF16),
        grid=(nb, d // dc),
        in_specs=[pl.BlockSpec((seq_len, dc), lambda b_, c: (blk0 + b_, c)),
                  pl.BlockSpec((ktaps, dc), lambda b_, c: (0, c)),
                  pl.BlockSpec((1, dc), lambda b_, c: (0, c))],
        out_specs=pl.BlockSpec((seq_len, dc), lambda b_, c: (b_, c)),
        scratch_shapes=[pltpu.VMEM((seq_len + 2 * CONV_PAD, dc), F32)],
        compiler_params=_params(("parallel", "parallel")),
        name=f"conformer_dwconv_{seq_len}",
    )(u, w, b.reshape(1, d))


def _sc_conv_kernel(q_ref, bg_ref, w_ref, o_ref, pad_ref):
    n = q_ref.shape[0]
    _stage_padded(q_ref[...].astype(F32), pad_ref)

    def emit(r0, acc):
        gate = bg_ref[pl.ds(r0, CONV_ROWS), :].astype(F32)
        o_ref[pl.ds(r0, CONV_ROWS), :] = (gate * acc).astype(BF16)

    _dwconv_chunks(pad_ref, w_ref, n, w_ref.shape[0], emit)


def shortconv_gate(geo, q, bg, w, seq_len, row0, nb):
    d = geo.d
    dc = 256
    ktaps = w.shape[0]
    blk0 = row0 // seq_len
    seq = pl.BlockSpec((seq_len, dc), lambda b_, c: (blk0 + b_, c))
    return pl.pallas_call(
        _sc_conv_kernel,
        out_shape=jax.ShapeDtypeStruct((nb * seq_len, d), BF16),
        grid=(nb, d // dc),
        in_specs=[seq, seq, pl.BlockSpec((ktaps, dc), lambda b_, c: (0, c))],
        out_specs=pl.BlockSpec((seq_len, dc), lambda b_, c: (b_, c)),
        scratch_shapes=[pltpu.VMEM((seq_len + 2 * CONV_PAD, dc), F32)],
        compiler_params=_params(("parallel", "parallel")),
        name=f"shortconv_gate_{seq_len}",
    )(q, bg, w)


def _mla_proj_kernel(x, mod_ref, gn_ref, wdq_ref, gq_ref, wun_ref, wup_ref, wus_ref,
                     wkc_ref, wkp_ref, wks_ref, gkv_ref, cosq_ref, sinq_ref,
                     q_ref, ckv_ref, kpe_ref, *, heads, nope, rope, qscale):
    h = _prenorm(x, mod_ref, gn_ref, 0)
    cq = (_rms(_dot(h, wdq_ref[...])) * gq_ref[...]).astype(BF16)
    qn = _dot(cq, wun_ref[...]) * qscale
    cos2, sin2 = cosq_ref[...], sinq_ref[...]
    cosq, sinq = cos2 * qscale, sin2 * qscale
    for pair in range(heads // 2):
        lo = pair * 2 * rope
        qp = (_dot(cq, wup_ref[:, lo:lo + 2 * rope]) * cosq
              + _dot(cq, wus_ref[:, lo:lo + 2 * rope]) * sinq)
        for j in range(2):
            hd = 2 * pair + j
            q_ref[hd, :, 0:nope] = qn[:, hd * nope:(hd + 1) * nope].astype(BF16)
            q_ref[hd, :, nope:nope + rope] = qp[:, j * rope:(j + 1) * rope].astype(BF16)
    ckv_ref[...] = _rms(_dot(h, wkc_ref[...])) * gkv_ref[...]
    kpe_ref[...] = (_dot(h, wkp_ref[...]) * cos2[:, 0:rope] + _dot(h, wks_ref[...]) * sin2[:, 0:rope])


def mla_project(geo, x, mod, gn, wdq, gq, wun, wup, wus, wkc, wkp, wks, gkv, cos2, sin2, heads, nope, rope,
                qscale):
    d = geo.d
    qr, kvr = wdq.shape[1], wkc.shape[1]
    row = lambda w: pl.BlockSpec((TM, w), lambda i: (i, 0))
    rot = pl.BlockSpec((TM, 2 * rope), lambda i: (_rope_block_of_tile(geo, i), 0))
    body = functools.partial(_mla_proj_kernel, heads=heads, nope=nope, rope=rope, qscale=qscale)
    return pl.pallas_call(
        _row_kernel(body, geo, x),
        out_shape=(jax.ShapeDtypeStruct((heads, geo.n, nope + rope), BF16),
                   jax.ShapeDtypeStruct((geo.n, kvr), F32),
                   jax.ShapeDtypeStruct((geo.n, rope), F32)),
        grid=(geo.n // TM,),
        in_specs=_row_specs(geo, x) + [_resident((d, qr)), _resident((1, qr)),
                                    _resident((qr, heads * nope)), _resident((qr, heads * rope)),
                                    _resident((qr, heads * rope)),
                                    _resident((d, kvr)), _resident((d, rope)), _resident((d, rope)),
                                    _resident((1, kvr)), rot, rot],
        out_specs=(pl.BlockSpec((heads, TM, nope + rope), lambda i: (0, i, 0)), row(kvr), row(rope)),
        compiler_params=_params(("parallel",)),
        name="mla_project",
    )(*_stream_args(x), mod, gn, wdq, gq.reshape(1, qr), wun, wup, wus, wkc, wkp, wks, gkv.reshape(1, kvr),
      cos2, sin2)


KV_ROWS = 256


def _mla_kv_kernel(*refs, heads, nope, rope, past_tiles):
    ckv_ref, kpe_ref, wn_ref, wv_ref, k_ref, v_ref = refs[-6:]
    ckv, kpe = ckv_ref[...], kpe_ref[...]
    if past_tiles:
        cached = pl.program_id(1) < past_tiles
        ckv = jnp.where(cached, refs[0][...], ckv)
        kpe = jnp.where(cached, refs[1][...], kpe)
    c = ckv.astype(BF16)
    kn = _dot(c, wn_ref[...])
    v_ref[...] = _dot(c, wv_ref[...]).astype(BF16)
    kp = kpe.astype(BF16)
    for hd in range(heads):
        k_ref[hd, :, 0:nope] = kn[:, hd * nope:(hd + 1) * nope].astype(BF16)
        k_ref[hd, :, nope:nope + rope] = kp


def mla_expand_kv(ckv, kpe, cache, layer, seq_len, row0, nb, wn, wv, heads, nope, rope, vdim):
    r = ckv.shape[1]
    past = 0 if cache is None else cache[0].shape[2]
    assert past % KV_ROWS == 0 and seq_len % KV_ROWS == 0 and row0 % KV_ROWS == 0
    pt, st, blk0 = past // KV_ROWS, seq_len // KV_ROWS, row0 // KV_ROWS
    lk = past + seq_len
    new_map = lambda b, t: (blk0 + b * st + jnp.maximum(t - pt, 0), 0)
    specs = [pl.BlockSpec((KV_ROWS, r), new_map), pl.BlockSpec((KV_ROWS, rope), new_map),
             _resident((r, heads * nope)), _resident((r, heads * vdim))]
    args = [ckv, kpe, wn, wv]
    if cache is not None:
        old_map = lambda b, t: (b, layer, jnp.minimum(t, pt - 1), 0)
        specs = [pl.BlockSpec((None, None, KV_ROWS, r), old_map),
                 pl.BlockSpec((None, None, KV_ROWS, rope), old_map)] + specs
        args = list(cache) + args
    return pl.pallas_call(
        functools.partial(_mla_kv_kernel, heads=heads, nope=nope, rope=rope, past_tiles=pt),
        out_shape=(jax.ShapeDtypeStruct((nb, heads, lk, nope + rope), BF16),
                   jax.ShapeDtypeStruct((nb, lk, heads * vdim), BF16)),
        grid=(nb, lk // KV_ROWS),
        in_specs=specs,
        out_specs=(pl.BlockSpec((None, heads, KV_ROWS, nope + rope), lambda b, t: (b, 0, t, 0)),
                   pl.BlockSpec((None, KV_ROWS, heads * vdim), lambda b, t: (b, t, 0))),
        compiler_params=_params(("parallel", "arbitrary")),
        name=f"mla_expand_kv_{lk}",
    )(*args)


ATTN_ROWS = 256


def _attn_kernel(q_ref, k_ref, v_ref, o_ref, *, vdim):
    hg, tq, _ = q_ref.shape
    chains = [(h, r) for h in range(hg) for r in range(0, tq, ATTN_ROWS)]

    def scores(h, r):
        return lax.dot_general(q_ref[h, r:r + ATTN_ROWS, :], k_ref[h], (((1,), (1,)), ((), ())),
                               preferred_element_type=F32)

    s_next = scores(*chains[0])
    for i, (h, r) in enumerate(chains):
        s = s_next
        if i + 1 < len(chains):
            s_next = scores(*chains[i + 1])
        m = jnp.max(s, axis=-1, keepdims=True)
        p = jnp.exp2(s - m)
        l = jnp.sum(p, axis=-1, keepdims=True)
        o = _dot(p.astype(BF16), v_ref[:, h * vdim:(h + 1) * vdim])
        o_ref[r:r + ATTN_ROWS, h * vdim:(h + 1) * vdim] = (o * (1.0 / l)).astype(BF16)


def mla_attention(q, k, v, seq_len, row0, nb, vdim):
    heads, _, dq = q.shape
    lk = k.shape[2]
    hg = max(1, min(heads, 2048 // seq_len))
    blk0 = row0 // seq_len
    return pl.pallas_call(
        functools.partial(_attn_kernel, vdim=vdim),
        out_shape=jax.ShapeDtypeStruct((nb * seq_len, heads * vdim), BF16),
        grid=(nb, heads // hg),
        in_specs=[pl.BlockSpec((hg, seq_len, dq), lambda b, h: (h, blk0 + b, 0)),
                  pl.BlockSpec((None, hg, lk, dq), lambda b, h: (b, h, 0, 0)),
                  pl.BlockSpec((None, lk, hg * vdim), lambda b, h: (b, 0, h))],
        out_specs=pl.BlockSpec((seq_len, hg * vdim), lambda b, h: (b, h)),
        compiler_params=_params(("parallel", "parallel")),
        name=f"mla_attention_{seq_len}",
    )(q, k, v)


def rope_tables(geo, rope):
    nf = rope // 4
    rows = geo.l_lat // GRID_W
    row = jnp.repeat(jnp.arange(rows, dtype=F32), GRID_W)
    col = jnp.tile(jnp.arange(GRID_W, dtype=F32), rows)
    inv = jnp.exp(-math.log(ROPE_THETA) * jnp.arange(nf, dtype=F32) * (4.0 / rope))
    ar, ac = row[:, None] * inv, col[:, None] * inv
    cos = jnp.concatenate([jnp.cos(ar), jnp.cos(ar), jnp.cos(ac), jnp.cos(ac)], axis=-1)
    sin = jnp.concatenate([-jnp.sin(ar), jnp.sin(ar), -jnp.sin(ac), jnp.sin(ac)], axis=-1)
    cos = jnp.concatenate([jnp.ones((TM, rope), F32), cos], axis=0)
    sin = jnp.concatenate([jnp.zeros((TM, rope), F32), sin], axis=0)
    return jnp.tile(cos, (1, 2)), jnp.tile(sin, (1, 2))


def _rope_block_of_tile(geo, i):
    nct, per_seq = geo.n_ctx // TM, geo.l_lat // TM
    return jnp.where(i < nct, 0, 1 + (i - nct) % per_seq)


def _swap_rope_halves(w, rope):
    nf = rope // 4
    shape = w.shape
    w = w.reshape(shape[:-1] + (shape[-1] // (2 * nf), 2, nf))
    return w[..., ::-1, :].reshape(shape)


def kernel(x_prompt, x_sample, c, cache_ckv, cache_kpe, c_ctx, w_mod, b_mod, norm_g, ffn_w_gate, ffn_w_up, ffn_w_down, hy_w_in, hy_b_in, hy_conv_w, hy_conv_b, hy_f_w1, hy_f_b1, hy_f_freq, hy_f_w2, hy_f_b2, hy_f_w3, hy_skip, hy_w_out, hy_b_out, cf_w_pw1, cf_b_pw1, cf_dw_w, cf_dw_b, cf_ln_g, cf_ln_b, cf_w_pw2, cf_b_pw2, sc_w_in, sc_conv_w, sc_w_out, mla_w_dq, mla_g_q, mla_w_uq, mla_w_dkv, mla_g_kv, mla_w_ukv, mla_w_o):
    nb_ctx, l_ctx, d = x_prompt.shape
    nb_lat, l_lat, _ = x_sample.shape
    geo = Geometry(nb_ctx, l_ctx, nb_lat, l_lat, d)
    depth = w_mod.shape[0]
    n_mixers = 4

    x = (x_prompt.reshape(geo.n_ctx, d), x_sample.reshape(geo.n_lat, d))
    cond = jnp.concatenate([c_ctx[None, :], c], axis=0)
    n_cond = cond.shape[0]
    cond = jnp.pad(cond, ((0, -n_cond % 8), (0, 0)))
    mod_all = modulation_all(cond, w_mod, b_mod)[:, :n_cond].reshape(depth, n_cond, 6, d)

    passes = ((l_ctx, 0, nb_ctx), (l_lat, geo.n_ctx, nb_lat))
    zero_bias = jnp.zeros((d,), F32)
    ffn_wg, ffn_wu, ffn_wd = ffn_w_gate.astype(BF16), ffn_w_up.astype(BF16), ffn_w_down.astype(BF16)
    new_ckv, new_kpe = [], []

    for i in range(depth):
        m, j = i % n_mixers, i // n_mixers
        mod, gn = mod_all[i], norm_g[i]
        if m == 0:
            p = hyena_in_proj(geo, x, mod, gn, hy_w_in[j].astype(BF16), hy_b_in[j])
            parts = []
            for seq_len, row0, nb in passes:
                blk = min(seq_len, HY_BLOCK)
                tables = dft_tables(blk)
                spectrum = hyena_block_spectra(seq_len, blk, d, tables, hy_f_w1[j], hy_f_b1[j], hy_f_freq[j],
                                               hy_f_w2[j], hy_f_b2[j], hy_f_w3[j])
                parts.append(hyena_mix(geo, p, hy_conv_w[j], hy_conv_b[j].reshape(1, -1), hy_skip[j],
                                       tables, spectrum, seq_len, row0, nb))
            a, w_out, b_out, ln = tuple(parts), hy_w_out[j], hy_b_out[j], None
        elif m == 1:
            u = conformer_in_proj(geo, x, mod, gn, cf_w_pw1[j].astype(BF16), cf_b_pw1[j])
            a = tuple(conformer_dwconv(geo, u, cf_dw_w[j], cf_dw_b[j], seq_len, row0, nb)
                      for seq_len, row0, nb in passes)
            w_out, b_out, ln = cf_w_pw2[j], cf_b_pw2[j], (cf_ln_g[j], cf_ln_b[j])
        elif m == 2:
            bg, q = shortconv_in_proj(geo, x, mod, gn, sc_w_in[j].astype(BF16))
            a = tuple(shortconv_gate(geo, q, bg, sc_conv_w[j], seq_len, row0, nb)
                      for seq_len, row0, nb in passes)
            w_out, b_out, ln = sc_w_out[j], zero_bias, None
        else:
            kvr = mla_g_kv.shape[1]
            rope = mla_w_dkv.shape[2] - kvr
            vdim = 128
            heads = mla_w_o.shape[1] // vdim
            nope = mla_w_uq.shape[2] // heads - rope
            wuq = mla_w_uq[j].reshape(-1, heads, nope + rope)
            wun = wuq[:, :, :nope].reshape(-1, heads * nope).astype(BF16)
            wup = wuq[:, :, nope:].reshape(-1, heads * rope)
            wukv = mla_w_ukv[j].reshape(kvr, heads, nope + vdim)
            wkn = wukv[:, :, :nope].reshape(kvr, heads * nope).astype(BF16)
            wkv = wukv[:, :, nope:].reshape(kvr, heads * vdim).astype(BF16)
            wkc, wkp = mla_w_dkv[j][:, :kvr], mla_w_dkv[j][:, kvr:]
            cos2, sin2 = rope_tables(geo, rope)
            q, ckv, kpe = mla_project(geo, x, mod, gn, mla_w_dq[j].astype(BF16), mla_g_q[j], wun,
                                      wup.astype(BF16), _swap_rope_halves(wup, rope).astype(BF16),
                                      wkc.astype(BF16), wkp.astype(BF16),
                                      _swap_rope_halves(wkp, rope).astype(BF16), mla_g_kv[j],
                                      cos2, sin2, heads, nope, rope,
                                      (nope + rope) ** -0.5 * math.log2(math.e))
            new_ckv.append(ckv[:geo.n_ctx].reshape(nb_ctx, l_ctx, kvr))
            new_kpe.append(kpe[:geo.n_ctx].reshape(nb_ctx, l_ctx, rope))
            parts = []
            for (seq_len, row0, nb), cache in zip(passes, (None, (cache_ckv, cache_kpe))):
                kk, vv = mla_expand_kv(ckv, kpe, cache, j, seq_len, row0, nb, wkn, wkv, heads, nope, rope, vdim)
                parts.append(mla_attention(q, kk, vv, seq_len, row0, nb, vdim))
            a, w_out, b_out, ln = tuple(parts), mla_w_o[j], zero_bias, None
        x = layer_tail(geo, a, x, mod, gn, w_out.astype(BF16), b_out, ln, ffn_wg, ffn_wu, ffn_wd, i,
                       split=(i == depth - 1))

    y_prompt = x[0].reshape(nb_ctx, l_ctx, d)
    y_sample = x[1].reshape(nb_lat, l_lat, d)
    return (y_prompt, y_sample, jnp.stack(new_ckv, axis=1), jnp.stack(new_kpe, axis=1))
```

```python
import functools
import math

import jax
import jax.numpy as jnp
from jax import lax
from jax.experimental import pallas as pl
from jax.experimental.pallas import tpu as pltpu

F32 = jnp.float32
BF16 = jnp.bfloat16

NORM_EPS = 1e-6
GRID_W = 64
ROPE_THETA = 10000.0
HY_DECAY_TARGET = 1e-2
HY_FAST_PCT = 0.3
HY_SLOW_PCT = 1.5

V7X_VMEM_LIMIT_BYTES = 60000 * 1024
LANE = 128
SUBLANES = 8

TM = 512
DC = 256


def _dot(a, b):
    return jnp.dot(a, b, preferred_element_type=F32)


def _dot_hi(a, b):
    return jnp.dot(a, b, preferred_element_type=F32, precision=lax.Precision.HIGHEST)


def _rms(x):
    return x * lax.rsqrt(jnp.mean(x * x, axis=-1, keepdims=True) + NORM_EPS)


def _sigmoid(x):
    return 1.0 / (1.0 + jnp.exp(-x))


def _silu(x):
    return x * _sigmoid(x)


def _params(semantics):
    return pltpu.CompilerParams(dimension_semantics=semantics,
                                vmem_limit_bytes=V7X_VMEM_LIMIT_BYTES)


def _resident(shape):
    zeros = (0,) * len(shape)
    return pl.BlockSpec(shape, lambda *_: zeros, pipeline_mode=pl.Buffered(1))


class Geometry:
    def __init__(self, nb_ctx, l_ctx, nb_lat, l_lat, d):
        self.nb_ctx, self.l_ctx, self.nb_lat, self.l_lat, self.d = nb_ctx, l_ctx, nb_lat, l_lat, d
        self.n_ctx = nb_ctx * l_ctx
        self.n_lat = nb_lat * l_lat
        self.n = self.n_ctx + self.n_lat
        assert self.n_ctx % TM == 0 and l_lat % TM == 0 and self.n_ctx % l_lat == 0

    def cond_of_tile(self, i):
        row = i * TM
        return jnp.where(row < self.n_ctx, 0, 1 + (row - self.n_ctx) // self.l_lat)


def _mod_kernel(c_ref, w_ref, b_ref, o_ref):
    a = _silu(c_ref[...]).astype(BF16)
    o_ref[...] = _dot(a, w_ref[...].astype(BF16)) + b_ref[...]


def modulation_all(cond, w_mod, b_mod):
    depth, d, d6 = w_mod.shape
    r = cond.shape[0]
    tn = 1536
    return pl.pallas_call(
        _mod_kernel,
        out_shape=jax.ShapeDtypeStruct((depth, r, d6), F32),
        grid=(depth, d6 // tn),
        in_specs=[pl.BlockSpec((r, d), lambda l, j: (0, 0)),
                  pl.BlockSpec((None, d, tn), lambda l, j: (l, 0, j)),
                  pl.BlockSpec((None, 1, tn), lambda l, j: (l, 0, j))],
        out_specs=pl.BlockSpec((None, r, tn), lambda l, j: (l, 0, j)),
        compiler_params=_params(("parallel", "parallel")),
        name="modulation",
    )(cond, w_mod, b_mod.reshape(depth, 1, d6))


def _stream_args(op):
    return list(op) if isinstance(op, tuple) else [op]


def _stream_specs(geo, op, width):
    if not isinstance(op, tuple):
        return [pl.BlockSpec((TM, width), lambda i: (i, 0))]
    nct = geo.n_ctx // TM
    return [pl.BlockSpec((TM, width), lambda i: (jnp.minimum(i, nct - 1), 0)),
            pl.BlockSpec((TM, width), lambda i: (jnp.maximum(i - nct, 0), 0))]


def _stream_tile(refs, nct):
    if len(refs) == 1:
        return refs[0][...]
    return jnp.where(pl.program_id(0) < nct, refs[0][...], refs[1][...])


def _prenorm(x, mod_ref, gn_ref, which):
    g = gn_ref[2 * which:2 * which + 1, :]
    sh = mod_ref[3 * which:3 * which + 1, :]
    sc = mod_ref[3 * which + 1:3 * which + 2, :]
    return ((_rms(x) * g) * (1.0 + sc) + sh).astype(BF16)


def _row_specs(geo, x):
    d = geo.d
    return _stream_specs(geo, x, d) + [
        pl.BlockSpec((None, 6, d), lambda i: (geo.cond_of_tile(i), 0, 0)),
        pl.BlockSpec((4, d), lambda i: (0, 0))]


def _row_kernel(body, geo, x):
    nx, nct = len(_stream_args(x)), geo.n_ctx // TM

    def kern(*refs):
        body(_stream_tile(refs[:nx], nct), *refs[nx:])

    return kern


def _hy_proj_kernel(x, mod_ref, gn_ref, w_ref, b_ref, o_ref):
    h = _prenorm(x, mod_ref, gn_ref, 0)
    step = 512
    for c in range(0, o_ref.shape[1], step):
        o_ref[:, c:c + step] = (_dot(h, w_ref[:, c:c + step]) + b_ref[:, c:c + step]).astype(BF16)


def hyena_in_proj(geo, x, mod, gn, w_in, b_in):
    d, d3 = w_in.shape
    return pl.pallas_call(
        _row_kernel(_hy_proj_kernel, geo, x),
        out_shape=jax.ShapeDtypeStruct((geo.n, d3), BF16),
        grid=(geo.n // TM,),
        in_specs=_row_specs(geo, x) + [_resident((d, d3)), _resident((1, d3))],
        out_specs=pl.BlockSpec((TM, d3), lambda i: (i, 0)),
        compiler_params=_params(("parallel",)),
        name="hyena_in_proj",
    )(*_stream_args(x), mod, gn, w_in, b_in.reshape(1, d3))


def _cf_proj_kernel(x, mod_ref, gn_ref, w_ref, b_ref, o_ref):
    h = _prenorm(x, mod_ref, gn_ref, 0)
    d = o_ref.shape[1]
    step = 512
    for c in range(0, d, step):
        a = _dot(h, w_ref[:, c:c + step]) + b_ref[:, c:c + step]
        g = _dot(h, w_ref[:, d + c:d + c + step]) + b_ref[:, d + c:d + c + step]
        o_ref[:, c:c + step] = (a * _sigmoid(g)).astype(BF16)


def conformer_in_proj(geo, x, mod, gn, w_pw1, b_pw1):
    d, d2 = w_pw1.shape
    return pl.pallas_call(
        _row_kernel(_cf_proj_kernel, geo, x),
        out_shape=jax.ShapeDtypeStruct((geo.n, d), BF16),
        grid=(geo.n // TM,),
        in_specs=_row_specs(geo, x) + [_resident((d, d2)), _resident((1, d2))],
        out_specs=pl.BlockSpec((TM, d), lambda i: (i, 0)),
        compiler_params=_params(("parallel",)),
        name="conformer_in_proj",
    )(*_stream_args(x), mod, gn, w_pw1, b_pw1.reshape(1, d2))


def _sc_proj_kernel(x, mod_ref, gn_ref, w_ref, bg_ref, q_ref):
    h = _prenorm(x, mod_ref, gn_ref, 0)
    d = bg_ref.shape[1]
    step = 512
    for c in range(0, d, step):
        bg_ref[:, c:c + step] = _dot(h, w_ref[:, c:c + step]).astype(BF16)
        cg = _dot(h, w_ref[:, d + c:d + c + step])
        hv = _dot(h, w_ref[:, 2 * d + c:2 * d + c + step])
        q_ref[:, c:c + step] = (cg * hv).astype(BF16)


def shortconv_in_proj(geo, x, mod, gn, w_in):
    d, d3 = w_in.shape
    out = jax.ShapeDtypeStruct((geo.n, d), BF16)
    return pl.pallas_call(
        _row_kernel(_sc_proj_kernel, geo, x),
        out_shape=(out, out),
        grid=(geo.n // TM,),
        in_specs=_row_specs(geo, x) + [_resident((d, d3))],
        out_specs=(pl.BlockSpec((TM, d), lambda i: (i, 0)), pl.BlockSpec((TM, d), lambda i: (i, 0))),
        compiler_params=_params(("parallel",)),
        name="shortconv_in_proj",
    )(*_stream_args(x), mod, gn, w_in)


TAIL_CHAINS = 2


def _layer_tail_kernel(a, x, mod_ref, gn_ref, wo_ref, bo_ref, lg_ref, lb_ref, wg_ref, wu_ref, wd_ref,
                       *rest, nct, layer_norm):
    acts = rest[-TAIL_CHAINS:]
    outs = rest[:-TAIL_CHAINS]
    rows = a.shape[0] // TAIL_CHAINS
    groups = [slice(i * rows, (i + 1) * rows) for i in range(TAIL_CHAINS)]
    if layer_norm:
        u = a.astype(F32)
        mu = jnp.mean(u, axis=-1, keepdims=True)
        uc = u - mu
        var = jnp.mean(uc * uc, axis=-1, keepdims=True)
        a = _silu((uc * lax.rsqrt(var + NORM_EPS)) * lg_ref[...] + lb_ref[...]).astype(BF16)
    ys = [_dot(a[g, :], wo_ref[...]) + bo_ref[...] for g in groups]
    xs = [x[g, :] + mod_ref[2:3, :] * (_rms(y) * gn_ref[1:2, :]) for g, y in zip(groups, ys)]
    hs = [_prenorm(xg, mod_ref, gn_ref, 1) for xg in xs]
    step = 256
    for c in range(0, acts[0].shape[1], step):
        for h, act_ref in zip(hs, acts):
            gate = _dot(h, wg_ref[:, c:c + step])
            up = _dot(h, wu_ref[:, c:c + step])
            act_ref[:, c:c + step] = (_silu(gate) * up).astype(BF16)
    ys = [_dot(act_ref[...], wd_ref[...]) for act_ref in acts]
    res = [xg + mod_ref[5:6, :] * (_rms(y) * gn_ref[3:4, :]) for xg, y in zip(xs, ys)]

    def write(o_ref):
        for g, val in zip(groups, res):
            o_ref[g, :] = val

    if len(outs) == 1:
        write(outs[0])
    else:
        i = pl.program_id(0)
        pl.when(i < nct)(lambda: write(outs[0]))
        pl.when(i >= nct)(lambda: write(outs[1]))


def layer_tail(geo, a, x, mod, gn, w_out, b_out, ln, wg, wu, wd, layer, split=False):
    _, d, dff = wg.shape
    layer_w = lambda r, c: pl.BlockSpec((None, r, c), lambda i: (layer, 0, 0), pipeline_mode=pl.Buffered(1))
    na, nx, nct = len(_stream_args(a)), len(_stream_args(x)), geo.n_ctx // TM
    body = functools.partial(_layer_tail_kernel, nct=nct, layer_norm=ln is not None)

    def kern(*refs):
        body(_stream_tile(refs[:na], nct), _stream_tile(refs[na:na + nx], nct), *refs[na + nx:])

    if ln is None:
        ln = (jnp.ones((d,), F32), jnp.zeros((d,), F32))
    if split:
        out_shape = (jax.ShapeDtypeStruct((geo.n_ctx, d), F32), jax.ShapeDtypeStruct((geo.n_lat, d), F32))
        out_specs = tuple(_stream_specs(geo, (None, None), d))
    else:
        out_shape = jax.ShapeDtypeStruct((geo.n, d), F32)
        out_specs = pl.BlockSpec((TM, d), lambda i: (i, 0))
    return pl.pallas_call(
        kern,
        out_shape=out_shape,
        grid=(geo.n // TM,),
        in_specs=_stream_specs(geo, a, d) + _row_specs(geo, x) + [
            _resident((d, d)), _resident((1, d)), _resident((1, d)), _resident((1, d)),
            layer_w(d, dff), layer_w(d, dff), layer_w(dff, d)],
        out_specs=out_specs,
        scratch_shapes=[pltpu.VMEM((TM // TAIL_CHAINS, dff), BF16)] * TAIL_CHAINS,
        compiler_params=_params(("arbitrary",) if split else ("parallel",)),
        name="layer_tail",
    )(*_stream_args(a), *_stream_args(x), mod, gn, w_out, b_out.reshape(1, d), ln[0].reshape(1, d),
      ln[1].reshape(1, d), wg, wu, wd)


DFT_ROWS = 256


def _dft_table_kernel(cb_ref, sb_ref, o_ref, *, seq_len):
    k0 = pl.program_id(0) * DFT_ROWS
    s = lax.broadcasted_iota(jnp.int32, (1, seq_len), 1)
    ang = ((k0 * s) & (2 * seq_len - 1)).astype(F32) * (math.pi / seq_len)
    ca, sa = jnp.cos(ang), jnp.sin(ang)
    cb, sb = cb_ref[...], sb_ref[...]
    o_ref[:, 0:seq_len] = (ca * cb - sa * sb).astype(BF16)
    o_ref[:, seq_len:2 * seq_len] = (sa * cb + ca * sb).astype(BF16)


def dft_tables(seq_len):
    assert seq_len & (seq_len - 1) == 0 and seq_len % DFT_ROWS == 0
    k = jnp.arange(seq_len, dtype=jnp.int32)
    idx = (k[:DFT_ROWS, None] * k[None, :]) % (2 * seq_len)
    ang = idx.astype(F32) * (math.pi / seq_len)
    sign = (1 - 2 * (k % 2)).astype(F32)[:, None]
    cvec = jnp.where(k == 0, 0.5 / seq_len, 1.0 / seq_len).astype(F32)[:, None]
    base = pl.BlockSpec((DFT_ROWS, seq_len), lambda i: (0, 0))
    cs = pl.pallas_call(
        functools.partial(_dft_table_kernel, seq_len=seq_len),
        out_shape=jax.ShapeDtypeStruct((seq_len, 2 * seq_len), BF16),
        grid=(seq_len // DFT_ROWS,),
        in_specs=[base, base],
        out_specs=pl.BlockSpec((DFT_ROWS, 2 * seq_len), lambda i: (i, 0)),
        compiler_params=_params(("parallel",)),
        name=f"dft_table_{seq_len}",
    )(jnp.cos(ang), jnp.sin(ang))
    return cs, sign, cvec


def _filter_kernel(zp_ref, zm_ref, tp_ref, tm_ref, keep_ref, w1_ref, b1_ref, fr_ref, w2_ref, b2_ref,
                   w3p_ref, w3m_ref, dl_ref, sign_ref, cvec_ref, cs_ref, hr_ref, hi_ref, hn_ref,
                   hp_ref, hm_ref):
    blk = zp_ref.shape[0]
    absdl = jnp.abs(dl_ref[...])

    @pl.when(pl.program_id(1) == 0)
    def _():
        for z_ref, h_ref in ((zp_ref, hp_ref), (zm_ref, hm_ref)):
            h = jnp.sin(fr_ref[0:1, :] * (_dot_hi(z_ref[...], w1_ref[...]) + b1_ref[...]))
            h_ref[...] = jnp.sin(fr_ref[1:2, :] * (_dot_hi(h, w2_ref[...]) + b2_ref[...]))

    def taps(h_ref, t_ref, w3_ref):
        return _dot(h_ref[...].astype(BF16), w3_ref[...].astype(BF16)) * jnp.exp(-t_ref[...] * absdl)

    gp = taps(hp_ref, tp_ref, w3p_ref)
    gm = taps(hm_ref, tm_ref, w3m_ref) * keep_ref[...]
    gs = gp + gm
    gd = gp - gm
    cv = cvec_ref[...]
    hr_ref[...] = cv * _dot(cs_ref[:, 0:blk], gs.astype(BF16))
    hi_ref[...] = cv * _dot(cs_ref[:, blk:2 * blk], gd.astype(BF16))
    hn_ref[...] = jnp.sum(gs * sign_ref[...], axis=0, keepdims=True) * (0.5 / blk)


def hyena_block_spectra(seq_len, blk, d, tables, w1, b1, freq, w2, b2, w3):
    cs, sign, cvec = tables
    emb, fw = w1.shape
    bands = (emb - 1) // 2
    nblk = seq_len // blk
    ndl = 2 * nblk - 1
    dd = jnp.arange(ndl, dtype=jnp.int32)[:, None] - (nblk - 1)
    m = jnp.arange(blk, dtype=jnp.int32)[None, :]
    pos_p = jnp.where(dd >= 0, dd * blk + m, -dd * blk - m)
    pos_m = jnp.where(dd >= 1, dd * blk - m, -dd * blk + m)
    keep = jnp.where((m == 0) & (dd != 0), 0.0, 1.0).astype(F32)

    def features(pos):
        pos = pos.reshape(-1, 1).astype(F32)
        t = pos / (seq_len - 1)
        w = 2.0 * math.pi * pos / seq_len
        f = jnp.linspace(1e-4, bands - 1, bands, dtype=F32)[None, :]
        z = jnp.concatenate([t, jnp.cos(f * w), -jnp.sin(f * w)], axis=-1)
        return jnp.pad(z, ((0, 0), (0, LANE - emb))), t

    zp, tp = features(pos_p)
    zm, tm = features(pos_m)
    w1p = jnp.pad(w1, ((0, LANE - emb), (0, 0)))
    deltas = jnp.linspace(math.log(HY_DECAY_TARGET) / HY_SLOW_PCT,
                          math.log(HY_DECAY_TARGET) / HY_FAST_PCT, d, dtype=F32)[None, :]
    fdc = 2 * DC
    nd = d // fdc
    full = lambda shape: pl.BlockSpec(shape, lambda i, j: (0,) * len(shape))
    rows = lambda width: pl.BlockSpec((blk, width), lambda i, j: (i, 0))
    w3_p = pl.BlockSpec((fw, fdc), lambda i, j: (0, jnp.where(i >= nblk - 1, 0, 2 * nd) + j))
    w3_m = pl.BlockSpec((fw, fdc), lambda i, j: (0, jnp.where(i >= nblk, 0, 2 * nd) + j))
    spec_out = pl.BlockSpec((blk, fdc), lambda i, j: (i, j))
    return pl.pallas_call(
        _filter_kernel,
        out_shape=(jax.ShapeDtypeStruct((ndl * blk, 2 * d), F32),
                   jax.ShapeDtypeStruct((ndl * blk, 2 * d), F32),
                   jax.ShapeDtypeStruct((ndl, 1, 2 * d), F32)),
        grid=(ndl, 2 * nd),
        in_specs=[rows(LANE), rows(LANE), rows(1), rows(1), rows(1),
                  full((LANE, fw)), full((1, fw)), full((2, fw)), full((fw, fw)), full((1, fw)),
                  w3_p, w3_m,
                  pl.BlockSpec((1, fdc), lambda i, j: (0, j % nd)),
                  full((blk, 1)), full((blk, 1)), full((blk, 2 * blk))],
        out_specs=(spec_out, spec_out, pl.BlockSpec((None, 1, fdc), lambda i, j: (i, 0, j))),
        scratch_shapes=[pltpu.VMEM((blk, fw), F32), pltpu.VMEM((blk, fw), F32)],
        compiler_params=_params(("parallel", "arbitrary")),
        name=f"hyena_filter_{seq_len}",
    )(zp, zm, tp, tm, keep.reshape(-1, 1), w1p, b1.reshape(1, fw), freq, w2, b2.reshape(1, fw), w3, w3,
      deltas, sign, cvec, cs)


HY_GROUP = 2
HY_BLOCK = 512
HY_STRIP = 16
HY_HALO = 16


def _dwconv3_rows(p_ref, base, r, nrows, seq_len, w_ref, b_ref):
    lo, hi = max(r - HY_HALO, 0), min(r + nrows + HY_HALO, seq_len)
    win = p_ref[base + lo:base + hi, :].astype(F32)
    n = hi - lo
    prev = pltpu.roll(win, 1, 0)
    nxt = pltpu.roll(win, n - 1, 0)
    rows = lax.broadcasted_iota(jnp.int32, (n, 1), 0)
    if lo == 0:
        prev = jnp.where(rows == 0, 0.0, prev)
    if hi == seq_len:
        nxt = jnp.where(rows == n - 1, 0.0, nxt)
    y = w_ref[0:1, :] * prev + w_ref[1:2, :] * win + w_ref[2:3, :] * nxt + b_ref[...]
    return y[r - lo:r - lo + nrows, :]


def _hyena_mix_kernel(pv_ref, p1_ref, p2_ref, wv_ref, w1_ref, w2_ref, bv_ref, b1_ref, b2_ref,
                      cs_ref, hr1_ref, hi1_ref, hn1_ref, hr2_ref, hi2_ref, hn2_ref, skip_ref,
                      o_ref, z_ref, zb_ref, fa_ref, fb_ref, y_ref, *, seq_len, blk):
    dc = pv_ref.shape[1]
    groups = pv_ref.shape[0] // seq_len
    nblk = seq_len // blk
    tile = lambda a: jnp.concatenate([a] * groups, axis=1)
    rows = lax.broadcasted_iota(jnp.int32, (blk, 1), 0)
    sign = (1 - 2 * (rows % 2)).astype(F32)

    def stage(r, val):
        z_ref[r:r + blk, :] = val
        zb_ref[r:r + blk, :] = val.astype(BF16)
        return jnp.sum(val * sign, axis=0, keepdims=True)

    def block_products(i_out, hr_ref, hi_ref):
        def strip(r0):
            yr, yi = [None] * groups, [None] * groups
            for j in range(nblk):
                h0 = (i_out - j + nblk - 1) * blk
                hr = hr_ref[h0 + r0:h0 + r0 + HY_STRIP, :]
                hi = hi_ref[h0 + r0:h0 + r0 + HY_STRIP, :]
                for g in range(groups):
                    a = fa_ref[j * blk + r0:j * blk + r0 + HY_STRIP, g * dc:(g + 1) * dc]
                    b = fb_ref[j * blk + r0:j * blk + r0 + HY_STRIP, g * dc:(g + 1) * dc]
                    tr, ti = a * hr - b * hi, a * hi + b * hr
                    yr[g], yi[g] = (tr, ti) if yr[g] is None else (yr[g] + tr, yi[g] + ti)
            for g in range(groups):
                y_ref[r0:r0 + HY_STRIP, g * dc:(g + 1) * dc] = yr[g].astype(BF16)
                y_ref[blk + r0:blk + r0 + HY_STRIP, g * dc:(g + 1) * dc] = yi[g].astype(BF16)

        for r0 in range(0, blk, HY_STRIP):
            strip(r0)

    def transform(hr_ref, hi_ref, hn_ref, skip, mult_ref, mw_ref, mb_ref, nyqs, emit):
        for j in range(nblk):
            r = j * blk
            fa_ref[r:r + blk, :] = _dot(cs_ref[:, 0:blk], zb_ref[r:r + blk, :])
            fb_ref[r:r + blk, :] = _dot(cs_ref[:, blk:2 * blk], zb_ref[r:r + blk, :])
        skip_t = tile(skip)
        out = []
        for i in range(nblk):
            r = i * blk
            block_products(i, hr_ref, hi_ref)
            nyq = None
            for j in range(nblk):
                term = nyqs[j] * tile(hn_ref[i - j + nblk - 1])
                nyq = term if nyq is None else nyq + term
            y = _dot(cs_ref[...], y_ref[...]) + sign * nyq + z_ref[r:r + blk, :] * skip_t
            mult = jnp.concatenate([_dwconv3_rows(mult_ref, g * seq_len, r, blk, seq_len, mw_ref, mb_ref)
                                    for g in range(groups)], axis=1)
            out.append(emit(r, mult * y))
        return out

    nyqs = []
    for j in range(nblk):
        v = jnp.concatenate([_dwconv3_rows(pv_ref, g * seq_len, j * blk, blk, seq_len, wv_ref, bv_ref)
                             for g in range(groups)], axis=1)
        nyqs.append(stage(j * blk, v))

    nyqs = transform(hr1_ref, hi1_ref, hn1_ref, skip_ref[0:1, :], p1_ref, w1_ref, b1_ref, nyqs, stage)

    def write_out(r, val):
        for g in range(groups):
            o_ref[g * seq_len + r:g * seq_len + r + blk, :] = val[:, g * dc:(g + 1) * dc].astype(BF16)

    transform(hr2_ref, hi2_ref, hn2_ref, skip_ref[1:2, :], p2_ref, w2_ref, b2_ref, nyqs, write_out)


def hyena_mix(geo, p, conv_w, conv_b, skip, tables, spectrum, seq_len, row0, nb):
    d = geo.d
    cs = tables[0]
    hr, hi, hn = spectrum
    blk = cs.shape[0]
    ndl = 2 * (seq_len // blk) - 1
    nd = d // DC
    group = max(HY_GROUP, 2048 // seq_len)
    rows = group * seq_len
    width = group * DC
    assert nb % group == 0 and row0 % rows == 0 and seq_len % blk == 0 and blk % HY_STRIP == 0
    blk0 = row0 // rows
    pspec = lambda part: pl.BlockSpec((rows, DC), lambda c, b: (blk0 + b, part * nd + c))
    wspec = lambda part: pl.BlockSpec((3, DC), lambda c, b: (0, part * nd + c))
    bspec = lambda part: pl.BlockSpec((1, DC), lambda c, b: (0, part * nd + c))
    hspec = lambda order: pl.BlockSpec((ndl * blk, DC), lambda c, b: (0, order * nd + c),
                                       pipeline_mode=pl.Buffered(1))
    nspec = lambda order: pl.BlockSpec((ndl, 1, DC), lambda c, b: (0, 0, order * nd + c))
    return pl.pallas_call(
        functools.partial(_hyena_mix_kernel, seq_len=seq_len, blk=blk),
        out_shape=jax.ShapeDtypeStruct((nb * seq_len, d), BF16),
        grid=(nd, nb // group),
        in_specs=[pspec(0), pspec(1), pspec(2), wspec(0), wspec(1), wspec(2), bspec(0), bspec(1), bspec(2),
                  _resident((blk, 2 * blk)),
                  hspec(0), hspec(0), nspec(0), hspec(1), hspec(1), nspec(1),
                  pl.BlockSpec((2, DC), lambda c, b: (0, c))],
        out_specs=pl.BlockSpec((rows, DC), lambda c, b: (b, c)),
        scratch_shapes=[pltpu.VMEM((seq_len, width), F32), pltpu.VMEM((seq_len, width), BF16),
                        pltpu.VMEM((seq_len, width), F32), pltpu.VMEM((seq_len, width), F32),
                        pltpu.VMEM((2 * blk, width), BF16)],
        compiler_params=_params(("arbitrary", "arbitrary")),
        name=f"hyena_mix_{seq_len}",
    )(p, p, p, conv_w, conv_w, conv_w, conv_b, conv_b, conv_b, cs,
      hr, hi, hn, hr, hi, hn, skip)


CONV_ROWS = 64
CONV_PAD = 16


def _stage_padded(src, pad_ref):
    n = src.shape[0]
    zeros = jnp.zeros((CONV_PAD, pad_ref.shape[1]), F32)
    pad_ref[0:CONV_PAD, :] = zeros
    pad_ref[CONV_PAD + n:2 * CONV_PAD + n, :] = zeros
    pad_ref[CONV_PAD:CONV_PAD + n, :] = src


def _dwconv_chunks(pad_ref, w_ref, n, ktaps, emit):
    half = ktaps // 2
    lo = (CONV_PAD - half) // SUBLANES * SUBLANES
    hi = -(-(CONV_PAD + half + CONV_ROWS) // SUBLANES) * SUBLANES
    rows = hi - lo

    def chunk(ci, carry):
        r0 = pl.multiple_of(ci * CONV_ROWS, CONV_ROWS)
        window = pad_ref[pl.ds(pl.multiple_of(r0 + lo, SUBLANES), rows), :]
        acc = None
        for k in range(ktaps):
            off = CONV_PAD - half + k - lo
            win = window if off == 0 else pltpu.roll(window, rows - off, 0)
            term = w_ref[k:k + 1, :] * win[0:CONV_ROWS, :]
            acc = term if acc is None else acc + term
        emit(r0, acc)
        return carry

    lax.fori_loop(0, n // CONV_ROWS, chunk, 0)


def _conv_group(seq_len, nb):
    group = max(1, min(nb, 1024 // seq_len))
    assert nb % group == 0
    return group


def _cf_conv_kernel(u_ref, w_ref, b_ref, o_ref, pad_ref, *, seq_len):
    bias = b_ref[...]
    for base in range(0, u_ref.shape[0], seq_len):
        _stage_padded(u_ref[base:base + seq_len, :].astype(F32), pad_ref)

        def emit(r0, acc, base=base):
            o_ref[pl.ds(base + r0, CONV_ROWS), :] = (acc + bias).astype(BF16)

        _dwconv_chunks(pad_ref, w_ref, seq_len, w_ref.shape[0], emit)


def conformer_dwconv(geo, u, w, b, seq_len, row0, nb):
    d = geo.d
    dc = 256
    ktaps = w.shape[0]
    rows = _conv_group(seq_len, nb) * seq_len
    blk0 = row0 // rows
    return pl.pallas_call(
        functools.partial(_cf_conv_kernel, seq_len=seq_len),
        out_shape=jax.ShapeDtypeStruct((nb * seq_len, d), BF16),
        grid=(nb * seq_len // rows, d // dc),
        in_specs=[pl.BlockSpec((rows, dc), lambda b_, c: (blk0 + b_, c)),
                  pl.BlockSpec((ktaps, dc), lambda b_, c: (0, c)),
                  pl.BlockSpec((1, dc), lambda b_, c: (0, c))],
        out_specs=pl.BlockSpec((rows, dc), lambda b_, c: (b_, c)),
        scratch_shapes=[pltpu.VMEM((seq_len + 2 * CONV_PAD, dc), F32)],
        compiler_params=_params(("parallel", "parallel")),
        name=f"conformer_dwconv_{seq_len}",
    )(u, w, b.reshape(1, d))


def _sc_conv_kernel(q_ref, bg_ref, w_ref, o_ref, pad_ref, *, seq_len):
    for base in range(0, q_ref.shape[0], seq_len):
        _stage_padded(q_ref[base:base + seq_len, :].astype(F32), pad_ref)

        def emit(r0, acc, base=base):
            gate = bg_ref[pl.ds(base + r0, CONV_ROWS), :].astype(F32)
            o_ref[pl.ds(base + r0, CONV_ROWS), :] = (gate * acc).astype(BF16)

        _dwconv_chunks(pad_ref, w_ref, seq_len, w_ref.shape[0], emit)


def shortconv_gate(geo, q, bg, w, seq_len, row0, nb):
    d = geo.d
    dc = 256
    ktaps = w.shape[0]
    rows = _conv_group(seq_len, nb) * seq_len
    blk0 = row0 // rows
    seq = pl.BlockSpec((rows, dc), lambda b_, c: (blk0 + b_, c))
    return pl.pallas_call(
        functools.partial(_sc_conv_kernel, seq_len=seq_len),
        out_shape=jax.ShapeDtypeStruct((nb * seq_len, d), BF16),
        grid=(nb * seq_len // rows, d // dc),
        in_specs=[seq, seq, pl.BlockSpec((ktaps, dc), lambda b_, c: (0, c))],
        out_specs=pl.BlockSpec((rows, dc), lambda b_, c: (b_, c)),
        scratch_shapes=[pltpu.VMEM((seq_len + 2 * CONV_PAD, dc), F32)],
        compiler_params=_params(("parallel", "parallel")),
        name=f"shortconv_gate_{seq_len}",
    )(q, bg, w)


def _mla_proj_kernel(x, mod_ref, gn_ref, wdq_ref, gq_ref, wun_ref, wup_ref, wus_ref,
                     wkc_ref, wkp_ref, wks_ref, gkv_ref, cosq_ref, sinq_ref,
                     q_ref, ckv_ref, kpe_ref, *, heads, nope, rope, qscale):
    h = _prenorm(x, mod_ref, gn_ref, 0)
    cq = (_rms(_dot(h, wdq_ref[...])) * gq_ref[...]).astype(BF16)
    qn = _dot(cq, wun_ref[...]) * qscale
    cos2, sin2 = cosq_ref[...], sinq_ref[...]
    cosq, sinq = cos2 * qscale, sin2 * qscale
    for pair in range(heads // 2):
        lo = pair * 2 * rope
        qp = (_dot(cq, wup_ref[:, lo:lo + 2 * rope]) * cosq
              + _dot(cq, wus_ref[:, lo:lo + 2 * rope]) * sinq)
        for j in range(2):
            hd = 2 * pair + j
            q_ref[hd, :, 0:nope] = qn[:, hd * nope:(hd + 1) * nope].astype(BF16)
            q_ref[hd, :, nope:nope + rope] = qp[:, j * rope:(j + 1) * rope].astype(BF16)
    ckv_ref[...] = _rms(_dot(h, wkc_ref[...])) * gkv_ref[...]
    kpe_ref[...] = (_dot(h, wkp_ref[...]) * cos2[:, 0:rope] + _dot(h, wks_ref[...]) * sin2[:, 0:rope])


def mla_project(geo, x, mod, gn, wdq, gq, wun, wup, wus, wkc, wkp, wks, gkv, cos2, sin2, heads, nope, rope,
                qscale):
    d = geo.d
    qr, kvr = wdq.shape[1], wkc.shape[1]
    row = lambda w: pl.BlockSpec((TM, w), lambda i: (i, 0))
    rot = pl.BlockSpec((TM, 2 * rope), lambda i: (_rope_block_of_tile(geo, i), 0))
    body = functools.partial(_mla_proj_kernel, heads=heads, nope=nope, rope=rope, qscale=qscale)
    return pl.pallas_call(
        _row_kernel(body, geo, x),
        out_shape=(jax.ShapeDtypeStruct((heads, geo.n, nope + rope), BF16),
                   jax.ShapeDtypeStruct((geo.n, kvr), F32),
                   jax.ShapeDtypeStruct((geo.n, rope), F32)),
        grid=(geo.n // TM,),
        in_specs=_row_specs(geo, x) + [_resident((d, qr)), _resident((1, qr)),
                                    _resident((qr, heads * nope)), _resident((qr, heads * rope)),
                                    _resident((qr, heads * rope)),
                                    _resident((d, kvr)), _resident((d, rope)), _resident((d, rope)),
                                    _resident((1, kvr)), rot, rot],
        out_specs=(pl.BlockSpec((heads, TM, nope + rope), lambda i: (0, i, 0)), row(kvr), row(rope)),
        compiler_params=_params(("parallel",)),
        name="mla_project",
    )(*_stream_args(x), mod, gn, wdq, gq.reshape(1, qr), wun, wup, wus, wkc, wkp, wks, gkv.reshape(1, kvr),
      cos2, sin2)


KV_ROWS = 256


def _mla_kv_kernel(*refs, heads, nope, rope, past_tiles):
    ckv_ref, kpe_ref, wn_ref, wv_ref, k_ref, v_ref = refs[-6:]
    ckv, kpe = ckv_ref[...], kpe_ref[...]
    if past_tiles:
        cached = pl.program_id(1) < past_tiles
        ckv = jnp.where(cached, refs[0][...], ckv)
        kpe = jnp.where(cached, refs[1][...], kpe)
    c = ckv.astype(BF16)
    kn = _dot(c, wn_ref[...])
    v_ref[...] = _dot(c, wv_ref[...]).astype(BF16)
    kp = kpe.astype(BF16)
    for hd in range(heads):
        k_ref[hd, :, 0:nope] = kn[:, hd * nope:(hd + 1) * nope].astype(BF16)
        k_ref[hd, :, nope:nope + rope] = kp


def mla_expand_kv(ckv, kpe, cache, layer, seq_len, row0, nb, wn, wv, heads, nope, rope, vdim):
    r = ckv.shape[1]
    past = 0 if cache is None else cache[0].shape[2]
    assert past % KV_ROWS == 0 and seq_len % KV_ROWS == 0 and row0 % KV_ROWS == 0
    pt, st, blk0 = past // KV_ROWS, seq_len // KV_ROWS, row0 // KV_ROWS
    lk = past + seq_len
    new_map = lambda b, t: (blk0 + b * st + jnp.maximum(t - pt, 0), 0)
    specs = [pl.BlockSpec((KV_ROWS, r), new_map), pl.BlockSpec((KV_ROWS, rope), new_map),
             _resident((r, heads * nope)), _resident((r, heads * vdim))]
    args = [ckv, kpe, wn, wv]
    if cache is not None:
        old_map = lambda b, t: (b, layer, jnp.minimum(t, pt - 1), 0)
        specs = [pl.BlockSpec((None, None, KV_ROWS, r), old_map),
                 pl.BlockSpec((None, None, KV_ROWS, rope), old_map)] + specs
        args = list(cache) + args
    return pl.pallas_call(
        functools.partial(_mla_kv_kernel, heads=heads, nope=nope, rope=rope, past_tiles=pt),
        out_shape=(jax.ShapeDtypeStruct((nb, heads, lk, nope + rope), BF16),
                   jax.ShapeDtypeStruct((nb, lk, heads * vdim), BF16)),
        grid=(nb, lk // KV_ROWS),
        in_specs=specs,
        out_specs=(pl.BlockSpec((None, heads, KV_ROWS, nope + rope), lambda b, t: (b, 0, t, 0)),
                   pl.BlockSpec((None, KV_ROWS, heads * vdim), lambda b, t: (b, t, 0))),
        compiler_params=_params(("parallel", "arbitrary")),
        name=f"mla_expand_kv_{lk}",
    )(*args)


ATTN_ROWS = 256


def _attn_kernel(q_ref, k_ref, v_ref, o_ref, *, vdim):
    hg, tq, _ = q_ref.shape
    chains = [(h, r) for h in range(hg) for r in range(0, tq, ATTN_ROWS)]

    def scores(h, r):
        return lax.dot_general(q_ref[h, r:r + ATTN_ROWS, :], k_ref[h], (((1,), (1,)), ((), ())),
                               preferred_element_type=F32)

    s_next = scores(*chains[0])
    for i, (h, r) in enumerate(chains):
        s = s_next
        if i + 1 < len(chains):
            s_next = scores(*chains[i + 1])
        m = jnp.max(s, axis=-1, keepdims=True)
        p = jnp.exp2(s - m)
        l = jnp.sum(p, axis=-1, keepdims=True)
        o = _dot(p.astype(BF16), v_ref[:, h * vdim:(h + 1) * vdim])
        o_ref[r:r + ATTN_ROWS, h * vdim:(h + 1) * vdim] = (o * (1.0 / l)).astype(BF16)


def mla_attention(q, k, v, seq_len, row0, nb, vdim):
    heads, _, dq = q.shape
    lk = k.shape[2]
    hg = max(1, min(heads, 2048 // seq_len))
    blk0 = row0 // seq_len
    return pl.pallas_call(
        functools.partial(_attn_kernel, vdim=vdim),
        out_shape=jax.ShapeDtypeStruct((nb * seq_len, heads * vdim), BF16),
        grid=(nb, heads // hg),
        in_specs=[pl.BlockSpec((hg, seq_len, dq), lambda b, h: (h, blk0 + b, 0)),
                  pl.BlockSpec((None, hg, lk, dq), lambda b, h: (b, h, 0, 0)),
                  pl.BlockSpec((None, lk, hg * vdim), lambda b, h: (b, 0, h))],
        out_specs=pl.BlockSpec((seq_len, hg * vdim), lambda b, h: (b, h)),
        compiler_params=_params(("parallel", "parallel")),
        name=f"mla_attention_{seq_len}",
    )(q, k, v)


def rope_tables(geo, rope):
    nf = rope // 4
    rows = geo.l_lat // GRID_W
    row = jnp.repeat(jnp.arange(rows, dtype=F32), GRID_W)
    col = jnp.tile(jnp.arange(GRID_W, dtype=F32), rows)
    inv = jnp.exp(-math.log(ROPE_THETA) * jnp.arange(nf, dtype=F32) * (4.0 / rope))
    ar, ac = row[:, None] * inv, col[:, None] * inv
    cos = jnp.concatenate([jnp.cos(ar), jnp.cos(ar), jnp.cos(ac), jnp.cos(ac)], axis=-1)
    sin = jnp.concatenate([-jnp.sin(ar), jnp.sin(ar), -jnp.sin(ac), jnp.sin(ac)], axis=-1)
    cos = jnp.concatenate([jnp.ones((TM, rope), F32), cos], axis=0)
    sin = jnp.concatenate([jnp.zeros((TM, rope), F32), sin], axis=0)
    return jnp.tile(cos, (1, 2)), jnp.tile(sin, (1, 2))


def _rope_block_of_tile(geo, i):
    nct, per_seq = geo.n_ctx // TM, geo.l_lat // TM
    return jnp.where(i < nct, 0, 1 + (i - nct) % per_seq)


def _swap_rope_halves(w, rope):
    nf = rope // 4
    shape = w.shape
    w = w.reshape(shape[:-1] + (shape[-1] // (2 * nf), 2, nf))
    return w[..., ::-1, :].reshape(shape)


def kernel(x_prompt, x_sample, c, cache_ckv, cache_kpe, c_ctx, w_mod, b_mod, norm_g, ffn_w_gate, ffn_w_up, ffn_w_down, hy_w_in, hy_b_in, hy_conv_w, hy_conv_b, hy_f_w1, hy_f_b1, hy_f_freq, hy_f_w2, hy_f_b2, hy_f_w3, hy_skip, hy_w_out, hy_b_out, cf_w_pw1, cf_b_pw1, cf_dw_w, cf_dw_b, cf_ln_g, cf_ln_b, cf_w_pw2, cf_b_pw2, sc_w_in, sc_conv_w, sc_w_out, mla_w_dq, mla_g_q, mla_w_uq, mla_w_dkv, mla_g_kv, mla_w_ukv, mla_w_o):
    nb_ctx, l_ctx, d = x_prompt.shape
    nb_lat, l_lat, _ = x_sample.shape
    geo = Geometry(nb_ctx, l_ctx, nb_lat, l_lat, d)
    depth = w_mod.shape[0]
    n_mixers = 4

    x = (x_prompt.reshape(geo.n_ctx, d), x_sample.reshape(geo.n_lat, d))
    cond = jnp.concatenate([c_ctx[None, :], c], axis=0)
    n_cond = cond.shape[0]
    cond = jnp.pad(cond, ((0, -n_cond % 8), (0, 0)))
    mod_all = modulation_all(cond, w_mod, b_mod)[:, :n_cond].reshape(depth, n_cond, 6, d)

    passes = ((l_ctx, 0, nb_ctx), (l_lat, geo.n_ctx, nb_lat))
    zero_bias = jnp.zeros((d,), F32)
    ffn_wg, ffn_wu, ffn_wd = ffn_w_gate.astype(BF16), ffn_w_up.astype(BF16), ffn_w_down.astype(BF16)
    new_ckv, new_kpe = [], []

    for i in range(depth):
        m, j = i % n_mixers, i // n_mixers
        mod, gn = mod_all[i], norm_g[i]
        if m == 0:
            p = hyena_in_proj(geo, x, mod, gn, hy_w_in[j].astype(BF16), hy_b_in[j])
            parts = []
            for seq_len, row0, nb in passes:
                blk = min(seq_len, HY_BLOCK)
                tables = dft_tables(blk)
                spectrum = hyena_block_spectra(seq_len, blk, d, tables, hy_f_w1[j], hy_f_b1[j], hy_f_freq[j],
                                               hy_f_w2[j], hy_f_b2[j], hy_f_w3[j])
                parts.append(hyena_mix(geo, p, hy_conv_w[j], hy_conv_b[j].reshape(1, -1), hy_skip[j],
                                       tables, spectrum, seq_len, row0, nb))
            a, w_out, b_out, ln = tuple(parts), hy_w_out[j], hy_b_out[j], None
        elif m == 1:
            u = conformer_in_proj(geo, x, mod, gn, cf_w_pw1[j].astype(BF16), cf_b_pw1[j])
            a = tuple(conformer_dwconv(geo, u, cf_dw_w[j], cf_dw_b[j], seq_len, row0, nb)
                      for seq_len, row0, nb in passes)
            w_out, b_out, ln = cf_w_pw2[j], cf_b_pw2[j], (cf_ln_g[j], cf_ln_b[j])
        elif m == 2:
            bg, q = shortconv_in_proj(geo, x, mod, gn, sc_w_in[j].astype(BF16))
            a = tuple(shortconv_gate(geo, q, bg, sc_conv_w[j], seq_len, row0, nb)
                      for seq_len, row0, nb in passes)
            w_out, b_out, ln = sc_w_out[j], zero_bias, None
        else:
            kvr = mla_g_kv.shape[1]
            rope = mla_w_dkv.shape[2] - kvr
            vdim = 128
            heads = mla_w_o.shape[1] // vdim
            nope = mla_w_uq.shape[2] // heads - rope
            wuq = mla_w_uq[j].reshape(-1, heads, nope + rope)
            wun = wuq[:, :, :nope].reshape(-1, heads * nope).astype(BF16)
            wup = wuq[:, :, nope:].reshape(-1, heads * rope)
            wukv = mla_w_ukv[j].reshape(kvr, heads, nope + vdim)
            wkn = wukv[:, :, :nope].reshape(kvr, heads * nope).astype(BF16)
            wkv = wukv[:, :, nope:].reshape(kvr, heads * vdim).astype(BF16)
            wkc, wkp = mla_w_dkv[j][:, :kvr], mla_w_dkv[j][:, kvr:]
            cos2, sin2 = rope_tables(geo, rope)
            q, ckv, kpe = mla_project(geo, x, mod, gn, mla_w_dq[j].astype(BF16), mla_g_q[j], wun,
                                      wup.astype(BF16), _swap_rope_halves(wup, rope).astype(BF16),
                                      wkc.astype(BF16), wkp.astype(BF16),
                                      _swap_rope_halves(wkp, rope).astype(BF16), mla_g_kv[j],
                                      cos2, sin2, heads, nope, rope,
                                      (nope + rope) ** -0.5 * math.log2(math.e))
            new_ckv.append(ckv[:geo.n_ctx].reshape(nb_ctx, l_ctx, kvr))
            new_kpe.append(kpe[:geo.n_ctx].reshape(nb_ctx, l_ctx, rope))
            parts = []
            for (seq_len, row0, nb), cache in zip(passes, (None, (cache_ckv, cache_kpe))):
                kk, vv = mla_expand_kv(ckv, kpe, cache, j, seq_len, row0, nb, wkn, wkv, heads, nope, rope, vdim)
                parts.append(mla_attention(q, kk, vv, seq_len, row0, nb, vdim))
            a, w_out, b_out, ln = tuple(parts), mla_w_o[j], zero_bias, None
        x = layer_tail(geo, a, x, mod, gn, w_out.astype(BF16), b_out, ln, ffn_wg, ffn_wu, ffn_wd, i,
                       split=(i == depth - 1))

    y_prompt = x[0].reshape(nb_ctx, l_ctx, d)
    y_sample = x[1].reshape(nb_lat, l_lat, d)
    return (y_prompt, y_sample, jnp.stack(new_ckv, axis=1), jnp.stack(new_kpe, axis=1))
```

```python
import functools
import math

import jax
import jax.numpy as jnp
from jax import lax
from jax.experimental import pallas as pl
from jax.experimental.pallas import tpu as pltpu

F32 = jnp.float32
BF16 = jnp.bfloat16

NORM_EPS = 1e-6
GRID_W = 64
ROPE_THETA = 10000.0
HY_DECAY_TARGET = 1e-2
HY_FAST_PCT = 0.3
HY_SLOW_PCT = 1.5

V7X_VMEM_LIMIT_BYTES = 60000 * 1024
LANE = 128
SUBLANES = 8

TM = 512
DC = 256


def _dot(a, b):
    return jnp.dot(a, b, preferred_element_type=F32)


def _dot_hi(a, b):
    return jnp.dot(a, b, preferred_element_type=F32, precision=lax.Precision.HIGHEST)


def _rms(x):
    return x * lax.rsqrt(jnp.mean(x * x, axis=-1, keepdims=True) + NORM_EPS)


def _sigmoid(x):
    return 1.0 / (1.0 + jnp.exp(-x))


def _silu(x):
    return x * _sigmoid(x)


def _params(semantics):
    return pltpu.CompilerParams(dimension_semantics=semantics,
                                vmem_limit_bytes=V7X_VMEM_LIMIT_BYTES)


def _resident(shape):
    zeros = (0,) * len(shape)
    return pl.BlockSpec(shape, lambda *_: zeros, pipeline_mode=pl.Buffered(1))


class Geometry:
    def __init__(self, nb_ctx, l_ctx, nb_lat, l_lat, d):
        self.nb_ctx, self.l_ctx, self.nb_lat, self.l_lat, self.d = nb_ctx, l_ctx, nb_lat, l_lat, d
        self.n_ctx = nb_ctx * l_ctx
        self.n_lat = nb_lat * l_lat
        self.n = self.n_ctx + self.n_lat
        assert self.n_ctx % TM == 0 and l_lat % TM == 0 and self.n_ctx % l_lat == 0

    def cond_of_tile(self, i):
        row = i * TM
        return jnp.where(row < self.n_ctx, 0, 1 + (row - self.n_ctx) // self.l_lat)


def _mod_kernel(c_ref, w_ref, b_ref, o_ref):
    a = _silu(c_ref[...]).astype(BF16)
    o_ref[...] = _dot(a, w_ref[...].astype(BF16)) + b_ref[...]


def modulation_all(cond, w_mod, b_mod):
    depth, d, d6 = w_mod.shape
    r = cond.shape[0]
    tn = 1536
    return pl.pallas_call(
        _mod_kernel,
        out_shape=jax.ShapeDtypeStruct((depth, r, d6), F32),
        grid=(depth, d6 // tn),
        in_specs=[pl.BlockSpec((r, d), lambda l, j: (0, 0)),
                  pl.BlockSpec((None, d, tn), lambda l, j: (l, 0, j)),
                  pl.BlockSpec((None, 1, tn), lambda l, j: (l, 0, j))],
        out_specs=pl.BlockSpec((None, r, tn), lambda l, j: (l, 0, j)),
        compiler_params=_params(("parallel", "parallel")),
        name="modulation",
    )(cond, w_mod, b_mod.reshape(depth, 1, d6))


def _stream_args(op):
    return list(op) if isinstance(op, tuple) else [op]


def _stream_specs(geo, op, width):
    if not isinstance(op, tuple):
        return [pl.BlockSpec((TM, width), lambda i: (i, 0))]
    nct = geo.n_ctx // TM
    return [pl.BlockSpec((TM, width), lambda i: (jnp.minimum(i, nct - 1), 0)),
            pl.BlockSpec((TM, width), lambda i: (jnp.maximum(i - nct, 0), 0))]


def _stream_tile(refs, nct):
    if len(refs) == 1:
        return refs[0][...]
    return jnp.where(pl.program_id(0) < nct, refs[0][...], refs[1][...])


def _prenorm(x, mod_ref, gn_ref, which):
    g = gn_ref[2 * which:2 * which + 1, :]
    sh = mod_ref[3 * which:3 * which + 1, :]
    sc = mod_ref[3 * which + 1:3 * which + 2, :]
    return ((_rms(x) * g) * (1.0 + sc) + sh).astype(BF16)


def _row_specs(geo, x):
    d = geo.d
    return _stream_specs(geo, x, d) + [
        pl.BlockSpec((None, 6, d), lambda i: (geo.cond_of_tile(i), 0, 0)),
        pl.BlockSpec((4, d), lambda i: (0, 0))]


def _row_kernel(body, geo, x):
    nx, nct = len(_stream_args(x)), geo.n_ctx // TM

    def kern(*refs):
        body(_stream_tile(refs[:nx], nct), *refs[nx:])

    return kern


def _hy_proj_kernel(x, mod_ref, gn_ref, w_ref, b_ref, o_ref):
    h = _prenorm(x, mod_ref, gn_ref, 0)
    step = 512
    for c in range(0, o_ref.shape[1], step):
        o_ref[:, c:c + step] = (_dot(h, w_ref[:, c:c + step]) + b_ref[:, c:c + step]).astype(BF16)


def hyena_in_proj(geo, x, mod, gn, w_in, b_in):
    d, d3 = w_in.shape
    return pl.pallas_call(
        _row_kernel(_hy_proj_kernel, geo, x),
        out_shape=jax.ShapeDtypeStruct((geo.n, d3), BF16),
        grid=(geo.n // TM,),
        in_specs=_row_specs(geo, x) + [_resident((d, d3)), _resident((1, d3))],
        out_specs=pl.BlockSpec((TM, d3), lambda i: (i, 0)),
        compiler_params=_params(("parallel",)),
        name="hyena_in_proj",
    )(*_stream_args(x), mod, gn, w_in, b_in.reshape(1, d3))


def _cf_proj_kernel(x, mod_ref, gn_ref, w_ref, b_ref, o_ref):
    h = _prenorm(x, mod_ref, gn_ref, 0)
    d = o_ref.shape[1]
    step = 512
    for c in range(0, d, step):
        a = _dot(h, w_ref[:, c:c + step]) + b_ref[:, c:c + step]
        g = _dot(h, w_ref[:, d + c:d + c + step]) + b_ref[:, d + c:d + c + step]
        o_ref[:, c:c + step] = (a * _sigmoid(g)).astype(BF16)


def conformer_in_proj(geo, x, mod, gn, w_pw1, b_pw1):
    d, d2 = w_pw1.shape
    return pl.pallas_call(
        _row_kernel(_cf_proj_kernel, geo, x),
        out_shape=jax.ShapeDtypeStruct((geo.n, d), BF16),
        grid=(geo.n // TM,),
        in_specs=_row_specs(geo, x) + [_resident((d, d2)), _resident((1, d2))],
        out_specs=pl.BlockSpec((TM, d), lambda i: (i, 0)),
        compiler_params=_params(("parallel",)),
        name="conformer_in_proj",
    )(*_stream_args(x), mod, gn, w_pw1, b_pw1.reshape(1, d2))


def _sc_proj_kernel(x, mod_ref, gn_ref, w_ref, bg_ref, q_ref):
    h = _prenorm(x, mod_ref, gn_ref, 0)
    d = bg_ref.shape[1]
    step = 512
    for c in range(0, d, step):
        bg_ref[:, c:c + step] = _dot(h, w_ref[:, c:c + step]).astype(BF16)
        cg = _dot(h, w_ref[:, d + c:d + c + step])
        hv = _dot(h, w_ref[:, 2 * d + c:2 * d + c + step])
        q_ref[:, c:c + step] = (cg * hv).astype(BF16)


def shortconv_in_proj(geo, x, mod, gn, w_in):
    d, d3 = w_in.shape
    out = jax.ShapeDtypeStruct((geo.n, d), BF16)
    return pl.pallas_call(
        _row_kernel(_sc_proj_kernel, geo, x),
        out_shape=(out, out),
        grid=(geo.n // TM,),
        in_specs=_row_specs(geo, x) + [_resident((d, d3))],
        out_specs=(pl.BlockSpec((TM, d), lambda i: (i, 0)), pl.BlockSpec((TM, d), lambda i: (i, 0))),
        compiler_params=_params(("parallel",)),
        name="shortconv_in_proj",
    )(*_stream_args(x), mod, gn, w_in)


TAIL_CHAINS = 2


def _layer_tail_kernel(a, x, mod_ref, gn_ref, wo_ref, bo_ref, lg_ref, lb_ref, wg_ref, wu_ref, wd_ref,
                       *rest, nct, layer_norm):
    acts = rest[-TAIL_CHAINS:]
    outs = rest[:-TAIL_CHAINS]
    rows = a.shape[0] // TAIL_CHAINS
    groups = [slice(i * rows, (i + 1) * rows) for i in range(TAIL_CHAINS)]
    if layer_norm:
        u = a.astype(F32)
        mu = jnp.mean(u, axis=-1, keepdims=True)
        uc = u - mu
        var = jnp.mean(uc * uc, axis=-1, keepdims=True)
        a = _silu((uc * lax.rsqrt(var + NORM_EPS)) * lg_ref[...] + lb_ref[...]).astype(BF16)
    ys = [_dot(a[g, :], wo_ref[...]) + bo_ref[...] for g in groups]
    xs = [x[g, :] + mod_ref[2:3, :] * (_rms(y) * gn_ref[1:2, :]) for g, y in zip(groups, ys)]
    hs = [_prenorm(xg, mod_ref, gn_ref, 1) for xg in xs]
    step = 256
    for c in range(0, acts[0].shape[1], step):
        for h, act_ref in zip(hs, acts):
            gate = _dot(h, wg_ref[:, c:c + step])
            up = _dot(h, wu_ref[:, c:c + step])
            act_ref[:, c:c + step] = (_silu(gate) * up).astype(BF16)
    ys = [_dot(act_ref[...], wd_ref[...]) for act_ref in acts]
    res = [xg + mod_ref[5:6, :] * (_rms(y) * gn_ref[3:4, :]) for xg, y in zip(xs, ys)]

    def write(o_ref):
        for g, val in zip(groups, res):
            o_ref[g, :] = val

    if len(outs) == 1:
        write(outs[0])
    else:
        i = pl.program_id(0)
        pl.when(i < nct)(lambda: write(outs[0]))
        pl.when(i >= nct)(lambda: write(outs[1]))


def layer_tail(geo, a, x, mod, gn, w_out, b_out, ln, wg, wu, wd, layer, split=False):
    _, d, dff = wg.shape
    layer_w = lambda r, c: pl.BlockSpec((None, r, c), lambda i: (layer, 0, 0), pipeline_mode=pl.Buffered(1))
    na, nx, nct = len(_stream_args(a)), len(_stream_args(x)), geo.n_ctx // TM
    body = functools.partial(_layer_tail_kernel, nct=nct, layer_norm=ln is not None)

    def kern(*refs):
        body(_stream_tile(refs[:na], nct), _stream_tile(refs[na:na + nx], nct), *refs[na + nx:])

    if ln is None:
        ln = (jnp.ones((d,), F32), jnp.zeros((d,), F32))
    if split:
        out_shape = (jax.ShapeDtypeStruct((geo.n_ctx, d), F32), jax.ShapeDtypeStruct((geo.n_lat, d), F32))
        out_specs = tuple(_stream_specs(geo, (None, None), d))
    else:
        out_shape = jax.ShapeDtypeStruct((geo.n, d), F32)
        out_specs = pl.BlockSpec((TM, d), lambda i: (i, 0))
    return pl.pallas_call(
        kern,
        out_shape=out_shape,
        grid=(geo.n // TM,),
        in_specs=_stream_specs(geo, a, d) + _row_specs(geo, x) + [
            _resident((d, d)), _resident((1, d)), _resident((1, d)), _resident((1, d)),
            layer_w(d, dff), layer_w(d, dff), layer_w(dff, d)],
        out_specs=out_specs,
        scratch_shapes=[pltpu.VMEM((TM // TAIL_CHAINS, dff), BF16)] * TAIL_CHAINS,
        compiler_params=_params(("arbitrary",) if split else ("parallel",)),
        name="layer_tail",
    )(*_stream_args(a), *_stream_args(x), mod, gn, w_out, b_out.reshape(1, d), ln[0].reshape(1, d),
      ln[1].reshape(1, d), wg, wu, wd)


DFT_ROWS = 256


def _dft_table_kernel(cb_ref, sb_ref, o_ref, *, seq_len):
    k0 = pl.program_id(0) * DFT_ROWS
    s = lax.broadcasted_iota(jnp.int32, (1, seq_len), 1)
    ang = ((k0 * s) & (2 * seq_len - 1)).astype(F32) * (math.pi / seq_len)
    ca, sa = jnp.cos(ang), jnp.sin(ang)
    cb, sb = cb_ref[...], sb_ref[...]
    o_ref[:, 0:seq_len] = (ca * cb - sa * sb).astype(BF16)
    o_ref[:, seq_len:2 * seq_len] = (sa * cb + ca * sb).astype(BF16)


def dft_tables(seq_len):
    assert seq_len & (seq_len - 1) == 0 and seq_len % DFT_ROWS == 0
    k = jnp.arange(seq_len, dtype=jnp.int32)
    idx = (k[:DFT_ROWS, None] * k[None, :]) % (2 * seq_len)
    ang = idx.astype(F32) * (math.pi / seq_len)
    sign = (1 - 2 * (k % 2)).astype(F32)[:, None]
    cvec = jnp.where(k == 0, 0.5 / seq_len, 1.0 / seq_len).astype(F32)[:, None]
    base = pl.BlockSpec((DFT_ROWS, seq_len), lambda i: (0, 0))
    cs = pl.pallas_call(
        functools.partial(_dft_table_kernel, seq_len=seq_len),
        out_shape=jax.ShapeDtypeStruct((seq_len, 2 * seq_len), BF16),
        grid=(seq_len // DFT_ROWS,),
        in_specs=[base, base],
        out_specs=pl.BlockSpec((DFT_ROWS, 2 * seq_len), lambda i: (i, 0)),
        compiler_params=_params(("parallel",)),
        name=f"dft_table_{seq_len}",
    )(jnp.cos(ang), jnp.sin(ang))
    return cs, sign, cvec


def _filter_kernel(zp_ref, zm_ref, tp_ref, tm_ref, keep_ref, w1_ref, b1_ref, fr_ref, w2_ref, b2_ref,
                   w3p_ref, w3m_ref, dl_ref, sign_ref, cvec_ref, cs_ref, hr_ref, hi_ref, hn_ref,
                   hp_ref, hm_ref):
    blk = zp_ref.shape[0]
    absdl = jnp.abs(dl_ref[...])

    @pl.when(pl.program_id(1) == 0)
    def _():
        fw = hp_ref.shape[1]
        pre = jnp.concatenate([_dot_hi(zp_ref[...], w1_ref[...]), _dot_hi(zm_ref[...], w1_ref[...])], axis=1)
        h = jnp.sin(fr_ref[0:1, :] * (pre + b1_ref[...]))
        h = jnp.sin(fr_ref[1:2, :] * (_dot_hi(h, w2_ref[...]) + b2_ref[...]))
        hp_ref[...] = h[:, 0:fw]
        hm_ref[...] = h[:, fw:2 * fw]

    def taps(h_ref, t_ref, w3_ref):
        return _dot(h_ref[...].astype(BF16), w3_ref[...].astype(BF16)) * jnp.exp(-t_ref[...] * absdl)

    gp = taps(hp_ref, tp_ref, w3p_ref)
    gm = taps(hm_ref, tm_ref, w3m_ref) * keep_ref[...]
    gs = gp + gm
    gd = gp - gm
    cv = cvec_ref[...]
    hr_ref[...] = cv * _dot(cs_ref[:, 0:blk], gs.astype(BF16))
    hi_ref[...] = cv * _dot(cs_ref[:, blk:2 * blk], gd.astype(BF16))
    hn_ref[...] = jnp.sum(gs * sign_ref[...], axis=0, keepdims=True) * (0.5 / blk)


def hyena_block_spectra(seq_len, blk, d, tables, w1, b1, freq, w2, b2, w3):
    cs, sign, cvec = tables
    emb, fw = w1.shape
    bands = (emb - 1) // 2
    nblk = seq_len // blk
    ndl = 2 * nblk - 1
    dd = jnp.arange(ndl, dtype=jnp.int32)[:, None] - (nblk - 1)
    m = jnp.arange(blk, dtype=jnp.int32)[None, :]
    pos_p = jnp.where(dd >= 0, dd * blk + m, -dd * blk - m)
    pos_m = jnp.where(dd >= 1, dd * blk - m, -dd * blk + m)
    keep = jnp.where((m == 0) & (dd != 0), 0.0, 1.0).astype(F32)

    def features(pos):
        pos = pos.reshape(-1, 1).astype(F32)
        t = pos / (seq_len - 1)
        w = 2.0 * math.pi * pos / seq_len
        f = jnp.linspace(1e-4, bands - 1, bands, dtype=F32)[None, :]
        z = jnp.concatenate([t, jnp.cos(f * w), -jnp.sin(f * w)], axis=-1)
        return jnp.pad(z, ((0, 0), (0, LANE - emb))), t

    zp, tp = features(pos_p)
    zm, tm = features(pos_m)
    w1p = jnp.pad(w1, ((0, LANE - emb), (0, 0)))
    zeros = jnp.zeros_like(w2)
    w2_pair = jnp.concatenate([jnp.concatenate([w2, zeros], axis=1),
                               jnp.concatenate([zeros, w2], axis=1)], axis=0)
    deltas = jnp.linspace(math.log(HY_DECAY_TARGET) / HY_SLOW_PCT,
                          math.log(HY_DECAY_TARGET) / HY_FAST_PCT, d, dtype=F32)[None, :]
    fdc = 2 * DC
    nd = d // fdc
    full = lambda shape: pl.BlockSpec(shape, lambda i, j: (0,) * len(shape))
    rows = lambda width: pl.BlockSpec((blk, width), lambda i, j: (i, 0))
    w3_p = pl.BlockSpec((fw, fdc), lambda i, j: (0, jnp.where(i >= nblk - 1, 0, 2 * nd) + j))
    w3_m = pl.BlockSpec((fw, fdc), lambda i, j: (0, jnp.where(i >= nblk, 0, 2 * nd) + j))
    spec_out = pl.BlockSpec((blk, fdc), lambda i, j: (i, j))
    return pl.pallas_call(
        _filter_kernel,
        out_shape=(jax.ShapeDtypeStruct((ndl * blk, 2 * d), F32),
                   jax.ShapeDtypeStruct((ndl * blk, 2 * d), F32),
                   jax.ShapeDtypeStruct((ndl, 1, 2 * d), F32)),
        grid=(ndl, 2 * nd),
        in_specs=[rows(LANE), rows(LANE), rows(1), rows(1), rows(1),
                  full((LANE, fw)), full((1, 2 * fw)), full((2, 2 * fw)), full((2 * fw, 2 * fw)),
                  full((1, 2 * fw)),
                  w3_p, w3_m,
                  pl.BlockSpec((1, fdc), lambda i, j: (0, j % nd)),
                  full((blk, 1)), full((blk, 1)), full((blk, 2 * blk))],
        out_specs=(spec_out, spec_out, pl.BlockSpec((None, 1, fdc), lambda i, j: (i, 0, j))),
        scratch_shapes=[pltpu.VMEM((blk, fw), F32), pltpu.VMEM((blk, fw), F32)],
        compiler_params=_params(("parallel", "arbitrary")),
        name=f"hyena_filter_{seq_len}",
    )(zp, zm, tp, tm, keep.reshape(-1, 1), w1p, jnp.tile(b1.reshape(1, fw), (1, 2)), jnp.tile(freq, (1, 2)),
      w2_pair, jnp.tile(b2.reshape(1, fw), (1, 2)), w3, w3, deltas, sign, cvec, cs)


HY_GROUP = 2
HY_BLOCK = 512
HY_STRIP = 16
HY_HALO = 16


def _dwconv3_rows(p_ref, base, r, nrows, seq_len, w_ref, b_ref):
    lo, hi = max(r - HY_HALO, 0), min(r + nrows + HY_HALO, seq_len)
    win = p_ref[base + lo:base + hi, :].astype(F32)
    n = hi - lo
    prev = pltpu.roll(win, 1, 0)
    nxt = pltpu.roll(win, n - 1, 0)
    rows = lax.broadcasted_iota(jnp.int32, (n, 1), 0)
    if lo == 0:
        prev = jnp.where(rows == 0, 0.0, prev)
    if hi == seq_len:
        nxt = jnp.where(rows == n - 1, 0.0, nxt)
    y = w_ref[0:1, :] * prev + w_ref[1:2, :] * win + w_ref[2:3, :] * nxt + b_ref[...]
    return y[r - lo:r - lo + nrows, :]


def _hyena_mix_kernel(pv_ref, p1_ref, p2_ref, wv_ref, w1_ref, w2_ref, bv_ref, b1_ref, b2_ref,
                      cs_ref, hr1_ref, hi1_ref, hn1_ref, hr2_ref, hi2_ref, hn2_ref, skip_ref,
                      o_ref, z_ref, zb_ref, fa_ref, fb_ref, y_ref, *, seq_len, blk):
    dc = pv_ref.shape[1]
    groups = pv_ref.shape[0] // seq_len
    nblk = seq_len // blk
    tile = lambda a: jnp.concatenate([a] * groups, axis=1)
    rows = lax.broadcasted_iota(jnp.int32, (blk, 1), 0)
    sign = (1 - 2 * (rows % 2)).astype(F32)

    def stage(r, val):
        z_ref[r:r + blk, :] = val
        zb_ref[r:r + blk, :] = val.astype(BF16)
        return jnp.sum(val * sign, axis=0, keepdims=True)

    def block_products(i_out, hr_ref, hi_ref):
        def strip(r0):
            yr, yi = [None] * groups, [None] * groups
            for j in range(nblk):
                h0 = (i_out - j + nblk - 1) * blk
                hr = hr_ref[h0 + r0:h0 + r0 + HY_STRIP, :]
                hi = hi_ref[h0 + r0:h0 + r0 + HY_STRIP, :]
                for g in range(groups):
                    a = fa_ref[j * blk + r0:j * blk + r0 + HY_STRIP, g * dc:(g + 1) * dc]
                    b = fb_ref[j * blk + r0:j * blk + r0 + HY_STRIP, g * dc:(g + 1) * dc]
                    tr, ti = a * hr - b * hi, a * hi + b * hr
                    yr[g], yi[g] = (tr, ti) if yr[g] is None else (yr[g] + tr, yi[g] + ti)
            for g in range(groups):
                y_ref[r0:r0 + HY_STRIP, g * dc:(g + 1) * dc] = yr[g].astype(BF16)
                y_ref[blk + r0:blk + r0 + HY_STRIP, g * dc:(g + 1) * dc] = yi[g].astype(BF16)

        for r0 in range(0, blk, HY_STRIP):
            strip(r0)

    def transform(hr_ref, hi_ref, hn_ref, skip, mult_ref, mw_ref, mb_ref, nyqs, emit):
        for j in range(nblk):
            r = j * blk
            fa_ref[r:r + blk, :] = _dot(cs_ref[:, 0:blk], zb_ref[r:r + blk, :])
            fb_ref[r:r + blk, :] = _dot(cs_ref[:, blk:2 * blk], zb_ref[r:r + blk, :])
        skip_t = tile(skip)
        out = []
        for i in range(nblk):
            r = i * blk
            block_products(i, hr_ref, hi_ref)
            nyq = None
            for j in range(nblk):
                term = nyqs[j] * tile(hn_ref[i - j + nblk - 1])
                nyq = term if nyq is None else nyq + term
            y = _dot(cs_ref[...], y_ref[...]) + sign * nyq + z_ref[r:r + blk, :] * skip_t
            mult = jnp.concatenate([_dwconv3_rows(mult_ref, g * seq_len, r, blk, seq_len, mw_ref, mb_ref)
                                    for g in range(groups)], axis=1)
            out.append(emit(r, mult * y))
        return out

    nyqs = []
    for j in range(nblk):
        v = jnp.concatenate([_dwconv3_rows(pv_ref, g * seq_len, j * blk, blk, seq_len, wv_ref, bv_ref)
                             for g in range(groups)], axis=1)
        nyqs.append(stage(j * blk, v))

    nyqs = transform(hr1_ref, hi1_ref, hn1_ref, skip_ref[0:1, :], p1_ref, w1_ref, b1_ref, nyqs, stage)

    def write_out(r, val):
        for g in range(groups):
            o_ref[g * seq_len + r:g * seq_len + r + blk, :] = val[:, g * dc:(g + 1) * dc].astype(BF16)

    transform(hr2_ref, hi2_ref, hn2_ref, skip_ref[1:2, :], p2_ref, w2_ref, b2_ref, nyqs, write_out)


def hyena_mix(geo, p, conv_w, conv_b, skip, tables, spectrum, seq_len, row0, nb):
    d = geo.d
    cs = tables[0]
    hr, hi, hn = spectrum
    blk = cs.shape[0]
    ndl = 2 * (seq_len // blk) - 1
    nd = d // DC
    group = max(HY_GROUP, 2048 // seq_len)
    rows = group * seq_len
    width = group * DC
    assert nb % group == 0 and row0 % rows == 0 and seq_len % blk == 0 and blk % HY_STRIP == 0
    blk0 = row0 // rows
    pspec = lambda part: pl.BlockSpec((rows, DC), lambda c, b: (blk0 + b, part * nd + c))
    wspec = lambda part: pl.BlockSpec((3, DC), lambda c, b: (0, part * nd + c))
    bspec = lambda part: pl.BlockSpec((1, DC), lambda c, b: (0, part * nd + c))
    hspec = lambda order: pl.BlockSpec((ndl * blk, DC), lambda c, b: (0, order * nd + c),
                                       pipeline_mode=pl.Buffered(1))
    nspec = lambda order: pl.BlockSpec((ndl, 1, DC), lambda c, b: (0, 0, order * nd + c))
    return pl.pallas_call(
        functools.partial(_hyena_mix_kernel, seq_len=seq_len, blk=blk),
        out_shape=jax.ShapeDtypeStruct((nb * seq_len, d), BF16),
        grid=(nd, nb // group),
        in_specs=[pspec(0), pspec(1), pspec(2), wspec(0), wspec(1), wspec(2), bspec(0), bspec(1), bspec(2),
                  _resident((blk, 2 * blk)),
                  hspec(0), hspec(0), nspec(0), hspec(1), hspec(1), nspec(1),
                  pl.BlockSpec((2, DC), lambda c, b: (0, c))],
        out_specs=pl.BlockSpec((rows, DC), lambda c, b: (b, c)),
        scratch_shapes=[pltpu.VMEM((seq_len, width), F32), pltpu.VMEM((seq_len, width), BF16),
                        pltpu.VMEM((seq_len, width), F32), pltpu.VMEM((seq_len, width), F32),
                        pltpu.VMEM((2 * blk, width), BF16)],
        compiler_params=_params(("arbitrary", "arbitrary")),
        name=f"hyena_mix_{seq_len}",
    )(p, p, p, conv_w, conv_w, conv_w, conv_b, conv_b, conv_b, cs,
      hr, hi, hn, hr, hi, hn, skip)


CONV_ROWS = 64
CONV_PAD = 16


def _stage_padded(src, pad_ref):
    n = src.shape[0]
    zeros = jnp.zeros((CONV_PAD, pad_ref.shape[1]), F32)
    pad_ref[0:CONV_PAD, :] = zeros
    pad_ref[CONV_PAD + n:2 * CONV_PAD + n, :] = zeros
    pad_ref[CONV_PAD:CONV_PAD + n, :] = src


def _dwconv_chunks(pad_ref, w_ref, n, ktaps, emit):
    half = ktaps // 2
    lo = (CONV_PAD - half) // SUBLANES * SUBLANES
    hi = -(-(CONV_PAD + half + CONV_ROWS) // SUBLANES) * SUBLANES
    rows = hi - lo

    def chunk(ci, carry):
        r0 = pl.multiple_of(ci * CONV_ROWS, CONV_ROWS)
        window = pad_ref[pl.ds(pl.multiple_of(r0 + lo, SUBLANES), rows), :]
        acc = None
        for k in range(ktaps):
            off = CONV_PAD - half + k - lo
            win = window if off == 0 else pltpu.roll(window, rows - off, 0)
            term = w_ref[k:k + 1, :] * win[0:CONV_ROWS, :]
            acc = term if acc is None else acc + term
        emit(r0, acc)
        return carry

    lax.fori_loop(0, n // CONV_ROWS, chunk, 0)


def _conv_group(seq_len, nb):
    group = max(1, min(nb, 1024 // seq_len))
    assert nb % group == 0
    return group


def _cf_conv_kernel(u_ref, w_ref, b_ref, o_ref, pad_ref, *, seq_len):
    bias = b_ref[...]
    for base in range(0, u_ref.shape[0], seq_len):
        _stage_padded(u_ref[base:base + seq_len, :].astype(F32), pad_ref)

        def emit(r0, acc, base=base):
            o_ref[pl.ds(base + r0, CONV_ROWS), :] = (acc + bias).astype(BF16)

        _dwconv_chunks(pad_ref, w_ref, seq_len, w_ref.shape[0], emit)


def conformer_dwconv(geo, u, w, b, seq_len, row0, nb):
    d = geo.d
    dc = 256
    ktaps = w.shape[0]
    rows = _conv_group(seq_len, nb) * seq_len
    blk0 = row0 // rows
    return pl.pallas_call(
        functools.partial(_cf_conv_kernel, seq_len=seq_len),
        out_shape=jax.ShapeDtypeStruct((nb * seq_len, d), BF16),
        grid=(nb * seq_len // rows, d // dc),
        in_specs=[pl.BlockSpec((rows, dc), lambda b_, c: (blk0 + b_, c)),
                  pl.BlockSpec((ktaps, dc), lambda b_, c: (0, c)),
                  pl.BlockSpec((1, dc), lambda b_, c: (0, c))],
        out_specs=pl.BlockSpec((rows, dc), lambda b_, c: (b_, c)),
        scratch_shapes=[pltpu.VMEM((seq_len + 2 * CONV_PAD, dc), F32)],
        compiler_params=_params(("parallel", "parallel")),
        name=f"conformer_dwconv_{seq_len}",
    )(u, w, b.reshape(1, d))


def _sc_conv_kernel(q_ref, bg_ref, w_ref, o_ref, pad_ref, *, seq_len):
    for base in range(0, q_ref.shape[0], seq_len):
        _stage_padded(q_ref[base:base + seq_len, :].astype(F32), pad_ref)

        def emit(r0, acc, base=base):
            gate = bg_ref[pl.ds(base + r0, CONV_ROWS), :].astype(F32)
            o_ref[pl.ds(base + r0, CONV_ROWS), :] = (gate * acc).astype(BF16)

        _dwconv_chunks(pad_ref, w_ref, seq_len, w_ref.shape[0], emit)


def shortconv_gate(geo, q, bg, w, seq_len, row0, nb):
    d = geo.d
    dc = 256
    ktaps = w.shape[0]
    rows = _conv_group(seq_len, nb) * seq_len
    blk0 = row0 // rows
    seq = pl.BlockSpec((rows, dc), lambda b_, c: (blk0 + b_, c))
    return pl.pallas_call(
        functools.partial(_sc_conv_kernel, seq_len=seq_len),
        out_shape=jax.ShapeDtypeStruct((nb * seq_len, d), BF16),
        grid=(nb * seq_len // rows, d // dc),
        in_specs=[seq, seq, pl.BlockSpec((ktaps, dc), lambda b_, c: (0, c))],
        out_specs=pl.BlockSpec((rows, dc), lambda b_, c: (b_, c)),
        scratch_shapes=[pltpu.VMEM((seq_len + 2 * CONV_PAD, dc), F32)],
        compiler_params=_params(("parallel", "parallel")),
        name=f"shortconv_gate_{seq_len}",
    )(q, bg, w)


def _mla_proj_kernel(x, mod_ref, gn_ref, wdq_ref, gq_ref, wun_ref, wup_ref, wus_ref,
                     wkc_ref, wkp_ref, wks_ref, gkv_ref, cosq_ref, sinq_ref,
                     q_ref, ckv_ref, kpe_ref, *, heads, nope, rope, qscale):
    h = _prenorm(x, mod_ref, gn_ref, 0)
    cos2, sin2 = cosq_ref[...], sinq_ref[...]
    cq_raw = _dot(h, wdq_ref[...])
    ckv_raw = _dot(h, wkc_ref[...])
    kpe = _dot(h, wkp_ref[...]) * cos2[:, 0:rope] + _dot(h, wks_ref[...]) * sin2[:, 0:rope]
    cq = (_rms(cq_raw) * gq_ref[...]).astype(BF16)
    qn = _dot(cq, wun_ref[...]) * qscale
    cosq, sinq = cos2 * qscale, sin2 * qscale
    for pair in range(heads // 2):
        lo = pair * 2 * rope
        qp = (_dot(cq, wup_ref[:, lo:lo + 2 * rope]) * cosq
              + _dot(cq, wus_ref[:, lo:lo + 2 * rope]) * sinq)
        for j in range(2):
            hd = 2 * pair + j
            q_ref[hd, :, 0:nope] = qn[:, hd * nope:(hd + 1) * nope].astype(BF16)
            q_ref[hd, :, nope:nope + rope] = qp[:, j * rope:(j + 1) * rope].astype(BF16)
    ckv_ref[...] = _rms(ckv_raw) * gkv_ref[...]
    kpe_ref[...] = kpe


def mla_project(geo, x, mod, gn, wdq, gq, wun, wup, wus, wkc, wkp, wks, gkv, cos2, sin2, heads, nope, rope,
                qscale):
    d = geo.d
    qr, kvr = wdq.shape[1], wkc.shape[1]
    row = lambda w: pl.BlockSpec((TM, w), lambda i: (i, 0))
    rot = pl.BlockSpec((TM, 2 * rope), lambda i: (_rope_block_of_tile(geo, i), 0))
    body = functools.partial(_mla_proj_kernel, heads=heads, nope=nope, rope=rope, qscale=qscale)
    return pl.pallas_call(
        _row_kernel(body, geo, x),
        out_shape=(jax.ShapeDtypeStruct((heads, geo.n, nope + rope), BF16),
                   jax.ShapeDtypeStruct((geo.n, kvr), F32),
                   jax.ShapeDtypeStruct((geo.n, rope), F32)),
        grid=(geo.n // TM,),
        in_specs=_row_specs(geo, x) + [_resident((d, qr)), _resident((1, qr)),
                                    _resident((qr, heads * nope)), _resident((qr, heads * rope)),
                                    _resident((qr, heads * rope)),
                                    _resident((d, kvr)), _resident((d, rope)), _resident((d, rope)),
                                    _resident((1, kvr)), rot, rot],
        out_specs=(pl.BlockSpec((heads, TM, nope + rope), lambda i: (0, i, 0)), row(kvr), row(rope)),
        compiler_params=_params(("parallel",)),
        name="mla_project",
    )(*_stream_args(x), mod, gn, wdq, gq.reshape(1, qr), wun, wup, wus, wkc, wkp, wks, gkv.reshape(1, kvr),
      cos2, sin2)


KV_ROWS = 256


def _mla_kv_kernel(*refs, heads, nope, rope, past_tiles, sub):
    wn_ref, wv_ref, k_ref, v_ref = refs[-4:]
    new = refs[-4 - 2 * sub:-4]
    for u in range(sub):
        rows = slice(u * KV_ROWS, (u + 1) * KV_ROWS)
        ckv, kpe = new[2 * u][...], new[2 * u + 1][...]
        if past_tiles:
            cached = pl.program_id(1) * sub + u < past_tiles
            ckv = jnp.where(cached, refs[0][...], ckv)
            kpe = jnp.where(cached, refs[1][...], kpe)
        c = ckv.astype(BF16)
        kn = _dot(c, wn_ref[...])
        v_ref[rows, :] = _dot(c, wv_ref[...]).astype(BF16)
        kp = kpe.astype(BF16)
        for hd in range(heads):
            k_ref[hd, rows, 0:nope] = kn[:, hd * nope:(hd + 1) * nope].astype(BF16)
            k_ref[hd, rows, nope:nope + rope] = kp


def mla_expand_kv(ckv, kpe, cache, layer, seq_len, row0, nb, wn, wv, heads, nope, rope, vdim):
    r = ckv.shape[1]
    past = 0 if cache is None else cache[0].shape[2]
    assert past % KV_ROWS == 0 and seq_len % KV_ROWS == 0 and row0 % KV_ROWS == 0
    pt, st, blk0 = past // KV_ROWS, seq_len // KV_ROWS, row0 // KV_ROWS
    lk = past + seq_len
    tiles = lk // KV_ROWS
    sub = 3 if (tiles % 3 == 0 and pt <= 1) else 1
    specs, args = [], []
    for u in range(sub):
        new_map = lambda b, t, u=u: (blk0 + b * st + jnp.maximum(t * sub + u - pt, 0), 0)
        specs += [pl.BlockSpec((KV_ROWS, r), new_map), pl.BlockSpec((KV_ROWS, rope), new_map)]
        args += [ckv, kpe]
    specs += [_resident((r, heads * nope)), _resident((r, heads * vdim))]
    args += [wn, wv]
    if cache is not None:
        old_map = lambda b, t: (b, layer, jnp.minimum(t * sub, pt - 1), 0)
        specs = [pl.BlockSpec((None, None, KV_ROWS, r), old_map),
                 pl.BlockSpec((None, None, KV_ROWS, rope), old_map)] + specs
        args = list(cache) + args
    return pl.pallas_call(
        functools.partial(_mla_kv_kernel, heads=heads, nope=nope, rope=rope, past_tiles=pt, sub=sub),
        out_shape=(jax.ShapeDtypeStruct((nb, heads, lk, nope + rope), BF16),
                   jax.ShapeDtypeStruct((nb, lk, heads * vdim), BF16)),
        grid=(nb, tiles // sub),
        in_specs=specs,
        out_specs=(pl.BlockSpec((None, heads, sub * KV_ROWS, nope + rope), lambda b, t: (b, 0, t, 0)),
                   pl.BlockSpec((None, sub * KV_ROWS, heads * vdim), lambda b, t: (b, t, 0))),
        compiler_params=_params(("parallel", "arbitrary")),
        name=f"mla_expand_kv_{lk}",
    )(*args)


ATTN_ROWS = 256


def _attn_kernel(q_ref, k_ref, v_ref, o_ref, *, vdim):
    hg, tq, _ = q_ref.shape
    chains = [(h, r) for h in range(hg) for r in range(0, tq, ATTN_ROWS)]

    def scores(h, r):
        return lax.dot_general(q_ref[h, r:r + ATTN_ROWS, :], k_ref[h], (((1,), (1,)), ((), ())),
                               preferred_element_type=F32)

    s_next = scores(*chains[0])
    for i, (h, r) in enumerate(chains):
        s = s_next
        if i + 1 < len(chains):
            s_next = scores(*chains[i + 1])
        m = jnp.max(s, axis=-1, keepdims=True)
        p = jnp.exp2(s - m)
        l = jnp.sum(p, axis=-1, keepdims=True)
        o = _dot(p.astype(BF16), v_ref[:, h * vdim:(h + 1) * vdim])
        o_ref[r:r + ATTN_ROWS, h * vdim:(h + 1) * vdim] = (o * (1.0 / l)).astype(BF16)


def mla_attention(q, k, v, seq_len, row0, nb, vdim):
    heads, _, dq = q.shape
    lk = k.shape[2]
    hg = max(1, min(heads, 2048 // seq_len))
    blk0 = row0 // seq_len
    return pl.pallas_call(
        functools.partial(_attn_kernel, vdim=vdim),
        out_shape=jax.ShapeDtypeStruct((nb * seq_len, heads * vdim), BF16),
        grid=(nb, heads // hg),
        in_specs=[pl.BlockSpec((hg, seq_len, dq), lambda b, h: (h, blk0 + b, 0)),
                  pl.BlockSpec((None, hg, lk, dq), lambda b, h: (b, h, 0, 0)),
                  pl.BlockSpec((None, lk, hg * vdim), lambda b, h: (b, 0, h))],
        out_specs=pl.BlockSpec((seq_len, hg * vdim), lambda b, h: (b, h)),
        compiler_params=_params(("parallel", "parallel")),
        name=f"mla_attention_{seq_len}",
    )(q, k, v)


def rope_tables(geo, rope):
    nf = rope // 4
    rows = geo.l_lat // GRID_W
    row = jnp.repeat(jnp.arange(rows, dtype=F32), GRID_W)
    col = jnp.tile(jnp.arange(GRID_W, dtype=F32), rows)
    inv = jnp.exp(-math.log(ROPE_THETA) * jnp.arange(nf, dtype=F32) * (4.0 / rope))
    ar, ac = row[:, None] * inv, col[:, None] * inv
    cos = jnp.concatenate([jnp.cos(ar), jnp.cos(ar), jnp.cos(ac), jnp.cos(ac)], axis=-1)
    sin = jnp.concatenate([-jnp.sin(ar), jnp.sin(ar), -jnp.sin(ac), jnp.sin(ac)], axis=-1)
    cos = jnp.concatenate([jnp.ones((TM, rope), F32), cos], axis=0)
    sin = jnp.concatenate([jnp.zeros((TM, rope), F32), sin], axis=0)
    return jnp.tile(cos, (1, 2)), jnp.tile(sin, (1, 2))


def _rope_block_of_tile(geo, i):
    nct, per_seq = geo.n_ctx // TM, geo.l_lat // TM
    return jnp.where(i < nct, 0, 1 + (i - nct) % per_seq)


def _swap_rope_halves(w, rope):
    nf = rope // 4
    shape = w.shape
    w = w.reshape(shape[:-1] + (shape[-1] // (2 * nf), 2, nf))
    return w[..., ::-1, :].reshape(shape)


def kernel(x_prompt, x_sample, c, cache_ckv, cache_kpe, c_ctx, w_mod, b_mod, norm_g, ffn_w_gate, ffn_w_up, ffn_w_down, hy_w_in, hy_b_in, hy_conv_w, hy_conv_b, hy_f_w1, hy_f_b1, hy_f_freq, hy_f_w2, hy_f_b2, hy_f_w3, hy_skip, hy_w_out, hy_b_out, cf_w_pw1, cf_b_pw1, cf_dw_w, cf_dw_b, cf_ln_g, cf_ln_b, cf_w_pw2, cf_b_pw2, sc_w_in, sc_conv_w, sc_w_out, mla_w_dq, mla_g_q, mla_w_uq, mla_w_dkv, mla_g_kv, mla_w_ukv, mla_w_o):
    nb_ctx, l_ctx, d = x_prompt.shape
    nb_lat, l_lat, _ = x_sample.shape
    geo = Geometry(nb_ctx, l_ctx, nb_lat, l_lat, d)
    depth = w_mod.shape[0]
    n_mixers = 4

    x = (x_prompt.reshape(geo.n_ctx, d), x_sample.reshape(geo.n_lat, d))
    cond = jnp.concatenate([c_ctx[None, :], c], axis=0)
    n_cond = cond.shape[0]
    cond = jnp.pad(cond, ((0, -n_cond % 8), (0, 0)))
    mod_all = modulation_all(cond, w_mod, b_mod)[:, :n_cond].reshape(depth, n_cond, 6, d)

    passes = ((l_ctx, 0, nb_ctx), (l_lat, geo.n_ctx, nb_lat))
    zero_bias = jnp.zeros((d,), F32)
    ffn_wg, ffn_wu, ffn_wd = ffn_w_gate.astype(BF16), ffn_w_up.astype(BF16), ffn_w_down.astype(BF16)
    new_ckv, new_kpe = [], []

    for i in range(depth):
        m, j = i % n_mixers, i // n_mixers
        mod, gn = mod_all[i], norm_g[i]
        if m == 0:
            p = hyena_in_proj(geo, x, mod, gn, hy_w_in[j].astype(BF16), hy_b_in[j])
            parts = []
            for seq_len, row0, nb in passes:
                blk = min(seq_len, HY_BLOCK)
                tables = dft_tables(blk)
                spectrum = hyena_block_spectra(seq_len, blk, d, tables, hy_f_w1[j], hy_f_b1[j], hy_f_freq[j],
                                               hy_f_w2[j], hy_f_b2[j], hy_f_w3[j])
                parts.append(hyena_mix(geo, p, hy_conv_w[j], hy_conv_b[j].reshape(1, -1), hy_skip[j],
                                       tables, spectrum, seq_len, row0, nb))
            a, w_out, b_out, ln = tuple(parts), hy_w_out[j], hy_b_out[j], None
        elif m == 1:
            u = conformer_in_proj(geo, x, mod, gn, cf_w_pw1[j].astype(BF16), cf_b_pw1[j])
            a = tuple(conformer_dwconv(geo, u, cf_dw_w[j], cf_dw_b[j], seq_len, row0, nb)
                      for seq_len, row0, nb in passes)
            w_out, b_out, ln = cf_w_pw2[j], cf_b_pw2[j], (cf_ln_g[j], cf_ln_b[j])
        elif m == 2:
            bg, q = shortconv_in_proj(geo, x, mod, gn, sc_w_in[j].astype(BF16))
            a = tuple(shortconv_gate(geo, q, bg, sc_conv_w[j], seq_len, row0, nb)
                      for seq_len, row0, nb in passes)
            w_out, b_out, ln = sc_w_out[j], zero_bias, None
        else:
            kvr = mla_g_kv.shape[1]
            rope = mla_w_dkv.shape[2] - kvr
            vdim = 128
            heads = mla_w_o.shape[1] // vdim
            nope = mla_w_uq.shape[2] // heads - rope
            wuq = mla_w_uq[j].reshape(-1, heads, nope + rope)
            wun = wuq[:, :, :nope].reshape(-1, heads * nope).astype(BF16)
            wup = wuq[:, :, nope:].reshape(-1, heads * rope)
            wukv = mla_w_ukv[j].reshape(kvr, heads, nope + vdim)
            wkn = wukv[:, :, :nope].reshape(kvr, heads * nope).astype(BF16)
            wkv = wukv[:, :, nope:].reshape(kvr, heads * vdim).astype(BF16)
            wkc, wkp = mla_w_dkv[j][:, :kvr], mla_w_dkv[j][:, kvr:]
            cos2, sin2 = rope_tables(geo, rope)
            q, ckv, kpe = mla_project(geo, x, mod, gn, mla_w_dq[j].astype(BF16), mla_g_q[j], wun,
                                      wup.astype(BF16), _swap_rope_halves(wup, rope).astype(BF16),
                                      wkc.astype(BF16), wkp.astype(BF16),
                                      _swap_rope_halves(wkp, rope).astype(BF16), mla_g_kv[j],
                                      cos2, sin2, heads, nope, rope,
                                      (nope + rope) ** -0.5 * math.log2(math.e))
            new_ckv.append(ckv[:geo.n_ctx].reshape(nb_ctx, l_ctx, kvr))
            new_kpe.append(kpe[:geo.n_ctx].reshape(nb_ctx, l_ctx, rope))
            parts = []
            for (seq_len, row0, nb), cache in zip(passes, (None, (cache_ckv, cache_kpe))):
                kk, vv = mla_expand_kv(ckv, kpe, cache, j, seq_len, row0, nb, wkn, wkv, heads, nope, rope, vdim)
                parts.append(mla_attention(q, kk, vv, seq_len, row0, nb, vdim))
            a, w_out, b_out, ln = tuple(parts), mla_w_o[j], zero_bias, None
        x = layer_tail(geo, a, x, mod, gn, w_out.astype(BF16), b_out, ln, ffn_wg, ffn_wu, ffn_wd, i,
                       split=(i == depth - 1))

    y_prompt = x[0].reshape(nb_ctx, l_ctx, d)
    y_sample = x[1].reshape(nb_lat, l_lat, d)
    return (y_prompt, y_sample, jnp.stack(new_ckv, axis=1), jnp.stack(new_kpe, axis=1))
```

```python
import functools
import math

import jax
import jax.numpy as jnp
from jax import lax
from jax.experimental import pallas as pl
from jax.experimental.pallas import tpu as pltpu

F32 = jnp.float32
BF16 = jnp.bfloat16

NORM_EPS = 1e-6
GRID_W = 64
ROPE_THETA = 10000.0
HY_DECAY_TARGET = 1e-2
HY_FAST_PCT = 0.3
HY_SLOW_PCT = 1.5

V7X_VMEM_LIMIT_BYTES = 60000 * 1024
LANE = 128
SUBLANES = 8

TM = 512
DC = 256


def _dot(a, b):
    return jnp.dot(a, b, preferred_element_type=F32)


def _dot_hi(a, b):
    return jnp.dot(a, b, preferred_element_type=F32, precision=lax.Precision.HIGHEST)


def _rms(x):
    return x * lax.rsqrt(jnp.mean(x * x, axis=-1, keepdims=True) + NORM_EPS)


def _sigmoid(x):
    return 1.0 / (1.0 + jnp.exp(-x))


def _silu(x):
    return x * _sigmoid(x)


def _params(semantics):
    return pltpu.CompilerParams(dimension_semantics=semantics,
                                vmem_limit_bytes=V7X_VMEM_LIMIT_BYTES)


def _resident(shape):
    zeros = (0,) * len(shape)
    return pl.BlockSpec(shape, lambda *_: zeros, pipeline_mode=pl.Buffered(1))


class Geometry:
    def __init__(self, nb_ctx, l_ctx, nb_lat, l_lat, d):
        self.nb_ctx, self.l_ctx, self.nb_lat, self.l_lat, self.d = nb_ctx, l_ctx, nb_lat, l_lat, d
        self.n_ctx = nb_ctx * l_ctx
        self.n_lat = nb_lat * l_lat
        self.n = self.n_ctx + self.n_lat
        assert self.n_ctx % TM == 0 and l_lat % TM == 0 and self.n_ctx % l_lat == 0

    def cond_of_tile(self, i):
        row = i * TM
        return jnp.where(row < self.n_ctx, 0, 1 + (row - self.n_ctx) // self.l_lat)


def _mod_kernel(c_ref, w_ref, b_ref, o_ref):
    a = _silu(c_ref[...]).astype(BF16)
    o_ref[...] = _dot(a, w_ref[...].astype(BF16)) + b_ref[...]


def modulation_all(cond, w_mod, b_mod):
    depth, d, d6 = w_mod.shape
    r = cond.shape[0]
    tn = 1536
    return pl.pallas_call(
        _mod_kernel,
        out_shape=jax.ShapeDtypeStruct((depth, r, d6), F32),
        grid=(depth, d6 // tn),
        in_specs=[pl.BlockSpec((r, d), lambda l, j: (0, 0)),
                  pl.BlockSpec((None, d, tn), lambda l, j: (l, 0, j)),
                  pl.BlockSpec((None, 1, tn), lambda l, j: (l, 0, j))],
        out_specs=pl.BlockSpec((None, r, tn), lambda l, j: (l, 0, j)),
        compiler_params=_params(("parallel", "parallel")),
        name="modulation",
    )(cond, w_mod, b_mod.reshape(depth, 1, d6))


def _stream_args(op):
    return list(op) if isinstance(op, tuple) else [op]


def _stream_specs(geo, op, width):
    if not isinstance(op, tuple):
        return [pl.BlockSpec((TM, width), lambda i: (i, 0))]
    nct = geo.n_ctx // TM
    return [pl.BlockSpec((TM, width), lambda i: (jnp.minimum(i, nct - 1), 0)),
            pl.BlockSpec((TM, width), lambda i: (jnp.maximum(i - nct, 0), 0))]


def _stream_tile(refs, nct):
    if len(refs) == 1:
        return refs[0][...]
    return jnp.where(pl.program_id(0) < nct, refs[0][...], refs[1][...])


def _prenorm(x, mod_ref, gn_ref, which):
    g = gn_ref[2 * which:2 * which + 1, :]
    sh = mod_ref[3 * which:3 * which + 1, :]
    sc = mod_ref[3 * which + 1:3 * which + 2, :]
    return ((_rms(x) * g) * (1.0 + sc) + sh).astype(BF16)


def _row_specs(geo, x):
    d = geo.d
    return _stream_specs(geo, x, d) + [
        pl.BlockSpec((None, 6, d), lambda i: (geo.cond_of_tile(i), 0, 0)),
        pl.BlockSpec((4, d), lambda i: (0, 0))]


def _row_kernel(body, geo, x):
    nx, nct = len(_stream_args(x)), geo.n_ctx // TM

    def kern(*refs):
        body(_stream_tile(refs[:nx], nct), *refs[nx:])

    return kern


def _side_cast_plan(stacks, layer, steps):
    in_specs, out_specs, shapes = [], [], []
    for w in stacks:
        _, r, c = w.shape
        s = max(k for k in range(1, steps + 1) if r % k == 0 and (r // k) % 16 == 0)
        in_specs.append(pl.BlockSpec((None, r // s, c), lambda i, s=s: (layer, jnp.minimum(i, s - 1), 0)))
        out_specs.append(pl.BlockSpec((r // s, c), lambda i, s=s: (jnp.minimum(i, s - 1), 0)))
        shapes.append(jax.ShapeDtypeStruct((r, c), BF16))
    return in_specs, out_specs, shapes


def _side_cast(in_refs, out_refs):
    for i_ref, o_ref in zip(in_refs, out_refs):
        o_ref[...] = i_ref[...].astype(BF16)


def _hy_proj_kernel(x, mod_ref, gn_ref, w_ref, b_ref, *rest):
    n_cast = (len(rest) - 1) // 2
    o_ref = rest[n_cast]
    _side_cast(rest[:n_cast], rest[n_cast + 1:])
    h = _prenorm(x, mod_ref, gn_ref, 0)
    step = 512
    for c in range(0, o_ref.shape[1], step):
        o_ref[:, c:c + step] = (_dot(h, w_ref[:, c:c + step]) + b_ref[:, c:c + step]).astype(BF16)


def hyena_in_proj(geo, x, mod, gn, w_in, b_in, cast_stacks=(), cast_layer=0):
    d, d3 = w_in.shape
    steps = geo.n // TM
    c_in, c_out, c_shapes = _side_cast_plan(cast_stacks, cast_layer, steps)
    out = pl.pallas_call(
        _row_kernel(_hy_proj_kernel, geo, x),
        out_shape=[jax.ShapeDtypeStruct((geo.n, d3), BF16)] + c_shapes,
        grid=(steps,),
        in_specs=_row_specs(geo, x) + [_resident((d, d3)), _resident((1, d3))] + c_in,
        out_specs=[pl.BlockSpec((TM, d3), lambda i: (i, 0))] + c_out,
        compiler_params=_params(("arbitrary",)),
        name="hyena_in_proj",
    )(*_stream_args(x), mod, gn, w_in, b_in.reshape(1, d3), *cast_stacks)
    return out[0], list(out[1:])


def _cf_proj_kernel(x, mod_ref, gn_ref, w_ref, b_ref, o_ref):
    h = _prenorm(x, mod_ref, gn_ref, 0)
    d = o_ref.shape[1]
    step = 512
    for c in range(0, d, step):
        a = _dot(h, w_ref[:, c:c + step]) + b_ref[:, c:c + step]
        g = _dot(h, w_ref[:, d + c:d + c + step]) + b_ref[:, d + c:d + c + step]
        o_ref[:, c:c + step] = (a * _sigmoid(g)).astype(BF16)


def conformer_in_proj(geo, x, mod, gn, w_pw1, b_pw1):
    d, d2 = w_pw1.shape
    return pl.pallas_call(
        _row_kernel(_cf_proj_kernel, geo, x),
        out_shape=jax.ShapeDtypeStruct((geo.n, d), BF16),
        grid=(geo.n // TM,),
        in_specs=_row_specs(geo, x) + [_resident((d, d2)), _resident((1, d2))],
        out_specs=pl.BlockSpec((TM, d), lambda i: (i, 0)),
        compiler_params=_params(("parallel",)),
        name="conformer_in_proj",
    )(*_stream_args(x), mod, gn, w_pw1, b_pw1.reshape(1, d2))


def _sc_proj_kernel(x, mod_ref, gn_ref, w_ref, bg_ref, q_ref):
    h = _prenorm(x, mod_ref, gn_ref, 0)
    d = bg_ref.shape[1]
    step = 512
    for c in range(0, d, step):
        bg_ref[:, c:c + step] = _dot(h, w_ref[:, c:c + step]).astype(BF16)
        cg = _dot(h, w_ref[:, d + c:d + c + step])
        hv = _dot(h, w_ref[:, 2 * d + c:2 * d + c + step])
        q_ref[:, c:c + step] = (cg * hv).astype(BF16)


def shortconv_in_proj(geo, x, mod, gn, w_in):
    d, d3 = w_in.shape
    out = jax.ShapeDtypeStruct((geo.n, d), BF16)
    return pl.pallas_call(
        _row_kernel(_sc_proj_kernel, geo, x),
        out_shape=(out, out),
        grid=(geo.n // TM,),
        in_specs=_row_specs(geo, x) + [_resident((d, d3))],
        out_specs=(pl.BlockSpec((TM, d), lambda i: (i, 0)), pl.BlockSpec((TM, d), lambda i: (i, 0))),
        compiler_params=_params(("parallel",)),
        name="shortconv_in_proj",
    )(*_stream_args(x), mod, gn, w_in)


TAIL_CHAINS = 2


def _layer_tail_kernel(a, x, mod_ref, gn_ref, wo_ref, bo_ref, lg_ref, lb_ref, wg_ref, wu_ref, wd_ref,
                       *rest, nct, layer_norm, n_cast):
    acts = rest[-TAIL_CHAINS:]
    outs = rest[n_cast:len(rest) - TAIL_CHAINS - n_cast]
    rows = a.shape[0] // TAIL_CHAINS
    groups = [slice(i * rows, (i + 1) * rows) for i in range(TAIL_CHAINS)]
    if layer_norm:
        u = a.astype(F32)
        mu = jnp.mean(u, axis=-1, keepdims=True)
        uc = u - mu
        var = jnp.mean(uc * uc, axis=-1, keepdims=True)
        a = _silu((uc * lax.rsqrt(var + NORM_EPS)) * lg_ref[...] + lb_ref[...]).astype(BF16)
    ys = [_dot(a[g, :], wo_ref[...]) + bo_ref[...] for g in groups]
    xs = [x[g, :] + mod_ref[2:3, :] * (_rms(y) * gn_ref[1:2, :]) for g, y in zip(groups, ys)]
    hs = [_prenorm(xg, mod_ref, gn_ref, 1) for xg in xs]
    step = 256
    for c in range(0, acts[0].shape[1], step):
        for h, act_ref in zip(hs, acts):
            gate = _dot(h, wg_ref[:, c:c + step])
            up = _dot(h, wu_ref[:, c:c + step])
            act_ref[:, c:c + step] = (_silu(gate) * up).astype(BF16)
    ys = [_dot(act_ref[...], wd_ref[...]) for act_ref in acts]
    _side_cast(rest[:n_cast], rest[len(rest) - TAIL_CHAINS - n_cast:len(rest) - TAIL_CHAINS])
    res = [xg + mod_ref[5:6, :] * (_rms(y) * gn_ref[3:4, :]) for xg, y in zip(xs, ys)]

    def write(o_ref):
        for g, val in zip(groups, res):
            o_ref[g, :] = val

    if len(outs) == 1:
        write(outs[0])
    else:
        i = pl.program_id(0)
        pl.when(i < nct)(lambda: write(outs[0]))
        pl.when(i >= nct)(lambda: write(outs[1]))


def layer_tail(geo, a, x, mod, gn, w_out, b_out, ln, wg, wu, wd, split=False, cast_stacks=(), cast_layer=0):
    d, dff = wg.shape
    steps = geo.n // TM
    na, nx, nct = len(_stream_args(a)), len(_stream_args(x)), geo.n_ctx // TM
    c_in, c_out, c_shapes = _side_cast_plan(cast_stacks, cast_layer, steps)
    body = functools.partial(_layer_tail_kernel, nct=nct, layer_norm=ln is not None, n_cast=len(c_in))

    def kern(*refs):
        body(_stream_tile(refs[:na], nct), _stream_tile(refs[na:na + nx], nct), *refs[na + nx:])

    if ln is None:
        ln = (jnp.ones((d,), F32), jnp.zeros((d,), F32))
    if split:
        out_shape = [jax.ShapeDtypeStruct((geo.n_ctx, d), F32), jax.ShapeDtypeStruct((geo.n_lat, d), F32)]
        out_specs = _stream_specs(geo, (None, None), d)
    else:
        out_shape = [jax.ShapeDtypeStruct((geo.n, d), F32)]
        out_specs = [pl.BlockSpec((TM, d), lambda i: (i, 0))]
    n_main = len(out_shape)
    out = pl.pallas_call(
        kern,
        out_shape=out_shape + c_shapes,
        grid=(steps,),
        in_specs=_stream_specs(geo, a, d) + _row_specs(geo, x) + [
            _resident((d, d)), _resident((1, d)), _resident((1, d)), _resident((1, d)),
            _resident((d, dff)), _resident((d, dff)), _resident((dff, d))] + c_in,
        out_specs=out_specs + c_out,
        scratch_shapes=[pltpu.VMEM((TM // TAIL_CHAINS, dff), BF16)] * TAIL_CHAINS,
        compiler_params=_params(("arbitrary",)),
        name="layer_tail",
    )(*_stream_args(a), *_stream_args(x), mod, gn, w_out, b_out.reshape(1, d), ln[0].reshape(1, d),
      ln[1].reshape(1, d), wg, wu, wd, *cast_stacks)
    main = tuple(out[:n_main]) if split else out[0]
    return main, list(out[n_main:])


DFT_ROWS = 256


def _dft_table_kernel(cb_ref, sb_ref, o_ref, *, seq_len):
    k0 = pl.program_id(0) * DFT_ROWS
    s = lax.broadcasted_iota(jnp.int32, (1, seq_len), 1)
    ang = ((k0 * s) & (2 * seq_len - 1)).astype(F32) * (math.pi / seq_len)
    ca, sa = jnp.cos(ang), jnp.sin(ang)
    cb, sb = cb_ref[...], sb_ref[...]
    o_ref[:, 0:seq_len] = (ca * cb - sa * sb).astype(BF16)
    o_ref[:, seq_len:2 * seq_len] = (sa * cb + ca * sb).astype(BF16)


def dft_tables(seq_len):
    assert seq_len & (seq_len - 1) == 0 and seq_len % DFT_ROWS == 0
    k = jnp.arange(seq_len, dtype=jnp.int32)
    idx = (k[:DFT_ROWS, None] * k[None, :]) % (2 * seq_len)
    ang = idx.astype(F32) * (math.pi / seq_len)
    sign = (1 - 2 * (k % 2)).astype(F32)[:, None]
    cvec = jnp.where(k == 0, 0.5 / seq_len, 1.0 / seq_len).astype(F32)[:, None]
    base = pl.BlockSpec((DFT_ROWS, seq_len), lambda i: (0, 0))
    cs = pl.pallas_call(
        functools.partial(_dft_table_kernel, seq_len=seq_len),
        out_shape=jax.ShapeDtypeStruct((seq_len, 2 * seq_len), BF16),
        grid=(seq_len // DFT_ROWS,),
        in_specs=[base, base],
        out_specs=pl.BlockSpec((DFT_ROWS, 2 * seq_len), lambda i: (i, 0)),
        compiler_params=_params(("parallel",)),
        name=f"dft_table_{seq_len}",
    )(jnp.cos(ang), jnp.sin(ang))
    return cs, sign, cvec


def _filter_kernel(zp_ref, zm_ref, tp_ref, tm_ref, keep_ref, w1_ref, b1_ref, fr_ref, w2_ref, b2_ref,
                   w3p_ref, w3m_ref, dl_ref, sign_ref, cvec_ref, cs_ref, hr_ref, hi_ref, hn_ref,
                   hp_ref, hm_ref):
    blk = zp_ref.shape[0]
    absdl = jnp.abs(dl_ref[...])

    @pl.when(pl.program_id(1) == 0)
    def _():
        fw = hp_ref.shape[1]
        pre = jnp.concatenate([_dot_hi(zp_ref[...], w1_ref[...]), _dot_hi(zm_ref[...], w1_ref[...])], axis=1)
        h = jnp.sin(fr_ref[0:1, :] * (pre + b1_ref[...]))
        h = jnp.sin(fr_ref[1:2, :] * (_dot_hi(h, w2_ref[...]) + b2_ref[...]))
        hp_ref[...] = h[:, 0:fw]
        hm_ref[...] = h[:, fw:2 * fw]

    def taps(h_ref, t_ref, w3_ref):
        return _dot(h_ref[...].astype(BF16), w3_ref[...].astype(BF16)) * jnp.exp(-t_ref[...] * absdl)

    gp = taps(hp_ref, tp_ref, w3p_ref)
    gm = taps(hm_ref, tm_ref, w3m_ref) * keep_ref[...]
    gs = gp + gm
    gd = gp - gm
    cv = cvec_ref[...]
    hr_ref[...] = cv * _dot(cs_ref[:, 0:blk], gs.astype(BF16))
    hi_ref[...] = cv * _dot(cs_ref[:, blk:2 * blk], gd.astype(BF16))
    hn_ref[...] = jnp.sum(gs * sign_ref[...], axis=0, keepdims=True) * (0.5 / blk)


def hyena_block_spectra(seq_len, blk, d, tables, w1, b1, freq, w2, b2, w3):
    cs, sign, cvec = tables
    emb, fw = w1.shape
    bands = (emb - 1) // 2
    nblk = seq_len // blk
    ndl = 2 * nblk - 1
    dd = jnp.arange(ndl, dtype=jnp.int32)[:, None] - (nblk - 1)
    m = jnp.arange(blk, dtype=jnp.int32)[None, :]
    pos_p = jnp.where(dd >= 0, dd * blk + m, -dd * blk - m)
    pos_m = jnp.where(dd >= 1, dd * blk - m, -dd * blk + m)
    keep = jnp.where((m == 0) & (dd != 0), 0.0, 1.0).astype(F32)

    def features(pos):
        pos = pos.reshape(-1, 1).astype(F32)
        t = pos / (seq_len - 1)
        w = 2.0 * math.pi * pos / seq_len
        f = jnp.linspace(1e-4, bands - 1, bands, dtype=F32)[None, :]
        z = jnp.concatenate([t, jnp.cos(f * w), -jnp.sin(f * w)], axis=-1)
        return jnp.pad(z, ((0, 0), (0, LANE - emb))), t

    zp, tp = features(pos_p)
    zm, tm = features(pos_m)
    w1p = jnp.pad(w1, ((0, LANE - emb), (0, 0)))
    zeros = jnp.zeros_like(w2)
    w2_pair = jnp.concatenate([jnp.concatenate([w2, zeros], axis=1),
                               jnp.concatenate([zeros, w2], axis=1)], axis=0)
    deltas = jnp.linspace(math.log(HY_DECAY_TARGET) / HY_SLOW_PCT,
                          math.log(HY_DECAY_TARGET) / HY_FAST_PCT, d, dtype=F32)[None, :]
    fdc = 2 * DC
    nd = d // fdc
    full = lambda shape: pl.BlockSpec(shape, lambda i, j: (0,) * len(shape))
    rows = lambda width: pl.BlockSpec((blk, width), lambda i, j: (i, 0))
    w3_p = pl.BlockSpec((fw, fdc), lambda i, j: (0, jnp.where(i >= nblk - 1, 0, 2 * nd) + j))
    w3_m = pl.BlockSpec((fw, fdc), lambda i, j: (0, jnp.where(i >= nblk, 0, 2 * nd) + j))
    spec_out = pl.BlockSpec((blk, fdc), lambda i, j: (i, j))
    return pl.pallas_call(
        _filter_kernel,
        out_shape=(jax.ShapeDtypeStruct((ndl * blk, 2 * d), F32),
                   jax.ShapeDtypeStruct((ndl * blk, 2 * d), F32),
                   jax.ShapeDtypeStruct((ndl, 1, 2 * d), F32)),
        grid=(ndl, 2 * nd),
        in_specs=[rows(LANE), rows(LANE), rows(1), rows(1), rows(1),
                  full((LANE, fw)), full((1, 2 * fw)), full((2, 2 * fw)), full((2 * fw, 2 * fw)),
                  full((1, 2 * fw)),
                  w3_p, w3_m,
                  pl.BlockSpec((1, fdc), lambda i, j: (0, j % nd)),
                  full((blk, 1)), full((blk, 1)), full((blk, 2 * blk))],
        out_specs=(spec_out, spec_out, pl.BlockSpec((None, 1, fdc), lambda i, j: (i, 0, j))),
        scratch_shapes=[pltpu.VMEM((blk, fw), F32), pltpu.VMEM((blk, fw), F32)],
        compiler_params=_params(("parallel", "arbitrary")),
        name=f"hyena_filter_{seq_len}",
    )(zp, zm, tp, tm, keep.reshape(-1, 1), w1p, jnp.tile(b1.reshape(1, fw), (1, 2)), jnp.tile(freq, (1, 2)),
      w2_pair, jnp.tile(b2.reshape(1, fw), (1, 2)), w3, w3, deltas, sign, cvec, cs)


HY_GROUP = 2
HY_BLOCK = 512
HY_STRIP = 16
HY_HALO = 16


def _dwconv3_rows(p_ref, base, r, nrows, seq_len, w_ref, b_ref):
    lo, hi = max(r - HY_HALO, 0), min(r + nrows + HY_HALO, seq_len)
    win = p_ref[base + lo:base + hi, :].astype(F32)
    n = hi - lo
    prev = pltpu.roll(win, 1, 0)
    nxt = pltpu.roll(win, n - 1, 0)
    rows = lax.broadcasted_iota(jnp.int32, (n, 1), 0)
    if lo == 0:
        prev = jnp.where(rows == 0, 0.0, prev)
    if hi == seq_len:
        nxt = jnp.where(rows == n - 1, 0.0, nxt)
    y = w_ref[0:1, :] * prev + w_ref[1:2, :] * win + w_ref[2:3, :] * nxt + b_ref[...]
    return y[r - lo:r - lo + nrows, :]


def _hyena_mix_kernel(pv_ref, p1_ref, p2_ref, wv_ref, w1_ref, w2_ref, bv_ref, b1_ref, b2_ref,
                      cs_ref, hr1_ref, hi1_ref, hn1_ref, hr2_ref, hi2_ref, hn2_ref, skip_ref,
                      o_ref, z_ref, zb_ref, fa_ref, fb_ref, y_ref, *, seq_len, blk):
    dc = pv_ref.shape[1]
    groups = pv_ref.shape[0] // seq_len
    nblk = seq_len // blk
    tile = lambda a: jnp.concatenate([a] * groups, axis=1)
    rows = lax.broadcasted_iota(jnp.int32, (blk, 1), 0)
    sign = (1 - 2 * (rows % 2)).astype(F32)

    def stage(r, val):
        z_ref[r:r + blk, :] = val
        zb_ref[r:r + blk, :] = val.astype(BF16)
        return jnp.sum(val * sign, axis=0, keepdims=True)

    def block_products(i_out, hr_ref, hi_ref):
        def strip(r0):
            yr, yi = [None] * groups, [None] * groups
            for j in range(nblk):
                h0 = (i_out - j + nblk - 1) * blk
                hr = hr_ref[h0 + r0:h0 + r0 + HY_STRIP, :]
                hi = hi_ref[h0 + r0:h0 + r0 + HY_STRIP, :]
                for g in range(groups):
                    a = fa_ref[j * blk + r0:j * blk + r0 + HY_STRIP, g * dc:(g + 1) * dc]
                    b = fb_ref[j * blk + r0:j * blk + r0 + HY_STRIP, g * dc:(g + 1) * dc]
                    tr, ti = a * hr - b * hi, a * hi + b * hr
                    yr[g], yi[g] = (tr, ti) if yr[g] is None else (yr[g] + tr, yi[g] + ti)
            for g in range(groups):
                y_ref[r0:r0 + HY_STRIP, g * dc:(g + 1) * dc] = yr[g].astype(BF16)
                y_ref[blk + r0:blk + r0 + HY_STRIP, g * dc:(g + 1) * dc] = yi[g].astype(BF16)

        for r0 in range(0, blk, HY_STRIP):
            strip(r0)

    def transform(hr_ref, hi_ref, hn_ref, skip, mult_ref, mw_ref, mb_ref, nyqs, emit):
        for j in range(nblk):
            r = j * blk
            fa_ref[r:r + blk, :] = _dot(cs_ref[:, 0:blk], zb_ref[r:r + blk, :])
            fb_ref[r:r + blk, :] = _dot(cs_ref[:, blk:2 * blk], zb_ref[r:r + blk, :])
        skip_t = tile(skip)
        out = []
        for i in range(nblk):
            r = i * blk
            block_products(i, hr_ref, hi_ref)
            nyq = None
            for j in range(nblk):
                term = nyqs[j] * tile(hn_ref[i - j + nblk - 1])
                nyq = term if nyq is None else nyq + term
            y = _dot(cs_ref[...], y_ref[...]) + sign * nyq + z_ref[r:r + blk, :] * skip_t
            mult = jnp.concatenate([_dwconv3_rows(mult_ref, g * seq_len, r, blk, seq_len, mw_ref, mb_ref)
                                    for g in range(groups)], axis=1)
            out.append(emit(r, mult * y))
        return out

    nyqs = []
    for j in range(nblk):
        v = jnp.concatenate([_dwconv3_rows(pv_ref, g * seq_len, j * blk, blk, seq_len, wv_ref, bv_ref)
                             for g in range(groups)], axis=1)
        nyqs.append(stage(j * blk, v))

    nyqs = transform(hr1_ref, hi1_ref, hn1_ref, skip_ref[0:1, :], p1_ref, w1_ref, b1_ref, nyqs, stage)

    def write_out(r, val):
        for g in range(groups):
            o_ref[g * seq_len + r:g * seq_len + r + blk, :] = val[:, g * dc:(g + 1) * dc].astype(BF16)

    transform(hr2_ref, hi2_ref, hn2_ref, skip_ref[1:2, :], p2_ref, w2_ref, b2_ref, nyqs, write_out)


def hyena_mix(geo, p, conv_w, conv_b, skip, tables, spectrum, seq_len, row0, nb):
    d = geo.d
    cs = tables[0]
    hr, hi, hn = spectrum
    blk = cs.shape[0]
    ndl = 2 * (seq_len // blk) - 1
    nd = d // DC
    group = max(HY_GROUP, 2048 // seq_len)
    rows = group * seq_len
    width = group * DC
    assert nb % group == 0 and row0 % rows == 0 and seq_len % blk == 0 and blk % HY_STRIP == 0
    blk0 = row0 // rows
    pspec = lambda part: pl.BlockSpec((rows, DC), lambda c, b: (blk0 + b, part * nd + c))
    wspec = lambda part: pl.BlockSpec((3, DC), lambda c, b: (0, part * nd + c))
    bspec = lambda part: pl.BlockSpec((1, DC), lambda c, b: (0, part * nd + c))
    hspec = lambda order: pl.BlockSpec((ndl * blk, DC), lambda c, b: (0, order * nd + c),
                                       pipeline_mode=pl.Buffered(1))
    nspec = lambda order: pl.BlockSpec((ndl, 1, DC), lambda c, b: (0, 0, order * nd + c))
    return pl.pallas_call(
        functools.partial(_hyena_mix_kernel, seq_len=seq_len, blk=blk),
        out_shape=jax.ShapeDtypeStruct((nb * seq_len, d), BF16),
        grid=(nd, nb // group),
        in_specs=[pspec(0), pspec(1), pspec(2), wspec(0), wspec(1), wspec(2), bspec(0), bspec(1), bspec(2),
                  _resident((blk, 2 * blk)),
                  hspec(0), hspec(0), nspec(0), hspec(1), hspec(1), nspec(1),
                  pl.BlockSpec((2, DC), lambda c, b: (0, c))],
        out_specs=pl.BlockSpec((rows, DC), lambda c, b: (b, c)),
        scratch_shapes=[pltpu.VMEM((seq_len, width), F32), pltpu.VMEM((seq_len, width), BF16),
                        pltpu.VMEM((seq_len, width), F32), pltpu.VMEM((seq_len, width), F32),
                        pltpu.VMEM((2 * blk, width), BF16)],
        compiler_params=_params(("arbitrary", "arbitrary")),
        name=f"hyena_mix_{seq_len}",
    )(p, p, p, conv_w, conv_w, conv_w, conv_b, conv_b, conv_b, cs,
      hr, hi, hn, hr, hi, hn, skip)


CONV_ROWS = 64
CONV_PAD = 16


def _stage_padded(src, pad_ref):
    n = src.shape[0]
    zeros = jnp.zeros((CONV_PAD, pad_ref.shape[1]), F32)
    pad_ref[0:CONV_PAD, :] = zeros
    pad_ref[CONV_PAD + n:2 * CONV_PAD + n, :] = zeros
    pad_ref[CONV_PAD:CONV_PAD + n, :] = src


def _dwconv_chunks(pad_ref, w_ref, n, ktaps, emit):
    half = ktaps // 2
    lo = (CONV_PAD - half) // SUBLANES * SUBLANES
    hi = -(-(CONV_PAD + half + CONV_ROWS) // SUBLANES) * SUBLANES
    rows = hi - lo

    def chunk(ci, carry):
        r0 = pl.multiple_of(ci * CONV_ROWS, CONV_ROWS)
        window = pad_ref[pl.ds(pl.multiple_of(r0 + lo, SUBLANES), rows), :]
        acc = None
        for k in range(ktaps):
            off = CONV_PAD - half + k - lo
            win = window if off == 0 else pltpu.roll(window, rows - off, 0)
            term = w_ref[k:k + 1, :] * win[0:CONV_ROWS, :]
            acc = term if acc is None else acc + term
        emit(r0, acc)
        return carry

    lax.fori_loop(0, n // CONV_ROWS, chunk, 0)


def _conv_group(seq_len, nb):
    group = max(1, min(nb, 1024 // seq_len))
    assert nb % group == 0
    return group


def _cf_conv_kernel(u_ref, w_ref, b_ref, o_ref, pad_ref, *, seq_len):
    bias = b_ref[...]
    for base in range(0, u_ref.shape[0], seq_len):
        _stage_padded(u_ref[base:base + seq_len, :].astype(F32), pad_ref)

        def emit(r0, acc, base=base):
            o_ref[pl.ds(base + r0, CONV_ROWS), :] = (acc + bias).astype(BF16)

        _dwconv_chunks(pad_ref, w_ref, seq_len, w_ref.shape[0], emit)


def conformer_dwconv(geo, u, w, b, seq_len, row0, nb):
    d = geo.d
    dc = 256
    ktaps = w.shape[0]
    rows = _conv_group(seq_len, nb) * seq_len
    blk0 = row0 // rows
    return pl.pallas_call(
        functools.partial(_cf_conv_kernel, seq_len=seq_len),
        out_shape=jax.ShapeDtypeStruct((nb * seq_len, d), BF16),
        grid=(nb * seq_len // rows, d // dc),
        in_specs=[pl.BlockSpec((rows, dc), lambda b_, c: (blk0 + b_, c)),
                  pl.BlockSpec((ktaps, dc), lambda b_, c: (0, c)),
                  pl.BlockSpec((1, dc), lambda b_, c: (0, c))],
        out_specs=pl.BlockSpec((rows, dc), lambda b_, c: (b_, c)),
        scratch_shapes=[pltpu.VMEM((seq_len + 2 * CONV_PAD, dc), F32)],
        compiler_params=_params(("parallel", "parallel")),
        name=f"conformer_dwconv_{seq_len}",
    )(u, w, b.reshape(1, d))


def _sc_conv_kernel(q_ref, bg_ref, w_ref, o_ref, pad_ref, *, seq_len):
    for base in range(0, q_ref.shape[0], seq_len):
        _stage_padded(q_ref[base:base + seq_len, :].astype(F32), pad_ref)

        def emit(r0, acc, base=base):
            gate = bg_ref[pl.ds(base + r0, CONV_ROWS), :].astype(F32)
            o_ref[pl.ds(base + r0, CONV_ROWS), :] = (gate * acc).astype(BF16)

        _dwconv_chunks(pad_ref, w_ref, seq_len, w_ref.shape[0], emit)


def shortconv_gate(geo, q, bg, w, seq_len, row0, nb):
    d = geo.d
    dc = 256
    ktaps = w.shape[0]
    rows = _conv_group(seq_len, nb) * seq_len
    blk0 = row0 // rows
    seq = pl.BlockSpec((rows, dc), lambda b_, c: (blk0 + b_, c))
    return pl.pallas_call(
        functools.partial(_sc_conv_kernel, seq_len=seq_len),
        out_shape=jax.ShapeDtypeStruct((nb * seq_len, d), BF16),
        grid=(nb * seq_len // rows, d // dc),
        in_specs=[seq, seq, pl.BlockSpec((ktaps, dc), lambda b_, c: (0, c))],
        out_specs=pl.BlockSpec((rows, dc), lambda b_, c: (b_, c)),
        scratch_shapes=[pltpu.VMEM((seq_len + 2 * CONV_PAD, dc), F32)],
        compiler_params=_params(("parallel", "parallel")),
        name=f"shortconv_gate_{seq_len}",
    )(q, bg, w)


def _mla_proj_kernel(x, mod_ref, gn_ref, wdq_ref, gq_ref, wun_ref, wup_ref, wus_ref,
                     wkc_ref, wkp_ref, wks_ref, gkv_ref, cosq_ref, sinq_ref,
                     q_ref, ckv_ref, kpe_ref, *, heads, nope, rope, qscale):
    h = _prenorm(x, mod_ref, gn_ref, 0)
    cos2, sin2 = cosq_ref[...], sinq_ref[...]
    cq_raw = _dot(h, wdq_ref[...])
    ckv_raw = _dot(h, wkc_ref[...])
    kpe = _dot(h, wkp_ref[...]) * cos2[:, 0:rope] + _dot(h, wks_ref[...]) * sin2[:, 0:rope]
    cq = (_rms(cq_raw) * gq_ref[...]).astype(BF16)
    qn = _dot(cq, wun_ref[...]) * qscale
    cosq, sinq = cos2 * qscale, sin2 * qscale
    for pair in range(heads // 2):
        lo = pair * 2 * rope
        qp = (_dot(cq, wup_ref[:, lo:lo + 2 * rope]) * cosq
              + _dot(cq, wus_ref[:, lo:lo + 2 * rope]) * sinq)
        for j in range(2):
            hd = 2 * pair + j
            q_ref[hd, :, 0:nope] = qn[:, hd * nope:(hd + 1) * nope].astype(BF16)
            q_ref[hd, :, nope:nope + rope] = qp[:, j * rope:(j + 1) * rope].astype(BF16)
    ckv_ref[...] = _rms(ckv_raw) * gkv_ref[...]
    kpe_ref[...] = kpe


def mla_project(geo, x, mod, gn, wdq, gq, wun, wup, wus, wkc, wkp, wks, gkv, cos2, sin2, heads, nope, rope,
                qscale):
    d = geo.d
    qr, kvr = wdq.shape[1], wkc.shape[1]
    row = lambda w: pl.BlockSpec((TM, w), lambda i: (i, 0))
    rot = pl.BlockSpec((TM, 2 * rope), lambda i: (_rope_block_of_tile(geo, i), 0))
    body = functools.partial(_mla_proj_kernel, heads=heads, nope=nope, rope=rope, qscale=qscale)
    return pl.pallas_call(
        _row_kernel(body, geo, x),
        out_shape=(jax.ShapeDtypeStruct((heads, geo.n, nope + rope), BF16),
                   jax.ShapeDtypeStruct((geo.n, kvr), F32),
                   jax.ShapeDtypeStruct((geo.n, rope), F32)),
        grid=(geo.n // TM,),
        in_specs=_row_specs(geo, x) + [_resident((d, qr)), _resident((1, qr)),
                                    _resident((qr, heads * nope)), _resident((qr, heads * rope)),
                                    _resident((qr, heads * rope)),
                                    _resident((d, kvr)), _resident((d, rope)), _resident((d, rope)),
                                    _resident((1, kvr)), rot, rot],
        out_specs=(pl.BlockSpec((heads, TM, nope + rope), lambda i: (0, i, 0)), row(kvr), row(rope)),
        compiler_params=_params(("parallel",)),
        name="mla_project",
    )(*_stream_args(x), mod, gn, wdq, gq.reshape(1, qr), wun, wup, wus, wkc, wkp, wks, gkv.reshape(1, kvr),
      cos2, sin2)


KV_ROWS = 256


def _mla_kv_kernel(*refs, heads, nope, rope, past_tiles, sub):
    wn_ref, wv_ref, k_ref, v_ref = refs[-4:]
    new = refs[-4 - 2 * sub:-4]
    for u in range(sub):
        rows = slice(u * KV_ROWS, (u + 1) * KV_ROWS)
        ckv, kpe = new[2 * u][...], new[2 * u + 1][...]
        if past_tiles:
            cached = pl.program_id(1) * sub + u < past_tiles
            ckv = jnp.where(cached, refs[0][...], ckv)
            kpe = jnp.where(cached, refs[1][...], kpe)
        c = ckv.astype(BF16)
        kn = _dot(c, wn_ref[...])
        v_ref[rows, :] = _dot(c, wv_ref[...]).astype(BF16)
        kp = kpe.astype(BF16)
        for hd in range(heads):
            k_ref[hd, rows, 0:nope] = kn[:, hd * nope:(hd + 1) * nope].astype(BF16)
            k_ref[hd, rows, nope:nope + rope] = kp


def mla_expand_kv(ckv, kpe, cache, layer, seq_len, row0, nb, wn, wv, heads, nope, rope, vdim):
    r = ckv.shape[1]
    past = 0 if cache is None else cache[0].shape[2]
    assert past % KV_ROWS == 0 and seq_len % KV_ROWS == 0 and row0 % KV_ROWS == 0
    pt, st, blk0 = past // KV_ROWS, seq_len // KV_ROWS, row0 // KV_ROWS
    lk = past + seq_len
    tiles = lk // KV_ROWS
    sub = 3 if (tiles % 3 == 0 and pt <= 1) else 1
    specs, args = [], []
    for u in range(sub):
        new_map = lambda b, t, u=u: (blk0 + b * st + jnp.maximum(t * sub + u - pt, 0), 0)
        specs += [pl.BlockSpec((KV_ROWS, r), new_map), pl.BlockSpec((KV_ROWS, rope), new_map)]
        args += [ckv, kpe]
    specs += [_resident((r, heads * nope)), _resident((r, heads * vdim))]
    args += [wn, wv]
    if cache is not None:
        old_map = lambda b, t: (b, layer, jnp.minimum(t * sub, pt - 1), 0)
        specs = [pl.BlockSpec((None, None, KV_ROWS, r), old_map),
                 pl.BlockSpec((None, None, KV_ROWS, rope), old_map)] + specs
        args = list(cache) + args
    return pl.pallas_call(
        functools.partial(_mla_kv_kernel, heads=heads, nope=nope, rope=rope, past_tiles=pt, sub=sub),
        out_shape=(jax.ShapeDtypeStruct((nb, heads, lk, nope + rope), BF16),
                   jax.ShapeDtypeStruct((nb, lk, heads * vdim), BF16)),
        grid=(nb, tiles // sub),
        in_specs=specs,
        out_specs=(pl.BlockSpec((None, heads, sub * KV_ROWS, nope + rope), lambda b, t: (b, 0, t, 0)),
                   pl.BlockSpec((None, sub * KV_ROWS, heads * vdim), lambda b, t: (b, t, 0))),
        compiler_params=_params(("parallel", "arbitrary")),
        name=f"mla_expand_kv_{lk}",
    )(*args)


ATTN_ROWS = 256


def _attn_kernel(q_ref, k_ref, v_ref, o_ref, *, vdim):
    hg, tq, _ = q_ref.shape
    chains = [(h, r) for h in range(hg) for r in range(0, tq, ATTN_ROWS)]

    def scores(h, r):
        return lax.dot_general(q_ref[h, r:r + ATTN_ROWS, :], k_ref[h], (((1,), (1,)), ((), ())),
                               preferred_element_type=F32)

    s_next = scores(*chains[0])
    for i, (h, r) in enumerate(chains):
        s = s_next
        if i + 1 < len(chains):
            s_next = scores(*chains[i + 1])
        m = jnp.max(s, axis=-1, keepdims=True)
        p = jnp.exp2(s - m)
        l = jnp.sum(p, axis=-1, keepdims=True)
        o = _dot(p.astype(BF16), v_ref[:, h * vdim:(h + 1) * vdim])
        o_ref[r:r + ATTN_ROWS, h * vdim:(h + 1) * vdim] = (o * (1.0 / l)).astype(BF16)


def mla_attention(q, k, v, seq_len, row0, nb, vdim):
    heads, _, dq = q.shape
    lk = k.shape[2]
    hg = max(1, min(heads, 2048 // seq_len))
    blk0 = row0 // seq_len
    return pl.pallas_call(
        functools.partial(_attn_kernel, vdim=vdim),
        out_shape=jax.ShapeDtypeStruct((nb * seq_len, heads * vdim), BF16),
        grid=(nb, heads // hg),
        in_specs=[pl.BlockSpec((hg, seq_len, dq), lambda b, h: (h, blk0 + b, 0)),
                  pl.BlockSpec((None, hg, lk, dq), lambda b, h: (b, h, 0, 0)),
                  pl.BlockSpec((None, lk, hg * vdim), lambda b, h: (b, 0, h))],
        out_specs=pl.BlockSpec((seq_len, hg * vdim), lambda b, h: (b, h)),
        compiler_params=_params(("parallel", "parallel")),
        name=f"mla_attention_{seq_len}",
    )(q, k, v)


def rope_tables(geo, rope):
    nf = rope // 4
    rows = geo.l_lat // GRID_W
    row = jnp.repeat(jnp.arange(rows, dtype=F32), GRID_W)
    col = jnp.tile(jnp.arange(GRID_W, dtype=F32), rows)
    inv = jnp.exp(-math.log(ROPE_THETA) * jnp.arange(nf, dtype=F32) * (4.0 / rope))
    ar, ac = row[:, None] * inv, col[:, None] * inv
    cos = jnp.concatenate([jnp.cos(ar), jnp.cos(ar), jnp.cos(ac), jnp.cos(ac)], axis=-1)
    sin = jnp.concatenate([-jnp.sin(ar), jnp.sin(ar), -jnp.sin(ac), jnp.sin(ac)], axis=-1)
    cos = jnp.concatenate([jnp.ones((TM, rope), F32), cos], axis=0)
    sin = jnp.concatenate([jnp.zeros((TM, rope), F32), sin], axis=0)
    return jnp.tile(cos, (1, 2)), jnp.tile(sin, (1, 2))


def _rope_block_of_tile(geo, i):
    nct, per_seq = geo.n_ctx // TM, geo.l_lat // TM
    return jnp.where(i < nct, 0, 1 + (i - nct) % per_seq)


def _swap_rope_halves(w, rope):
    nf = rope // 4
    shape = w.shape
    w = w.reshape(shape[:-1] + (shape[-1] // (2 * nf), 2, nf))
    return w[..., ::-1, :].reshape(shape)


def kernel(x_prompt, x_sample, c, cache_ckv, cache_kpe, c_ctx, w_mod, b_mod, norm_g, ffn_w_gate, ffn_w_up, ffn_w_down, hy_w_in, hy_b_in, hy_conv_w, hy_conv_b, hy_f_w1, hy_f_b1, hy_f_freq, hy_f_w2, hy_f_b2, hy_f_w3, hy_skip, hy_w_out, hy_b_out, cf_w_pw1, cf_b_pw1, cf_dw_w, cf_dw_b, cf_ln_g, cf_ln_b, cf_w_pw2, cf_b_pw2, sc_w_in, sc_conv_w, sc_w_out, mla_w_dq, mla_g_q, mla_w_uq, mla_w_dkv, mla_g_kv, mla_w_ukv, mla_w_o):
    nb_ctx, l_ctx, d = x_prompt.shape
    nb_lat, l_lat, _ = x_sample.shape
    geo = Geometry(nb_ctx, l_ctx, nb_lat, l_lat, d)
    depth = w_mod.shape[0]
    n_mixers = 4

    x = (x_prompt.reshape(geo.n_ctx, d), x_sample.reshape(geo.n_lat, d))
    cond = jnp.concatenate([c_ctx[None, :], c], axis=0)
    n_cond = cond.shape[0]
    cond = jnp.pad(cond, ((0, -n_cond % 8), (0, 0)))
    mod_all = modulation_all(cond, w_mod, b_mod)[:, :n_cond].reshape(depth, n_cond, 6, d)

    passes = ((l_ctx, 0, nb_ctx), (l_lat, geo.n_ctx, nb_lat))
    zero_bias = jnp.zeros((d,), F32)
    ffn_stacks = (ffn_w_gate, ffn_w_up, ffn_w_down)
    ffn_bf16 = None
    new_ckv, new_kpe = [], []

    for i in range(depth):
        m, j = i % n_mixers, i // n_mixers
        mod, gn = mod_all[i], norm_g[i]
        if m == 0:
            if ffn_bf16 is None:
                p, ffn_bf16 = hyena_in_proj(geo, x, mod, gn, hy_w_in[j].astype(BF16), hy_b_in[j], ffn_stacks, i)
            else:
                p, _ = hyena_in_proj(geo, x, mod, gn, hy_w_in[j].astype(BF16), hy_b_in[j])
            parts = []
            for seq_len, row0, nb in passes:
                blk = min(seq_len, HY_BLOCK)
                tables = dft_tables(blk)
                spectrum = hyena_block_spectra(seq_len, blk, d, tables, hy_f_w1[j], hy_f_b1[j], hy_f_freq[j],
                                               hy_f_w2[j], hy_f_b2[j], hy_f_w3[j])
                parts.append(hyena_mix(geo, p, hy_conv_w[j], hy_conv_b[j].reshape(1, -1), hy_skip[j],
                                       tables, spectrum, seq_len, row0, nb))
            a, w_out, b_out, ln = tuple(parts), hy_w_out[j], hy_b_out[j], None
        elif m == 1:
            u = conformer_in_proj(geo, x, mod, gn, cf_w_pw1[j].astype(BF16), cf_b_pw1[j])
            a = tuple(conformer_dwconv(geo, u, cf_dw_w[j], cf_dw_b[j], seq_len, row0, nb)
                      for seq_len, row0, nb in passes)
            w_out, b_out, ln = cf_w_pw2[j], cf_b_pw2[j], (cf_ln_g[j], cf_ln_b[j])
        elif m == 2:
            bg, q = shortconv_in_proj(geo, x, mod, gn, sc_w_in[j].astype(BF16))
            a = tuple(shortconv_gate(geo, q, bg, sc_conv_w[j], seq_len, row0, nb)
                      for seq_len, row0, nb in passes)
            w_out, b_out, ln = sc_w_out[j], zero_bias, None
        else:
            kvr = mla_g_kv.shape[1]
            rope = mla_w_dkv.shape[2] - kvr
            vdim = 128
            heads = mla_w_o.shape[1] // vdim
            nope = mla_w_uq.shape[2] // heads - rope
            wuq = mla_w_uq[j].reshape(-1, heads, nope + rope)
            wun = wuq[:, :, :nope].reshape(-1, heads * nope).astype(BF16)
            wup = wuq[:, :, nope:].reshape(-1, heads * rope)
            wukv = mla_w_ukv[j].reshape(kvr, heads, nope + vdim)
            wkn = wukv[:, :, :nope].reshape(kvr, heads * nope).astype(BF16)
            wkv = wukv[:, :, nope:].reshape(kvr, heads * vdim).astype(BF16)
            wkc, wkp = mla_w_dkv[j][:, :kvr], mla_w_dkv[j][:, kvr:]
            cos2, sin2 = rope_tables(geo, rope)
            q, ckv, kpe = mla_project(geo, x, mod, gn, mla_w_dq[j].astype(BF16), mla_g_q[j], wun,
                                      wup.astype(BF16), _swap_rope_halves(wup, rope).astype(BF16),
                                      wkc.astype(BF16), wkp.astype(BF16),
                                      _swap_rope_halves(wkp, rope).astype(BF16), mla_g_kv[j],
                                      cos2, sin2, heads, nope, rope,
                                      (nope + rope) ** -0.5 * math.log2(math.e))
            new_ckv.append(ckv[:geo.n_ctx].reshape(nb_ctx, l_ctx, kvr))
            new_kpe.append(kpe[:geo.n_ctx].reshape(nb_ctx, l_ctx, rope))
            parts = []
            for (seq_len, row0, nb), cache in zip(passes, (None, (cache_ckv, cache_kpe))):
                kk, vv = mla_expand_kv(ckv, kpe, cache, j, seq_len, row0, nb, wkn, wkv, heads, nope, rope, vdim)
                parts.append(mla_attention(q, kk, vv, seq_len, row0, nb, vdim))
            a, w_out, b_out, ln = tuple(parts), mla_w_o[j], zero_bias, None
        if ffn_bf16 is None:
            ffn_bf16 = [w[i].astype(BF16) for w in ffn_stacks]
        last = i == depth - 1
        x, ffn_bf16 = layer_tail(geo, a, x, mod, gn, w_out.astype(BF16), b_out, ln, *ffn_bf16, split=last,
                                 cast_stacks=() if last else ffn_stacks, cast_layer=i + 1)
        ffn_bf16 = ffn_bf16 or None

    y_prompt = x[0].reshape(nb_ctx, l_ctx, d)
    y_sample = x[1].reshape(nb_lat, l_lat, d)
    return (y_prompt, y_sample, jnp.stack(new_ckv, axis=1), jnp.stack(new_kpe, axis=1))
```

```python
import functools
import math

import jax
import jax.numpy as jnp
from jax import lax
from jax.experimental import pallas as pl
from jax.experimental.pallas import tpu as pltpu

F32 = jnp.float32
BF16 = jnp.bfloat16

NORM_EPS = 1e-6
GRID_W = 64
ROPE_THETA = 10000.0
HY_DECAY_TARGET = 1e-2
HY_FAST_PCT = 0.3
HY_SLOW_PCT = 1.5

V7X_VMEM_LIMIT_BYTES = 60000 * 1024
LANE = 128
SUBLANES = 8

TM = 512
DC = 256


def _dot(a, b):
    return jnp.dot(a, b, preferred_element_type=F32)


def _dot_hi(a, b):
    return jnp.dot(a, b, preferred_element_type=F32, precision=lax.Precision.HIGHEST)


def _rms(x):
    return x * lax.rsqrt(jnp.mean(x * x, axis=-1, keepdims=True) + NORM_EPS)


def _sigmoid(x):
    return 1.0 / (1.0 + jnp.exp(-x))


def _silu(x):
    return x * _sigmoid(x)


def _params(semantics):
    return pltpu.CompilerParams(dimension_semantics=semantics,
                                vmem_limit_bytes=V7X_VMEM_LIMIT_BYTES)


def _resident(shape):
    zeros = (0,) * len(shape)
    return pl.BlockSpec(shape, lambda *_: zeros, pipeline_mode=pl.Buffered(1))


class Geometry:
    def __init__(self, nb_ctx, l_ctx, nb_lat, l_lat, d):
        self.nb_ctx, self.l_ctx, self.nb_lat, self.l_lat, self.d = nb_ctx, l_ctx, nb_lat, l_lat, d
        self.n_ctx = nb_ctx * l_ctx
        self.n_lat = nb_lat * l_lat
        self.n = self.n_ctx + self.n_lat
        assert self.n_ctx % TM == 0 and l_lat % TM == 0 and self.n_ctx % l_lat == 0

    def cond_of_tile(self, i):
        row = i * TM
        return jnp.where(row < self.n_ctx, 0, 1 + (row - self.n_ctx) // self.l_lat)


def _mod_kernel(c_ref, w_ref, b_ref, o_ref):
    a = _silu(c_ref[...]).astype(BF16)
    o_ref[...] = _dot(a, w_ref[...].astype(BF16)) + b_ref[...]


def modulation_all(cond, w_mod, b_mod):
    depth, d, d6 = w_mod.shape
    r = cond.shape[0]
    tn = 1536
    return pl.pallas_call(
        _mod_kernel,
        out_shape=jax.ShapeDtypeStruct((depth, r, d6), F32),
        grid=(depth, d6 // tn),
        in_specs=[pl.BlockSpec((r, d), lambda l, j: (0, 0)),
                  pl.BlockSpec((None, d, tn), lambda l, j: (l, 0, j)),
                  pl.BlockSpec((None, 1, tn), lambda l, j: (l, 0, j))],
        out_specs=pl.BlockSpec((None, r, tn), lambda l, j: (l, 0, j)),
        compiler_params=_params(("parallel", "parallel")),
        name="modulation",
    )(cond, w_mod, b_mod.reshape(depth, 1, d6))


def _stream_args(op):
    return list(op) if isinstance(op, tuple) else [op]


def _stream_specs(geo, op, width):
    if not isinstance(op, tuple):
        return [pl.BlockSpec((TM, width), lambda i: (i, 0))]
    nct = geo.n_ctx // TM
    return [pl.BlockSpec((TM, width), lambda i: (jnp.minimum(i, nct - 1), 0)),
            pl.BlockSpec((TM, width), lambda i: (jnp.maximum(i - nct, 0), 0))]


def _stream_tile(refs, nct):
    if len(refs) == 1:
        return refs[0][...]
    return jnp.where(pl.program_id(0) < nct, refs[0][...], refs[1][...])


def _prenorm(x, mod_ref, gn_ref, which):
    g = gn_ref[2 * which:2 * which + 1, :]
    sh = mod_ref[3 * which:3 * which + 1, :]
    sc = mod_ref[3 * which + 1:3 * which + 2, :]
    return ((_rms(x) * g) * (1.0 + sc) + sh).astype(BF16)


def _row_specs(geo, x):
    d = geo.d
    return _stream_specs(geo, x, d) + [
        pl.BlockSpec((None, 6, d), lambda i: (geo.cond_of_tile(i), 0, 0)),
        pl.BlockSpec((4, d), lambda i: (0, 0))]


def _row_kernel(body, geo, x):
    nx, nct = len(_stream_args(x)), geo.n_ctx // TM

    def kern(*refs):
        body(_stream_tile(refs[:nx], nct), *refs[nx:])

    return kern


def _side_cast_plan(picks, steps):
    in_specs, out_specs, shapes = [], [], []
    for w, layer in picks:
        _, r, c = w.shape
        s = max(k for k in range(1, steps + 1) if r % k == 0 and (r // k) % 16 == 0)
        in_specs.append(pl.BlockSpec((None, r // s, c),
                                     lambda i, s=s, layer=layer: (layer, jnp.minimum(i, s - 1), 0)))
        out_specs.append(pl.BlockSpec((r // s, c), lambda i, s=s: (jnp.minimum(i, s - 1), 0)))
        shapes.append(jax.ShapeDtypeStruct((r, c), BF16))
    return in_specs, out_specs, shapes


def _side_cast(in_refs, out_refs):
    for i_ref, o_ref in zip(in_refs, out_refs):
        o_ref[...] = i_ref[...].astype(BF16)


def _hy_proj_kernel(x, mod_ref, gn_ref, w_ref, b_ref, *rest):
    n_cast = (len(rest) - 1) // 2
    o_ref = rest[n_cast]
    _side_cast(rest[:n_cast], rest[n_cast + 1:])
    h = _prenorm(x, mod_ref, gn_ref, 0)
    step = 512
    for c in range(0, o_ref.shape[1], step):
        o_ref[:, c:c + step] = (_dot(h, w_ref[:, c:c + step]) + b_ref[:, c:c + step]).astype(BF16)


def hyena_in_proj(geo, x, mod, gn, w_in, b_in, casts=()):
    d, d3 = w_in.shape
    steps = geo.n // TM
    c_in, c_out, c_shapes = _side_cast_plan(casts, steps)
    out = pl.pallas_call(
        _row_kernel(_hy_proj_kernel, geo, x),
        out_shape=[jax.ShapeDtypeStruct((geo.n, d3), BF16)] + c_shapes,
        grid=(steps,),
        in_specs=_row_specs(geo, x) + [_resident((d, d3)), _resident((1, d3))] + c_in,
        out_specs=[pl.BlockSpec((TM, d3), lambda i: (i, 0))] + c_out,
        compiler_params=_params(("arbitrary",)),
        name="hyena_in_proj",
    )(*_stream_args(x), mod, gn, w_in, b_in.reshape(1, d3), *[w for w, _ in casts])
    return out[0], list(out[1:])


def _cf_proj_kernel(x, mod_ref, gn_ref, w_ref, b_ref, o_ref):
    h = _prenorm(x, mod_ref, gn_ref, 0)
    d = o_ref.shape[1]
    step = 512
    for c in range(0, d, step):
        a = _dot(h, w_ref[:, c:c + step]) + b_ref[:, c:c + step]
        g = _dot(h, w_ref[:, d + c:d + c + step]) + b_ref[:, d + c:d + c + step]
        o_ref[:, c:c + step] = (a * _sigmoid(g)).astype(BF16)


def conformer_in_proj(geo, x, mod, gn, w_pw1, b_pw1):
    d, d2 = w_pw1.shape
    return pl.pallas_call(
        _row_kernel(_cf_proj_kernel, geo, x),
        out_shape=jax.ShapeDtypeStruct((geo.n, d), BF16),
        grid=(geo.n // TM,),
        in_specs=_row_specs(geo, x) + [_resident((d, d2)), _resident((1, d2))],
        out_specs=pl.BlockSpec((TM, d), lambda i: (i, 0)),
        compiler_params=_params(("parallel",)),
        name="conformer_in_proj",
    )(*_stream_args(x), mod, gn, w_pw1, b_pw1.reshape(1, d2))


def _sc_proj_kernel(x, mod_ref, gn_ref, w_ref, bg_ref, q_ref):
    h = _prenorm(x, mod_ref, gn_ref, 0)
    d = bg_ref.shape[1]
    step = 512
    for c in range(0, d, step):
        bg_ref[:, c:c + step] = _dot(h, w_ref[:, c:c + step]).astype(BF16)
        cg = _dot(h, w_ref[:, d + c:d + c + step])
        hv = _dot(h, w_ref[:, 2 * d + c:2 * d + c + step])
        q_ref[:, c:c + step] = (cg * hv).astype(BF16)


def shortconv_in_proj(geo, x, mod, gn, w_in):
    d, d3 = w_in.shape
    out = jax.ShapeDtypeStruct((geo.n, d), BF16)
    return pl.pallas_call(
        _row_kernel(_sc_proj_kernel, geo, x),
        out_shape=(out, out),
        grid=(geo.n // TM,),
        in_specs=_row_specs(geo, x) + [_resident((d, d3))],
        out_specs=(pl.BlockSpec((TM, d), lambda i: (i, 0)), pl.BlockSpec((TM, d), lambda i: (i, 0))),
        compiler_params=_params(("parallel",)),
        name="shortconv_in_proj",
    )(*_stream_args(x), mod, gn, w_in)


TAIL_CHAINS = 2


def _layer_tail_kernel(a, x, mod_ref, gn_ref, wo_ref, bo_ref, lg_ref, lb_ref, wg_ref, wu_ref, wd_ref,
                       *rest, nct, layer_norm, n_cast):
    acts = rest[-TAIL_CHAINS:]
    outs = rest[n_cast:len(rest) - TAIL_CHAINS - n_cast]
    rows = a.shape[0] // TAIL_CHAINS
    groups = [slice(i * rows, (i + 1) * rows) for i in range(TAIL_CHAINS)]
    if layer_norm:
        u = a.astype(F32)
        mu = jnp.mean(u, axis=-1, keepdims=True)
        uc = u - mu
        var = jnp.mean(uc * uc, axis=-1, keepdims=True)
        a = _silu((uc * lax.rsqrt(var + NORM_EPS)) * lg_ref[...] + lb_ref[...]).astype(BF16)
    ys = [_dot(a[g, :], wo_ref[...]) + bo_ref[...] for g in groups]
    xs = [x[g, :] + mod_ref[2:3, :] * (_rms(y) * gn_ref[1:2, :]) for g, y in zip(groups, ys)]
    hs = [_prenorm(xg, mod_ref, gn_ref, 1) for xg in xs]
    step = 256
    for c in range(0, acts[0].shape[1], step):
        for h, act_ref in zip(hs, acts):
            gate = _dot(h, wg_ref[:, c:c + step])
            up = _dot(h, wu_ref[:, c:c + step])
            act_ref[:, c:c + step] = (_silu(gate) * up).astype(BF16)
    ys = [_dot(act_ref[...], wd_ref[...]) for act_ref in acts]
    _side_cast(rest[:n_cast], rest[len(rest) - TAIL_CHAINS - n_cast:len(rest) - TAIL_CHAINS])
    res = [xg + mod_ref[5:6, :] * (_rms(y) * gn_ref[3:4, :]) for xg, y in zip(xs, ys)]

    def write(o_ref):
        for g, val in zip(groups, res):
            o_ref[g, :] = val

    if len(outs) == 1:
        write(outs[0])
    else:
        i = pl.program_id(0)
        pl.when(i < nct)(lambda: write(outs[0]))
        pl.when(i >= nct)(lambda: write(outs[1]))


def layer_tail(geo, a, x, mod, gn, w_out, b_out, ln, wg, wu, wd, split=False, casts=()):
    d, dff = wg.shape
    steps = geo.n // TM
    na, nx, nct = len(_stream_args(a)), len(_stream_args(x)), geo.n_ctx // TM
    c_in, c_out, c_shapes = _side_cast_plan(casts, steps)
    body = functools.partial(_layer_tail_kernel, nct=nct, layer_norm=ln is not None, n_cast=len(c_in))

    def kern(*refs):
        body(_stream_tile(refs[:na], nct), _stream_tile(refs[na:na + nx], nct), *refs[na + nx:])

    if ln is None:
        ln = (jnp.ones((d,), F32), jnp.zeros((d,), F32))
    if split:
        out_shape = [jax.ShapeDtypeStruct((geo.n_ctx, d), F32), jax.ShapeDtypeStruct((geo.n_lat, d), F32)]
        out_specs = _stream_specs(geo, (None, None), d)
    else:
        out_shape = [jax.ShapeDtypeStruct((geo.n, d), F32)]
        out_specs = [pl.BlockSpec((TM, d), lambda i: (i, 0))]
    n_main = len(out_shape)
    out = pl.pallas_call(
        kern,
        out_shape=out_shape + c_shapes,
        grid=(steps,),
        in_specs=_stream_specs(geo, a, d) + _row_specs(geo, x) + [
            _resident((d, d)), _resident((1, d)), _resident((1, d)), _resident((1, d)),
            _resident((d, dff)), _resident((d, dff)), _resident((dff, d))] + c_in,
        out_specs=out_specs + c_out,
        scratch_shapes=[pltpu.VMEM((TM // TAIL_CHAINS, dff), BF16)] * TAIL_CHAINS,
        compiler_params=_params(("arbitrary",)),
        name="layer_tail",
    )(*_stream_args(a), *_stream_args(x), mod, gn, w_out, b_out.reshape(1, d), ln[0].reshape(1, d),
      ln[1].reshape(1, d), wg, wu, wd, *[w for w, _ in casts])
    main = tuple(out[:n_main]) if split else out[0]
    return main, list(out[n_main:])


DFT_ROWS = 256


def _dft_table_kernel(cb_ref, sb_ref, o_ref, *, seq_len):
    k0 = pl.program_id(0) * DFT_ROWS
    s = lax.broadcasted_iota(jnp.int32, (1, seq_len), 1)
    ang = ((k0 * s) & (2 * seq_len - 1)).astype(F32) * (math.pi / seq_len)
    ca, sa = jnp.cos(ang), jnp.sin(ang)
    cb, sb = cb_ref[...], sb_ref[...]
    o_ref[:, 0:seq_len] = (ca * cb - sa * sb).astype(BF16)
    o_ref[:, seq_len:2 * seq_len] = (sa * cb + ca * sb).astype(BF16)


def dft_tables(seq_len):
    assert seq_len & (seq_len - 1) == 0 and seq_len % DFT_ROWS == 0
    k = jnp.arange(seq_len, dtype=jnp.int32)
    idx = (k[:DFT_ROWS, None] * k[None, :]) % (2 * seq_len)
    ang = idx.astype(F32) * (math.pi / seq_len)
    sign = (1 - 2 * (k % 2)).astype(F32)[:, None]
    cvec = jnp.where(k == 0, 0.5 / seq_len, 1.0 / seq_len).astype(F32)[:, None]
    base = pl.BlockSpec((DFT_ROWS, seq_len), lambda i: (0, 0))
    cs = pl.pallas_call(
        functools.partial(_dft_table_kernel, seq_len=seq_len),
        out_shape=jax.ShapeDtypeStruct((seq_len, 2 * seq_len), BF16),
        grid=(seq_len // DFT_ROWS,),
        in_specs=[base, base],
        out_specs=pl.BlockSpec((DFT_ROWS, 2 * seq_len), lambda i: (i, 0)),
        compiler_params=_params(("parallel",)),
        name=f"dft_table_{seq_len}",
    )(jnp.cos(ang), jnp.sin(ang))
    return cs, sign, cvec


def _filter_kernel(zp_ref, zm_ref, tp_ref, tm_ref, keep_ref, w1_ref, b1_ref, fr_ref, w2_ref, b2_ref,
                   w3p_ref, w3m_ref, dl_ref, sign_ref, cvec_ref, cs_ref, hr_ref, hi_ref, hn_ref,
                   hp_ref, hm_ref):
    blk = zp_ref.shape[0]
    absdl = jnp.abs(dl_ref[...])

    @pl.when(pl.program_id(1) == 0)
    def _():
        fw = hp_ref.shape[1]
        pre = jnp.concatenate([_dot_hi(zp_ref[...], w1_ref[...]), _dot_hi(zm_ref[...], w1_ref[...])], axis=1)
        h = jnp.sin(fr_ref[0:1, :] * (pre + b1_ref[...]))
        h = jnp.sin(fr_ref[1:2, :] * (_dot_hi(h, w2_ref[...]) + b2_ref[...]))
        hp_ref[...] = h[:, 0:fw]
        hm_ref[...] = h[:, fw:2 * fw]

    def taps(h_ref, t_ref, w3_ref):
        return _dot(h_ref[...].astype(BF16), w3_ref[...].astype(BF16)) * jnp.exp(-t_ref[...] * absdl)

    gp = taps(hp_ref, tp_ref, w3p_ref)
    gm = taps(hm_ref, tm_ref, w3m_ref) * keep_ref[...]
    gs = gp + gm
    gd = gp - gm
    cv = cvec_ref[...]
    hr_ref[...] = cv * _dot(cs_ref[:, 0:blk], gs.astype(BF16))
    hi_ref[...] = cv * _dot(cs_ref[:, blk:2 * blk], gd.astype(BF16))
    hn_ref[...] = jnp.sum(gs * sign_ref[...], axis=0, keepdims=True) * (0.5 / blk)


def hyena_block_spectra(seq_len, blk, d, tables, w1, b1, freq, w2, b2, w3):
    cs, sign, cvec = tables
    emb, fw = w1.shape
    bands = (emb - 1) // 2
    nblk = seq_len // blk
    ndl = 2 * nblk - 1
    dd = jnp.arange(ndl, dtype=jnp.int32)[:, None] - (nblk - 1)
    m = jnp.arange(blk, dtype=jnp.int32)[None, :]
    pos_p = jnp.where(dd >= 0, dd * blk + m, -dd * blk - m)
    pos_m = jnp.where(dd >= 1, dd * blk - m, -dd * blk + m)
    keep = jnp.where((m == 0) & (dd != 0), 0.0, 1.0).astype(F32)

    def features(pos):
        pos = pos.reshape(-1, 1).astype(F32)
        t = pos / (seq_len - 1)
        w = 2.0 * math.pi * pos / seq_len
        f = jnp.linspace(1e-4, bands - 1, bands, dtype=F32)[None, :]
        z = jnp.concatenate([t, jnp.cos(f * w), -jnp.sin(f * w)], axis=-1)
        return jnp.pad(z, ((0, 0), (0, LANE - emb))), t

    zp, tp = features(pos_p)
    zm, tm = features(pos_m)
    w1p = jnp.pad(w1, ((0, LANE - emb), (0, 0)))
    zeros = jnp.zeros_like(w2)
    w2_pair = jnp.concatenate([jnp.concatenate([w2, zeros], axis=1),
                               jnp.concatenate([zeros, w2], axis=1)], axis=0)
    deltas = jnp.linspace(math.log(HY_DECAY_TARGET) / HY_SLOW_PCT,
                          math.log(HY_DECAY_TARGET) / HY_FAST_PCT, d, dtype=F32)[None, :]
    fdc = 2 * DC
    nd = d // fdc
    full = lambda shape: pl.BlockSpec(shape, lambda i, j: (0,) * len(shape))
    rows = lambda width: pl.BlockSpec((blk, width), lambda i, j: (i, 0))
    w3_p = pl.BlockSpec((fw, fdc), lambda i, j: (0, jnp.where(i >= nblk - 1, 0, 2 * nd) + j))
    w3_m = pl.BlockSpec((fw, fdc), lambda i, j: (0, jnp.where(i >= nblk, 0, 2 * nd) + j))
    spec_out = pl.BlockSpec((blk, fdc), lambda i, j: (i, j))
    return pl.pallas_call(
        _filter_kernel,
        out_shape=(jax.ShapeDtypeStruct((ndl * blk, 2 * d), F32),
                   jax.ShapeDtypeStruct((ndl * blk, 2 * d), F32),
                   jax.ShapeDtypeStruct((ndl, 1, 2 * d), F32)),
        grid=(ndl, 2 * nd),
        in_specs=[rows(LANE), rows(LANE), rows(1), rows(1), rows(1),
                  full((LANE, fw)), full((1, 2 * fw)), full((2, 2 * fw)), full((2 * fw, 2 * fw)),
                  full((1, 2 * fw)),
                  w3_p, w3_m,
                  pl.BlockSpec((1, fdc), lambda i, j: (0, j % nd)),
                  full((blk, 1)), full((blk, 1)), full((blk, 2 * blk))],
        out_specs=(spec_out, spec_out, pl.BlockSpec((None, 1, fdc), lambda i, j: (i, 0, j))),
        scratch_shapes=[pltpu.VMEM((blk, fw), F32), pltpu.VMEM((blk, fw), F32)],
        compiler_params=_params(("parallel", "arbitrary")),
        name=f"hyena_filter_{seq_len}",
    )(zp, zm, tp, tm, keep.reshape(-1, 1), w1p, jnp.tile(b1.reshape(1, fw), (1, 2)), jnp.tile(freq, (1, 2)),
      w2_pair, jnp.tile(b2.reshape(1, fw), (1, 2)), w3, w3, deltas, sign, cvec, cs)


HY_GROUP = 2
HY_BLOCK = 512
HY_STRIP = 16
HY_HALO = 16


def _dwconv3_rows(p_ref, base, r, nrows, seq_len, w_ref, b_ref):
    lo, hi = max(r - HY_HALO, 0), min(r + nrows + HY_HALO, seq_len)
    win = p_ref[base + lo:base + hi, :].astype(F32)
    n = hi - lo
    prev = pltpu.roll(win, 1, 0)
    nxt = pltpu.roll(win, n - 1, 0)
    rows = lax.broadcasted_iota(jnp.int32, (n, 1), 0)
    if lo == 0:
        prev = jnp.where(rows == 0, 0.0, prev)
    if hi == seq_len:
        nxt = jnp.where(rows == n - 1, 0.0, nxt)
    y = w_ref[0:1, :] * prev + w_ref[1:2, :] * win + w_ref[2:3, :] * nxt + b_ref[...]
    return y[r - lo:r - lo + nrows, :]


def _hyena_mix_kernel(pv_ref, p1_ref, p2_ref, wv_ref, w1_ref, w2_ref, bv_ref, b1_ref, b2_ref,
                      cs_ref, hr1_ref, hi1_ref, hn1_ref, hr2_ref, hi2_ref, hn2_ref, skip_ref,
                      o_ref, z_ref, zb_ref, fa_ref, fb_ref, y_ref, *, seq_len, blk):
    dc = pv_ref.shape[1]
    groups = pv_ref.shape[0] // seq_len
    nblk = seq_len // blk
    tile = lambda a: jnp.concatenate([a] * groups, axis=1)
    rows = lax.broadcasted_iota(jnp.int32, (blk, 1), 0)
    sign = (1 - 2 * (rows % 2)).astype(F32)

    def stage(r, val):
        z_ref[r:r + blk, :] = val
        zb_ref[r:r + blk, :] = val.astype(BF16)
        return jnp.sum(val * sign, axis=0, keepdims=True)

    def block_products(i_out, hr_ref, hi_ref):
        def strip(r0):
            yr, yi = [None] * groups, [None] * groups
            for j in range(nblk):
                h0 = (i_out - j + nblk - 1) * blk
                hr = hr_ref[h0 + r0:h0 + r0 + HY_STRIP, :]
                hi = hi_ref[h0 + r0:h0 + r0 + HY_STRIP, :]
                for g in range(groups):
                    a = fa_ref[j * blk + r0:j * blk + r0 + HY_STRIP, g * dc:(g + 1) * dc]
                    b = fb_ref[j * blk + r0:j * blk + r0 + HY_STRIP, g * dc:(g + 1) * dc]
                    tr, ti = a * hr - b * hi, a * hi + b * hr
                    yr[g], yi[g] = (tr, ti) if yr[g] is None else (yr[g] + tr, yi[g] + ti)
            for g in range(groups):
                y_ref[r0:r0 + HY_STRIP, g * dc:(g + 1) * dc] = yr[g].astype(BF16)
                y_ref[blk + r0:blk + r0 + HY_STRIP, g * dc:(g + 1) * dc] = yi[g].astype(BF16)

        for r0 in range(0, blk, HY_STRIP):
            strip(r0)

    def transform(hr_ref, hi_ref, hn_ref, skip, mult_ref, mw_ref, mb_ref, nyqs, emit):
        for j in range(nblk):
            r = j * blk
            fa_ref[r:r + blk, :] = _dot(cs_ref[:, 0:blk], zb_ref[r:r + blk, :])
            fb_ref[r:r + blk, :] = _dot(cs_ref[:, blk:2 * blk], zb_ref[r:r + blk, :])
        skip_t = tile(skip)
        out = []
        for i in range(nblk):
            r = i * blk
            block_products(i, hr_ref, hi_ref)
            nyq = None
            for j in range(nblk):
                term = nyqs[j] * tile(hn_ref[i - j + nblk - 1])
                nyq = term if nyq is None else nyq + term
            y = _dot(cs_ref[...], y_ref[...]) + sign * nyq + z_ref[r:r + blk, :] * skip_t
            mult = jnp.concatenate([_dwconv3_rows(mult_ref, g * seq_len, r, blk, seq_len, mw_ref, mb_ref)
                                    for g in range(groups)], axis=1)
            out.append(emit(r, mult * y))
        return out

    nyqs = []
    for j in range(nblk):
        v = jnp.concatenate([_dwconv3_rows(pv_ref, g * seq_len, j * blk, blk, seq_len, wv_ref, bv_ref)
                             for g in range(groups)], axis=1)
        nyqs.append(stage(j * blk, v))

    nyqs = transform(hr1_ref, hi1_ref, hn1_ref, skip_ref[0:1, :], p1_ref, w1_ref, b1_ref, nyqs, stage)

    def write_out(r, val):
        for g in range(groups):
            o_ref[g * seq_len + r:g * seq_len + r + blk, :] = val[:, g * dc:(g + 1) * dc].astype(BF16)

    transform(hr2_ref, hi2_ref, hn2_ref, skip_ref[1:2, :], p2_ref, w2_ref, b2_ref, nyqs, write_out)


def hyena_mix(geo, p, conv_w, conv_b, skip, tables, spectrum, seq_len, row0, nb):
    d = geo.d
    cs = tables[0]
    hr, hi, hn = spectrum
    blk = cs.shape[0]
    ndl = 2 * (seq_len // blk) - 1
    nd = d // DC
    group = max(HY_GROUP, 2048 // seq_len)
    rows = group * seq_len
    width = group * DC
    assert nb % group == 0 and row0 % rows == 0 and seq_len % blk == 0 and blk % HY_STRIP == 0
    blk0 = row0 // rows
    pspec = lambda part: pl.BlockSpec((rows, DC), lambda c, b: (blk0 + b, part * nd + c))
    wspec = lambda part: pl.BlockSpec((3, DC), lambda c, b: (0, part * nd + c))
    bspec = lambda part: pl.BlockSpec((1, DC), lambda c, b: (0, part * nd + c))
    hspec = lambda order: pl.BlockSpec((ndl * blk, DC), lambda c, b: (0, order * nd + c),
                                       pipeline_mode=pl.Buffered(1))
    nspec = lambda order: pl.BlockSpec((ndl, 1, DC), lambda c, b: (0, 0, order * nd + c))
    return pl.pallas_call(
        functools.partial(_hyena_mix_kernel, seq_len=seq_len, blk=blk),
        out_shape=jax.ShapeDtypeStruct((nb * seq_len, d), BF16),
        grid=(nd, nb // group),
        in_specs=[pspec(0), pspec(1), pspec(2), wspec(0), wspec(1), wspec(2), bspec(0), bspec(1), bspec(2),
                  _resident((blk, 2 * blk)),
                  hspec(0), hspec(0), nspec(0), hspec(1), hspec(1), nspec(1),
                  pl.BlockSpec((2, DC), lambda c, b: (0, c))],
        out_specs=pl.BlockSpec((rows, DC), lambda c, b: (b, c)),
        scratch_shapes=[pltpu.VMEM((seq_len, width), F32), pltpu.VMEM((seq_len, width), BF16),
                        pltpu.VMEM((seq_len, width), F32), pltpu.VMEM((seq_len, width), F32),
                        pltpu.VMEM((2 * blk, width), BF16)],
        compiler_params=_params(("arbitrary", "arbitrary")),
        name=f"hyena_mix_{seq_len}",
    )(p, p, p, conv_w, conv_w, conv_w, conv_b, conv_b, conv_b, cs,
      hr, hi, hn, hr, hi, hn, skip)


CONV_ROWS = 64
CONV_PAD = 16


def _stage_padded(src, pad_ref):
    n = src.shape[0]
    zeros = jnp.zeros((CONV_PAD, pad_ref.shape[1]), F32)
    pad_ref[0:CONV_PAD, :] = zeros
    pad_ref[CONV_PAD + n:2 * CONV_PAD + n, :] = zeros
    pad_ref[CONV_PAD:CONV_PAD + n, :] = src


def _dwconv_chunks(pad_ref, w_ref, n, ktaps, emit):
    half = ktaps // 2
    lo = (CONV_PAD - half) // SUBLANES * SUBLANES
    hi = -(-(CONV_PAD + half + CONV_ROWS) // SUBLANES) * SUBLANES
    rows = hi - lo

    def chunk(ci, carry):
        r0 = pl.multiple_of(ci * CONV_ROWS, CONV_ROWS)
        window = pad_ref[pl.ds(pl.multiple_of(r0 + lo, SUBLANES), rows), :]
        acc = None
        for k in range(ktaps):
            off = CONV_PAD - half + k - lo
            win = window if off == 0 else pltpu.roll(window, rows - off, 0)
            term = w_ref[k:k + 1, :] * win[0:CONV_ROWS, :]
            acc = term if acc is None else acc + term
        emit(r0, acc)
        return carry

    lax.fori_loop(0, n // CONV_ROWS, chunk, 0)


def _conv_group(seq_len, nb):
    group = max(1, min(nb, 1024 // seq_len))
    assert nb % group == 0
    return group


def _cf_conv_kernel(u_ref, w_ref, b_ref, o_ref, pad_ref, *, seq_len):
    bias = b_ref[...]
    for base in range(0, u_ref.shape[0], seq_len):
        _stage_padded(u_ref[base:base + seq_len, :].astype(F32), pad_ref)

        def emit(r0, acc, base=base):
            o_ref[pl.ds(base + r0, CONV_ROWS), :] = (acc + bias).astype(BF16)

        _dwconv_chunks(pad_ref, w_ref, seq_len, w_ref.shape[0], emit)


def conformer_dwconv(geo, u, w, b, seq_len, row0, nb):
    d = geo.d
    dc = 256
    ktaps = w.shape[0]
    rows = _conv_group(seq_len, nb) * seq_len
    blk0 = row0 // rows
    return pl.pallas_call(
        functools.partial(_cf_conv_kernel, seq_len=seq_len),
        out_shape=jax.ShapeDtypeStruct((nb * seq_len, d), BF16),
        grid=(nb * seq_len // rows, d // dc),
        in_specs=[pl.BlockSpec((rows, dc), lambda b_, c: (blk0 + b_, c)),
                  pl.BlockSpec((ktaps, dc), lambda b_, c: (0, c)),
                  pl.BlockSpec((1, dc), lambda b_, c: (0, c))],
        out_specs=pl.BlockSpec((rows, dc), lambda b_, c: (b_, c)),
        scratch_shapes=[pltpu.VMEM((seq_len + 2 * CONV_PAD, dc), F32)],
        compiler_params=_params(("parallel", "parallel")),
        name=f"conformer_dwconv_{seq_len}",
    )(u, w, b.reshape(1, d))


def _sc_conv_kernel(q_ref, bg_ref, w_ref, o_ref, pad_ref, *, seq_len):
    for base in range(0, q_ref.shape[0], seq_len):
        _stage_padded(q_ref[base:base + seq_len, :].astype(F32), pad_ref)

        def emit(r0, acc, base=base):
            gate = bg_ref[pl.ds(base + r0, CONV_ROWS), :].astype(F32)
            o_ref[pl.ds(base + r0, CONV_ROWS), :] = (gate * acc).astype(BF16)

        _dwconv_chunks(pad_ref, w_ref, seq_len, w_ref.shape[0], emit)


def shortconv_gate(geo, q, bg, w, seq_len, row0, nb):
    d = geo.d
    dc = 256
    ktaps = w.shape[0]
    rows = _conv_group(seq_len, nb) * seq_len
    blk0 = row0 // rows
    seq = pl.BlockSpec((rows, dc), lambda b_, c: (blk0 + b_, c))
    return pl.pallas_call(
        functools.partial(_sc_conv_kernel, seq_len=seq_len),
        out_shape=jax.ShapeDtypeStruct((nb * seq_len, d), BF16),
        grid=(nb * seq_len // rows, d // dc),
        in_specs=[seq, seq, pl.BlockSpec((ktaps, dc), lambda b_, c: (0, c))],
        out_specs=pl.BlockSpec((rows, dc), lambda b_, c: (b_, c)),
        scratch_shapes=[pltpu.VMEM((seq_len + 2 * CONV_PAD, dc), F32)],
        compiler_params=_params(("parallel", "parallel")),
        name=f"shortconv_gate_{seq_len}",
    )(q, bg, w)


def _mla_proj_kernel(x, mod_ref, gn_ref, wdq_ref, gq_ref, wun_ref, wup_ref, wus_ref,
                     wkc_ref, wkp_ref, wks_ref, gkv_ref, cosq_ref, sinq_ref,
                     q_ref, ckv_ref, kpe_ref, *, heads, nope, rope, qscale):
    h = _prenorm(x, mod_ref, gn_ref, 0)
    cos2, sin2 = cosq_ref[...], sinq_ref[...]
    cq_raw = _dot(h, wdq_ref[...])
    ckv_raw = _dot(h, wkc_ref[...])
    kpe = _dot(h, wkp_ref[...]) * cos2[:, 0:rope] + _dot(h, wks_ref[...]) * sin2[:, 0:rope]
    cq = (_rms(cq_raw) * gq_ref[...]).astype(BF16)
    qn = _dot(cq, wun_ref[...]) * qscale
    cosq, sinq = cos2 * qscale, sin2 * qscale
    for pair in range(heads // 2):
        lo = pair * 2 * rope
        qp = (_dot(cq, wup_ref[:, lo:lo + 2 * rope]) * cosq
              + _dot(cq, wus_ref[:, lo:lo + 2 * rope]) * sinq)
        for j in range(2):
            hd = 2 * pair + j
            q_ref[hd, :, 0:nope] = qn[:, hd * nope:(hd + 1) * nope].astype(BF16)
            q_ref[hd, :, nope:nope + rope] = qp[:, j * rope:(j + 1) * rope].astype(BF16)
    ckv_ref[...] = _rms(ckv_raw) * gkv_ref[...]
    kpe_ref[...] = kpe


def mla_project(geo, x, mod, gn, wdq, gq, wun, wup, wus, wkc, wkp, wks, gkv, cos2, sin2, heads, nope, rope,
                qscale):
    d = geo.d
    qr, kvr = wdq.shape[1], wkc.shape[1]
    row = lambda w: pl.BlockSpec((TM, w), lambda i: (i, 0))
    rot = pl.BlockSpec((TM, 2 * rope), lambda i: (_rope_block_of_tile(geo, i), 0))
    body = functools.partial(_mla_proj_kernel, heads=heads, nope=nope, rope=rope, qscale=qscale)
    return pl.pallas_call(
        _row_kernel(body, geo, x),
        out_shape=(jax.ShapeDtypeStruct((heads, geo.n, nope + rope), BF16),
                   jax.ShapeDtypeStruct((geo.n, kvr), F32),
                   jax.ShapeDtypeStruct((geo.n, rope), F32)),
        grid=(geo.n // TM,),
        in_specs=_row_specs(geo, x) + [_resident((d, qr)), _resident((1, qr)),
                                    _resident((qr, heads * nope)), _resident((qr, heads * rope)),
                                    _resident((qr, heads * rope)),
                                    _resident((d, kvr)), _resident((d, rope)), _resident((d, rope)),
                                    _resident((1, kvr)), rot, rot],
        out_specs=(pl.BlockSpec((heads, TM, nope + rope), lambda i: (0, i, 0)), row(kvr), row(rope)),
        compiler_params=_params(("parallel",)),
        name="mla_project",
    )(*_stream_args(x), mod, gn, wdq, gq.reshape(1, qr), wun, wup, wus, wkc, wkp, wks, gkv.reshape(1, kvr),
      cos2, sin2)


KV_ROWS = 256


def _mla_kv_kernel(*refs, heads, nope, rope, past_tiles, sub):
    wn_ref, wv_ref, k_ref, v_ref = refs[-4:]
    new = refs[-4 - 2 * sub:-4]
    for u in range(sub):
        rows = slice(u * KV_ROWS, (u + 1) * KV_ROWS)
        ckv, kpe = new[2 * u][...], new[2 * u + 1][...]
        if past_tiles:
            cached = pl.program_id(1) * sub + u < past_tiles
            ckv = jnp.where(cached, refs[0][...], ckv)
            kpe = jnp.where(cached, refs[1][...], kpe)
        c = ckv.astype(BF16)
        kn = _dot(c, wn_ref[...])
        v_ref[rows, :] = _dot(c, wv_ref[...]).astype(BF16)
        kp = kpe.astype(BF16)
        for hd in range(heads):
            k_ref[hd, rows, 0:nope] = kn[:, hd * nope:(hd + 1) * nope].astype(BF16)
            k_ref[hd, rows, nope:nope + rope] = kp


def mla_expand_kv(ckv, kpe, cache, layer, seq_len, row0, nb, wn, wv, heads, nope, rope, vdim):
    r = ckv.shape[1]
    past = 0 if cache is None else cache[0].shape[2]
    assert past % KV_ROWS == 0 and seq_len % KV_ROWS == 0 and row0 % KV_ROWS == 0
    pt, st, blk0 = past // KV_ROWS, seq_len // KV_ROWS, row0 // KV_ROWS
    lk = past + seq_len
    tiles = lk // KV_ROWS
    sub = 3 if (tiles % 3 == 0 and pt <= 1) else 1
    specs, args = [], []
    for u in range(sub):
        new_map = lambda b, t, u=u: (blk0 + b * st + jnp.maximum(t * sub + u - pt, 0), 0)
        specs += [pl.BlockSpec((KV_ROWS, r), new_map), pl.BlockSpec((KV_ROWS, rope), new_map)]
        args += [ckv, kpe]
    specs += [_resident((r, heads * nope)), _resident((r, heads * vdim))]
    args += [wn, wv]
    if cache is not None:
        old_map = lambda b, t: (b, layer, jnp.minimum(t * sub, pt - 1), 0)
        specs = [pl.BlockSpec((None, None, KV_ROWS, r), old_map),
                 pl.BlockSpec((None, None, KV_ROWS, rope), old_map)] + specs
        args = list(cache) + args
    return pl.pallas_call(
        functools.partial(_mla_kv_kernel, heads=heads, nope=nope, rope=rope, past_tiles=pt, sub=sub),
        out_shape=(jax.ShapeDtypeStruct((nb, heads, lk, nope + rope), BF16),
                   jax.ShapeDtypeStruct((nb, lk, heads * vdim), BF16)),
        grid=(nb, tiles // sub),
        in_specs=specs,
        out_specs=(pl.BlockSpec((None, heads, sub * KV_ROWS, nope + rope), lambda b, t: (b, 0, t, 0)),
                   pl.BlockSpec((None, sub * KV_ROWS, heads * vdim), lambda b, t: (b, t, 0))),
        compiler_params=_params(("parallel", "arbitrary")),
        name=f"mla_expand_kv_{lk}",
    )(*args)


ATTN_ROWS = 256


def _attn_kernel(q_ref, k_ref, v_ref, o_ref, *, vdim):
    hg, tq, _ = q_ref.shape
    chains = [(h, r) for h in range(hg) for r in range(0, tq, ATTN_ROWS)]

    def scores(h, r):
        return lax.dot_general(q_ref[h, r:r + ATTN_ROWS, :], k_ref[h], (((1,), (1,)), ((), ())),
                               preferred_element_type=F32)

    s_next = scores(*chains[0])
    for i, (h, r) in enumerate(chains):
        s = s_next
        if i + 1 < len(chains):
            s_next = scores(*chains[i + 1])
        m = jnp.max(s, axis=-1, keepdims=True)
        p = jnp.exp2(s - m)
        l = jnp.sum(p, axis=-1, keepdims=True)
        o = _dot(p.astype(BF16), v_ref[:, h * vdim:(h + 1) * vdim])
        o_ref[r:r + ATTN_ROWS, h * vdim:(h + 1) * vdim] = (o * (1.0 / l)).astype(BF16)


def mla_attention(q, k, v, seq_len, row0, nb, vdim):
    heads, _, dq = q.shape
    lk = k.shape[2]
    hg = max(1, min(heads, 2048 // seq_len))
    blk0 = row0 // seq_len
    return pl.pallas_call(
        functools.partial(_attn_kernel, vdim=vdim),
        out_shape=jax.ShapeDtypeStruct((nb * seq_len, heads * vdim), BF16),
        grid=(nb, heads // hg),
        in_specs=[pl.BlockSpec((hg, seq_len, dq), lambda b, h: (h, blk0 + b, 0)),
                  pl.BlockSpec((None, hg, lk, dq), lambda b, h: (b, h, 0, 0)),
                  pl.BlockSpec((None, lk, hg * vdim), lambda b, h: (b, 0, h))],
        out_specs=pl.BlockSpec((seq_len, hg * vdim), lambda b, h: (b, h)),
        compiler_params=_params(("parallel", "parallel")),
        name=f"mla_attention_{seq_len}",
    )(q, k, v)


def rope_tables(geo, rope):
    nf = rope // 4
    rows = geo.l_lat // GRID_W
    row = jnp.repeat(jnp.arange(rows, dtype=F32), GRID_W)
    col = jnp.tile(jnp.arange(GRID_W, dtype=F32), rows)
    inv = jnp.exp(-math.log(ROPE_THETA) * jnp.arange(nf, dtype=F32) * (4.0 / rope))
    ar, ac = row[:, None] * inv, col[:, None] * inv
    cos = jnp.concatenate([jnp.cos(ar), jnp.cos(ar), jnp.cos(ac), jnp.cos(ac)], axis=-1)
    sin = jnp.concatenate([-jnp.sin(ar), jnp.sin(ar), -jnp.sin(ac), jnp.sin(ac)], axis=-1)
    cos = jnp.concatenate([jnp.ones((TM, rope), F32), cos], axis=0)
    sin = jnp.concatenate([jnp.zeros((TM, rope), F32), sin], axis=0)
    return jnp.tile(cos, (1, 2)), jnp.tile(sin, (1, 2))


def _rope_block_of_tile(geo, i):
    nct, per_seq = geo.n_ctx // TM, geo.l_lat // TM
    return jnp.where(i < nct, 0, 1 + (i - nct) % per_seq)


def _swap_rope_halves(w, rope):
    nf = rope // 4
    shape = w.shape
    w = w.reshape(shape[:-1] + (shape[-1] // (2 * nf), 2, nf))
    return w[..., ::-1, :].reshape(shape)


def kernel(x_prompt, x_sample, c, cache_ckv, cache_kpe, c_ctx, w_mod, b_mod, norm_g, ffn_w_gate, ffn_w_up, ffn_w_down, hy_w_in, hy_b_in, hy_conv_w, hy_conv_b, hy_f_w1, hy_f_b1, hy_f_freq, hy_f_w2, hy_f_b2, hy_f_w3, hy_skip, hy_w_out, hy_b_out, cf_w_pw1, cf_b_pw1, cf_dw_w, cf_dw_b, cf_ln_g, cf_ln_b, cf_w_pw2, cf_b_pw2, sc_w_in, sc_conv_w, sc_w_out, mla_w_dq, mla_g_q, mla_w_uq, mla_w_dkv, mla_g_kv, mla_w_ukv, mla_w_o):
    nb_ctx, l_ctx, d = x_prompt.shape
    nb_lat, l_lat, _ = x_sample.shape
    geo = Geometry(nb_ctx, l_ctx, nb_lat, l_lat, d)
    depth = w_mod.shape[0]
    n_mixers = 4

    x = (x_prompt.reshape(geo.n_ctx, d), x_sample.reshape(geo.n_lat, d))
    cond = jnp.concatenate([c_ctx[None, :], c], axis=0)
    n_cond = cond.shape[0]
    cond = jnp.pad(cond, ((0, -n_cond % 8), (0, 0)))
    mod_all = modulation_all(cond, w_mod, b_mod)[:, :n_cond].reshape(depth, n_cond, 6, d)

    passes = ((l_ctx, 0, nb_ctx), (l_lat, geo.n_ctx, nb_lat))
    zero_bias = jnp.zeros((d,), F32)
    ffn_stacks = (ffn_w_gate, ffn_w_up, ffn_w_down)
    mixer_proj = {0: (hy_w_in, hy_w_out), 1: (cf_w_pw1, cf_w_pw2), 2: (sc_w_in, sc_w_out), 3: (mla_w_dq, mla_w_o)}
    new_ckv, new_kpe = [], []

    def tail_casts(layer):
        w_in_, w_out_ = mixer_proj[layer % n_mixers]
        return [(w, layer) for w in ffn_stacks] + [(w_in_, layer // n_mixers), (w_out_, layer // n_mixers)]

    ready = {}

    for i in range(depth):
        m, j = i % n_mixers, i // n_mixers
        mod, gn = mod_all[i], norm_g[i]
        w_in = ready.get("w_in")
        if w_in is None:
            w_in = mixer_proj[m][0][j].astype(BF16)
        if m == 0:
            own = [] if ready else tail_casts(i)[:3] + tail_casts(i)[4:]
            p, cast = hyena_in_proj(geo, x, mod, gn, w_in, hy_b_in[j], own)
            if own:
                ready = dict(ffn=cast[:3], w_out=cast[3])
            parts = []
            for seq_len, row0, nb in passes:
                blk = min(seq_len, HY_BLOCK)
                tables = dft_tables(blk)
                spectrum = hyena_block_spectra(seq_len, blk, d, tables, hy_f_w1[j], hy_f_b1[j], hy_f_freq[j],
                                               hy_f_w2[j], hy_f_b2[j], hy_f_w3[j])
                parts.append(hyena_mix(geo, p, hy_conv_w[j], hy_conv_b[j].reshape(1, -1), hy_skip[j],
                                       tables, spectrum, seq_len, row0, nb))
            a, w_out, b_out, ln = tuple(parts), hy_w_out[j], hy_b_out[j], None
        elif m == 1:
            u = conformer_in_proj(geo, x, mod, gn, w_in, cf_b_pw1[j])
            a = tuple(conformer_dwconv(geo, u, cf_dw_w[j], cf_dw_b[j], seq_len, row0, nb)
                      for seq_len, row0, nb in passes)
            w_out, b_out, ln = cf_w_pw2[j], cf_b_pw2[j], (cf_ln_g[j], cf_ln_b[j])
        elif m == 2:
            bg, q = shortconv_in_proj(geo, x, mod, gn, w_in)
            a = tuple(shortconv_gate(geo, q, bg, sc_conv_w[j], seq_len, row0, nb)
                      for seq_len, row0, nb in passes)
            w_out, b_out, ln = sc_w_out[j], zero_bias, None
        else:
            kvr = mla_g_kv.shape[1]
            rope = mla_w_dkv.shape[2] - kvr
            vdim = 128
            heads = mla_w_o.shape[1] // vdim
            nope = mla_w_uq.shape[2] // heads - rope
            wuq = mla_w_uq[j].reshape(-1, heads, nope + rope)
            wun = wuq[:, :, :nope].reshape(-1, heads * nope).astype(BF16)
            wup = wuq[:, :, nope:].reshape(-1, heads * rope)
            wukv = mla_w_ukv[j].reshape(kvr, heads, nope + vdim)
            wkn = wukv[:, :, :nope].reshape(kvr, heads * nope).astype(BF16)
            wkv = wukv[:, :, nope:].reshape(kvr, heads * vdim).astype(BF16)
            wkc, wkp = mla_w_dkv[j][:, :kvr], mla_w_dkv[j][:, kvr:]
            cos2, sin2 = rope_tables(geo, rope)
            q, ckv, kpe = mla_project(geo, x, mod, gn, w_in, mla_g_q[j], wun,
                                      wup.astype(BF16), _swap_rope_halves(wup, rope).astype(BF16),
                                      wkc.astype(BF16), wkp.astype(BF16),
                                      _swap_rope_halves(wkp, rope).astype(BF16), mla_g_kv[j],
                                      cos2, sin2, heads, nope, rope,
                                      (nope + rope) ** -0.5 * math.log2(math.e))
            new_ckv.append(ckv[:geo.n_ctx].reshape(nb_ctx, l_ctx, kvr))
            new_kpe.append(kpe[:geo.n_ctx].reshape(nb_ctx, l_ctx, rope))
            parts = []
            for (seq_len, row0, nb), cache in zip(passes, (None, (cache_ckv, cache_kpe))):
                kk, vv = mla_expand_kv(ckv, kpe, cache, j, seq_len, row0, nb, wkn, wkv, heads, nope, rope, vdim)
                parts.append(mla_attention(q, kk, vv, seq_len, row0, nb, vdim))
            a, w_out, b_out, ln = tuple(parts), mla_w_o[j], zero_bias, None
        ffn = ready.get("ffn") or [w[i].astype(BF16) for w in ffn_stacks]
        w_out = ready["w_out"] if "w_out" in ready else w_out.astype(BF16)
        last = i == depth - 1
        x, cast = layer_tail(geo, a, x, mod, gn, w_out, b_out, ln, *ffn, split=last,
                             casts=[] if last else tail_casts(i + 1))
        ready = {} if last else dict(ffn=cast[:3], w_in=cast[3], w_out=cast[4])

    y_prompt = x[0].reshape(nb_ctx, l_ctx, d)
    y_sample = x[1].reshape(nb_lat, l_lat, d)
    return (y_prompt, y_sample, jnp.stack(new_ckv, axis=1), jnp.stack(new_kpe, axis=1))
```

```python
import functools
import math

import jax
import jax.numpy as jnp
from jax import lax
from jax.experimental import pallas as pl
from jax.experimental.pallas import tpu as pltpu

F32 = jnp.float32
BF16 = jnp.bfloat16

NORM_EPS = 1e-6
GRID_W = 64
ROPE_THETA = 10000.0
HY_DECAY_TARGET = 1e-2
HY_FAST_PCT = 0.3
HY_SLOW_PCT = 1.5

V7X_VMEM_LIMIT_BYTES = 60000 * 1024
LANE = 128
SUBLANES = 8

TM = 512
DC = 256


def _dot(a, b):
    return jnp.dot(a, b, preferred_element_type=F32)


def _dot_hi(a, b):
    return jnp.dot(a, b, preferred_element_type=F32, precision=lax.Precision.HIGHEST)


def _rms(x):
    return x * lax.rsqrt(jnp.mean(x * x, axis=-1, keepdims=True) + NORM_EPS)


def _sigmoid(x):
    return 1.0 / (1.0 + jnp.exp(-x))


def _silu(x):
    return x * _sigmoid(x)


def _params(semantics):
    return pltpu.CompilerParams(dimension_semantics=semantics,
                                vmem_limit_bytes=V7X_VMEM_LIMIT_BYTES)


def _resident(shape):
    zeros = (0,) * len(shape)
    return pl.BlockSpec(shape, lambda *_: zeros, pipeline_mode=pl.Buffered(1))


class Geometry:
    def __init__(self, nb_ctx, l_ctx, nb_lat, l_lat, d):
        self.nb_ctx, self.l_ctx, self.nb_lat, self.l_lat, self.d = nb_ctx, l_ctx, nb_lat, l_lat, d
        self.n_ctx = nb_ctx * l_ctx
        self.n_lat = nb_lat * l_lat
        self.n = self.n_ctx + self.n_lat
        assert self.n_ctx % TM == 0 and l_lat % TM == 0 and self.n_ctx % l_lat == 0

    def cond_of_tile(self, i):
        row = i * TM
        return jnp.where(row < self.n_ctx, 0, 1 + (row - self.n_ctx) // self.l_lat)


def _mod_kernel(c_ref, w_ref, b_ref, o_ref):
    a = _silu(c_ref[...]).astype(BF16)
    o_ref[...] = _dot(a, w_ref[...].astype(BF16)) + b_ref[...]


def modulation_all(cond, w_mod, b_mod):
    depth, d, d6 = w_mod.shape
    r = cond.shape[0]
    tn = 1536
    return pl.pallas_call(
        _mod_kernel,
        out_shape=jax.ShapeDtypeStruct((depth, r, d6), F32),
        grid=(depth, d6 // tn),
        in_specs=[pl.BlockSpec((r, d), lambda l, j: (0, 0)),
                  pl.BlockSpec((None, d, tn), lambda l, j: (l, 0, j)),
                  pl.BlockSpec((None, 1, tn), lambda l, j: (l, 0, j))],
        out_specs=pl.BlockSpec((None, r, tn), lambda l, j: (l, 0, j)),
        compiler_params=_params(("parallel", "parallel")),
        name="modulation",
    )(cond, w_mod, b_mod.reshape(depth, 1, d6))


def _stream_args(op):
    return list(op) if isinstance(op, tuple) else [op]


def _stream_specs(geo, op, width):
    if not isinstance(op, tuple):
        return [pl.BlockSpec((TM, width), lambda i: (i, 0))]
    nct = geo.n_ctx // TM
    return [pl.BlockSpec((TM, width), lambda i: (jnp.minimum(i, nct - 1), 0)),
            pl.BlockSpec((TM, width), lambda i: (jnp.maximum(i - nct, 0), 0))]


def _stream_tile(refs, nct):
    if len(refs) == 1:
        return refs[0][...]
    return jnp.where(pl.program_id(0) < nct, refs[0][...], refs[1][...])


def _prenorm(x, mod_ref, gn_ref, which):
    g = gn_ref[2 * which:2 * which + 1, :]
    sh = mod_ref[3 * which:3 * which + 1, :]
    sc = mod_ref[3 * which + 1:3 * which + 2, :]
    return (_rms(x) * (g * (1.0 + sc)) + sh).astype(BF16)


def _row_specs(geo, x):
    d = geo.d
    return _stream_specs(geo, x, d) + [
        pl.BlockSpec((None, 6, d), lambda i: (geo.cond_of_tile(i), 0, 0)),
        pl.BlockSpec((4, d), lambda i: (0, 0))]


def _row_kernel(body, geo, x):
    nx, nct = len(_stream_args(x)), geo.n_ctx // TM

    def kern(*refs):
        body(_stream_tile(refs[:nx], nct), *refs[nx:])

    return kern


def _side_cast_plan(picks, steps):
    in_specs, out_specs, shapes = [], [], []
    for w, layer in picks:
        _, r, c = w.shape
        s = max(k for k in range(1, steps + 1) if r % k == 0 and (r // k) % 16 == 0)
        in_specs.append(pl.BlockSpec((None, r // s, c),
                                     lambda i, s=s, layer=layer: (layer, jnp.minimum(i, s - 1), 0)))
        out_specs.append(pl.BlockSpec((r // s, c), lambda i, s=s: (jnp.minimum(i, s - 1), 0)))
        shapes.append(jax.ShapeDtypeStruct((r, c), BF16))
    return in_specs, out_specs, shapes


def _side_cast(in_refs, out_refs):
    for i_ref, o_ref in zip(in_refs, out_refs):
        o_ref[...] = i_ref[...].astype(BF16)


def _hy_proj_kernel(x, mod_ref, gn_ref, w_ref, b_ref, *rest):
    n_cast = (len(rest) - 1) // 2
    o_ref = rest[n_cast]
    _side_cast(rest[:n_cast], rest[n_cast + 1:])
    h = _prenorm(x, mod_ref, gn_ref, 0)
    step = 512
    for c in range(0, o_ref.shape[1], step):
        o_ref[:, c:c + step] = (_dot(h, w_ref[:, c:c + step]) + b_ref[:, c:c + step]).astype(BF16)


def hyena_in_proj(geo, x, mod, gn, w_in, b_in, casts=()):
    d, d3 = w_in.shape
    steps = geo.n // TM
    c_in, c_out, c_shapes = _side_cast_plan(casts, steps)
    out = pl.pallas_call(
        _row_kernel(_hy_proj_kernel, geo, x),
        out_shape=[jax.ShapeDtypeStruct((geo.n, d3), BF16)] + c_shapes,
        grid=(steps,),
        in_specs=_row_specs(geo, x) + [_resident((d, d3)), _resident((1, d3))] + c_in,
        out_specs=[pl.BlockSpec((TM, d3), lambda i: (i, 0))] + c_out,
        compiler_params=_params(("arbitrary",)),
        name="hyena_in_proj",
    )(*_stream_args(x), mod, gn, w_in, b_in.reshape(1, d3), *[w for w, _ in casts])
    return out[0], list(out[1:])


def _cf_proj_kernel(x, mod_ref, gn_ref, w_ref, b_ref, o_ref):
    h = _prenorm(x, mod_ref, gn_ref, 0)
    d = o_ref.shape[1]
    step = 512
    for c in range(0, d, step):
        a = _dot(h, w_ref[:, c:c + step]) + b_ref[:, c:c + step]
        g = _dot(h, w_ref[:, d + c:d + c + step]) + b_ref[:, d + c:d + c + step]
        o_ref[:, c:c + step] = (a * _sigmoid(g)).astype(BF16)


def conformer_in_proj(geo, x, mod, gn, w_pw1, b_pw1):
    d, d2 = w_pw1.shape
    return pl.pallas_call(
        _row_kernel(_cf_proj_kernel, geo, x),
        out_shape=jax.ShapeDtypeStruct((geo.n, d), BF16),
        grid=(geo.n // TM,),
        in_specs=_row_specs(geo, x) + [_resident((d, d2)), _resident((1, d2))],
        out_specs=pl.BlockSpec((TM, d), lambda i: (i, 0)),
        compiler_params=_params(("parallel",)),
        name="conformer_in_proj",
    )(*_stream_args(x), mod, gn, w_pw1, b_pw1.reshape(1, d2))


def _sc_proj_kernel(x, mod_ref, gn_ref, w_ref, bg_ref, q_ref):
    h = _prenorm(x, mod_ref, gn_ref, 0)
    d = bg_ref.shape[1]
    step = 512
    for c in range(0, d, step):
        bg_ref[:, c:c + step] = _dot(h, w_ref[:, c:c + step]).astype(BF16)
        cg = _dot(h, w_ref[:, d + c:d + c + step])
        hv = _dot(h, w_ref[:, 2 * d + c:2 * d + c + step])
        q_ref[:, c:c + step] = (cg * hv).astype(BF16)


def shortconv_in_proj(geo, x, mod, gn, w_in):
    d, d3 = w_in.shape
    out = jax.ShapeDtypeStruct((geo.n, d), BF16)
    return pl.pallas_call(
        _row_kernel(_sc_proj_kernel, geo, x),
        out_shape=(out, out),
        grid=(geo.n // TM,),
        in_specs=_row_specs(geo, x) + [_resident((d, d3))],
        out_specs=(pl.BlockSpec((TM, d), lambda i: (i, 0)), pl.BlockSpec((TM, d), lambda i: (i, 0))),
        compiler_params=_params(("parallel",)),
        name="shortconv_in_proj",
    )(*_stream_args(x), mod, gn, w_in)


TAIL_CHAINS = 2


def _layer_tail_kernel(a, x, mod_ref, gn_ref, wo_ref, bo_ref, lg_ref, lb_ref, wg_ref, wu_ref, wd_ref,
                       *rest, nct, layer_norm, n_cast):
    acts = rest[-TAIL_CHAINS:]
    outs = rest[n_cast:len(rest) - TAIL_CHAINS - n_cast]
    rows = a.shape[0] // TAIL_CHAINS
    groups = [slice(i * rows, (i + 1) * rows) for i in range(TAIL_CHAINS)]
    if layer_norm:
        u = a.astype(F32)
        mu = jnp.mean(u, axis=-1, keepdims=True)
        uc = u - mu
        var = jnp.mean(uc * uc, axis=-1, keepdims=True)
        a = _silu((uc * lax.rsqrt(var + NORM_EPS)) * lg_ref[...] + lb_ref[...]).astype(BF16)
    gate1 = mod_ref[2:3, :] * gn_ref[1:2, :]
    gate2 = mod_ref[5:6, :] * gn_ref[3:4, :]
    ys = [_dot(a[g, :], wo_ref[...]) + bo_ref[...] for g in groups]
    xs = [x[g, :] + gate1 * _rms(y) for g, y in zip(groups, ys)]
    hs = [_prenorm(xg, mod_ref, gn_ref, 1) for xg in xs]
    step = 256
    for c in range(0, acts[0].shape[1], step):
        for h, act_ref in zip(hs, acts):
            gate = _dot(h, wg_ref[:, c:c + step])
            up = _dot(h, wu_ref[:, c:c + step])
            act_ref[:, c:c + step] = (_silu(gate) * up).astype(BF16)
    ys = [_dot(act_ref[...], wd_ref[...]) for act_ref in acts]
    _side_cast(rest[:n_cast], rest[len(rest) - TAIL_CHAINS - n_cast:len(rest) - TAIL_CHAINS])
    res = [xg + gate2 * _rms(y) for xg, y in zip(xs, ys)]

    def write(o_ref):
        for g, val in zip(groups, res):
            o_ref[g, :] = val

    if len(outs) == 1:
        write(outs[0])
    else:
        i = pl.program_id(0)
        pl.when(i < nct)(lambda: write(outs[0]))
        pl.when(i >= nct)(lambda: write(outs[1]))


def layer_tail(geo, a, x, mod, gn, w_out, b_out, ln, wg, wu, wd, split=False, casts=()):
    d, dff = wg.shape
    steps = geo.n // TM
    na, nx, nct = len(_stream_args(a)), len(_stream_args(x)), geo.n_ctx // TM
    c_in, c_out, c_shapes = _side_cast_plan(casts, steps)
    body = functools.partial(_layer_tail_kernel, nct=nct, layer_norm=ln is not None, n_cast=len(c_in))

    def kern(*refs):
        body(_stream_tile(refs[:na], nct), _stream_tile(refs[na:na + nx], nct), *refs[na + nx:])

    if ln is None:
        ln = (jnp.ones((d,), F32), jnp.zeros((d,), F32))
    if split:
        out_shape = [jax.ShapeDtypeStruct((geo.n_ctx, d), F32), jax.ShapeDtypeStruct((geo.n_lat, d), F32)]
        out_specs = _stream_specs(geo, (None, None), d)
    else:
        out_shape = [jax.ShapeDtypeStruct((geo.n, d), F32)]
        out_specs = [pl.BlockSpec((TM, d), lambda i: (i, 0))]
    n_main = len(out_shape)
    out = pl.pallas_call(
        kern,
        out_shape=out_shape + c_shapes,
        grid=(steps,),
        in_specs=_stream_specs(geo, a, d) + _row_specs(geo, x) + [
            _resident((d, d)), _resident((1, d)), _resident((1, d)), _resident((1, d)),
            _resident((d, dff)), _resident((d, dff)), _resident((dff, d))] + c_in,
        out_specs=out_specs + c_out,
        scratch_shapes=[pltpu.VMEM((TM // TAIL_CHAINS, dff), BF16)] * TAIL_CHAINS,
        compiler_params=_params(("arbitrary",)),
        name="layer_tail",
    )(*_stream_args(a), *_stream_args(x), mod, gn, w_out, b_out.reshape(1, d), ln[0].reshape(1, d),
      ln[1].reshape(1, d), wg, wu, wd, *[w for w, _ in casts])
    main = tuple(out[:n_main]) if split else out[0]
    return main, list(out[n_main:])


DFT_ROWS = 256


def _dft_table_kernel(cb_ref, sb_ref, o_ref, *, seq_len):
    k0 = pl.program_id(0) * DFT_ROWS
    s = lax.broadcasted_iota(jnp.int32, (1, seq_len), 1)
    ang = ((k0 * s) & (2 * seq_len - 1)).astype(F32) * (math.pi / seq_len)
    ca, sa = jnp.cos(ang), jnp.sin(ang)
    cb, sb = cb_ref[...], sb_ref[...]
    o_ref[:, 0:seq_len] = (ca * cb - sa * sb).astype(BF16)
    o_ref[:, seq_len:2 * seq_len] = (sa * cb + ca * sb).astype(BF16)


def dft_tables(seq_len):
    assert seq_len & (seq_len - 1) == 0 and seq_len % DFT_ROWS == 0
    k = jnp.arange(seq_len, dtype=jnp.int32)
    idx = (k[:DFT_ROWS, None] * k[None, :]) % (2 * seq_len)
    ang = idx.astype(F32) * (math.pi / seq_len)
    sign = (1 - 2 * (k % 2)).astype(F32)[:, None]
    cvec = jnp.where(k == 0, 0.5 / seq_len, 1.0 / seq_len).astype(F32)[:, None]
    base = pl.BlockSpec((DFT_ROWS, seq_len), lambda i: (0, 0))
    cs = pl.pallas_call(
        functools.partial(_dft_table_kernel, seq_len=seq_len),
        out_shape=jax.ShapeDtypeStruct((seq_len, 2 * seq_len), BF16),
        grid=(seq_len // DFT_ROWS,),
        in_specs=[base, base],
        out_specs=pl.BlockSpec((DFT_ROWS, 2 * seq_len), lambda i: (i, 0)),
        compiler_params=_params(("parallel",)),
        name=f"dft_table_{seq_len}",
    )(jnp.cos(ang), jnp.sin(ang))
    return cs, sign, cvec


def _filter_kernel(zp_ref, zm_ref, tp_ref, tm_ref, keep_ref, w1_ref, b1_ref, fr_ref, w2_ref, b2_ref,
                   w3p_ref, w3m_ref, dl_ref, sign_ref, cvec_ref, cs_ref, hr_ref, hi_ref, hn_ref,
                   hp_ref, hm_ref):
    blk = zp_ref.shape[0]
    absdl = jnp.abs(dl_ref[...])

    @pl.when(pl.program_id(1) == 0)
    def _():
        fw = hp_ref.shape[1]
        pre = jnp.concatenate([_dot_hi(zp_ref[...], w1_ref[...]), _dot_hi(zm_ref[...], w1_ref[...])], axis=1)
        h = jnp.sin(fr_ref[0:1, :] * (pre + b1_ref[...]))
        h = jnp.sin(fr_ref[1:2, :] * (_dot_hi(h, w2_ref[...]) + b2_ref[...]))
        hp_ref[...] = h[:, 0:fw]
        hm_ref[...] = h[:, fw:2 * fw]

    def taps(h_ref, t_ref, w3_ref):
        return _dot(h_ref[...].astype(BF16), w3_ref[...].astype(BF16)) * jnp.exp(-t_ref[...] * absdl)

    gp = taps(hp_ref, tp_ref, w3p_ref)
    gm = taps(hm_ref, tm_ref, w3m_ref) * keep_ref[...]
    gs = gp + gm
    gd = gp - gm
    cv = cvec_ref[...]
    hr_ref[...] = cv * _dot(cs_ref[:, 0:blk], gs.astype(BF16))
    hi_ref[...] = cv * _dot(cs_ref[:, blk:2 * blk], gd.astype(BF16))
    hn_ref[...] = jnp.sum(gs * sign_ref[...], axis=0, keepdims=True) * (0.5 / blk)


def hyena_block_spectra(seq_len, blk, d, tables, w1, b1, freq, w2, b2, w3):
    cs, sign, cvec = tables
    emb, fw = w1.shape
    bands = (emb - 1) // 2
    nblk = seq_len // blk
    ndl = 2 * nblk - 1
    dd = jnp.arange(ndl, dtype=jnp.int32)[:, None] - (nblk - 1)
    m = jnp.arange(blk, dtype=jnp.int32)[None, :]
    pos_p = jnp.where(dd >= 0, dd * blk + m, -dd * blk - m)
    pos_m = jnp.where(dd >= 1, dd * blk - m, -dd * blk + m)
    keep = jnp.where((m == 0) & (dd != 0), 0.0, 1.0).astype(F32)

    def features(pos):
        pos = pos.reshape(-1, 1).astype(F32)
        t = pos / (seq_len - 1)
        w = 2.0 * math.pi * pos / seq_len
        f = jnp.linspace(1e-4, bands - 1, bands, dtype=F32)[None, :]
        z = jnp.concatenate([t, jnp.cos(f * w), -jnp.sin(f * w)], axis=-1)
        return jnp.pad(z, ((0, 0), (0, LANE - emb))), t

    zp, tp = features(pos_p)
    zm, tm = features(pos_m)
    w1p = jnp.pad(w1, ((0, LANE - emb), (0, 0)))
    zeros = jnp.zeros_like(w2)
    w2_pair = jnp.concatenate([jnp.concatenate([w2, zeros], axis=1),
                               jnp.concatenate([zeros, w2], axis=1)], axis=0)
    deltas = jnp.linspace(math.log(HY_DECAY_TARGET) / HY_SLOW_PCT,
                          math.log(HY_DECAY_TARGET) / HY_FAST_PCT, d, dtype=F32)[None, :]
    fdc = 2 * DC
    nd = d // fdc
    full = lambda shape: pl.BlockSpec(shape, lambda i, j: (0,) * len(shape))
    rows = lambda width: pl.BlockSpec((blk, width), lambda i, j: (i, 0))
    w3_p = pl.BlockSpec((fw, fdc), lambda i, j: (0, jnp.where(i >= nblk - 1, 0, 2 * nd) + j))
    w3_m = pl.BlockSpec((fw, fdc), lambda i, j: (0, jnp.where(i >= nblk, 0, 2 * nd) + j))
    spec_out = pl.BlockSpec((blk, fdc), lambda i, j: (i, j))
    return pl.pallas_call(
        _filter_kernel,
        out_shape=(jax.ShapeDtypeStruct((ndl * blk, 2 * d), F32),
                   jax.ShapeDtypeStruct((ndl * blk, 2 * d), F32),
                   jax.ShapeDtypeStruct((ndl, 1, 2 * d), F32)),
        grid=(ndl, 2 * nd),
        in_specs=[rows(LANE), rows(LANE), rows(1), rows(1), rows(1),
                  full((LANE, fw)), full((1, 2 * fw)), full((2, 2 * fw)), full((2 * fw, 2 * fw)),
                  full((1, 2 * fw)),
                  w3_p, w3_m,
                  pl.BlockSpec((1, fdc), lambda i, j: (0, j % nd)),
                  full((blk, 1)), full((blk, 1)), full((blk, 2 * blk))],
        out_specs=(spec_out, spec_out, pl.BlockSpec((None, 1, fdc), lambda i, j: (i, 0, j))),
        scratch_shapes=[pltpu.VMEM((blk, fw), F32), pltpu.VMEM((blk, fw), F32)],
        compiler_params=_params(("parallel", "arbitrary")),
        name=f"hyena_filter_{seq_len}",
    )(zp, zm, tp, tm, keep.reshape(-1, 1), w1p, jnp.tile(b1.reshape(1, fw), (1, 2)), jnp.tile(freq, (1, 2)),
      w2_pair, jnp.tile(b2.reshape(1, fw), (1, 2)), w3, w3, deltas, sign, cvec, cs)


HY_GROUP = 2
HY_BLOCK = 512
HY_STRIP = 16
HY_HALO = 16


def _dwconv3_rows(p_ref, base, r, nrows, seq_len, w_ref, b_ref):
    lo, hi = max(r - HY_HALO, 0), min(r + nrows + HY_HALO, seq_len)
    win = p_ref[base + lo:base + hi, :].astype(F32)
    n = hi - lo
    prev = pltpu.roll(win, 1, 0)
    nxt = pltpu.roll(win, n - 1, 0)
    rows = lax.broadcasted_iota(jnp.int32, (n, 1), 0)
    if lo == 0:
        prev = jnp.where(rows == 0, 0.0, prev)
    if hi == seq_len:
        nxt = jnp.where(rows == n - 1, 0.0, nxt)
    y = w_ref[0:1, :] * prev + w_ref[1:2, :] * win + w_ref[2:3, :] * nxt + b_ref[...]
    return y[r - lo:r - lo + nrows, :]


def _hyena_mix_kernel(pv_ref, p1_ref, p2_ref, wv_ref, w1_ref, w2_ref, bv_ref, b1_ref, b2_ref,
                      cs_ref, hr1_ref, hi1_ref, hn1_ref, hr2_ref, hi2_ref, hn2_ref, skip_ref,
                      o_ref, z_ref, zb_ref, fa_ref, fb_ref, y_ref, *, seq_len, blk):
    dc = pv_ref.shape[1]
    groups = pv_ref.shape[0] // seq_len
    nblk = seq_len // blk
    tile = lambda a: jnp.concatenate([a] * groups, axis=1)
    rows = lax.broadcasted_iota(jnp.int32, (blk, 1), 0)
    sign = (1 - 2 * (rows % 2)).astype(F32)

    def stage(r, val):
        z_ref[r:r + blk, :] = val
        zb_ref[r:r + blk, :] = val.astype(BF16)
        return jnp.sum(val * sign, axis=0, keepdims=True)

    def block_products(i_out, hr_ref, hi_ref):
        def strip(r0):
            yr, yi = [None] * groups, [None] * groups
            for j in range(nblk):
                h0 = (i_out - j + nblk - 1) * blk
                hr = hr_ref[h0 + r0:h0 + r0 + HY_STRIP, :]
                hi = hi_ref[h0 + r0:h0 + r0 + HY_STRIP, :]
                for g in range(groups):
                    a = fa_ref[j * blk + r0:j * blk + r0 + HY_STRIP, g * dc:(g + 1) * dc]
                    b = fb_ref[j * blk + r0:j * blk + r0 + HY_STRIP, g * dc:(g + 1) * dc]
                    tr, ti = a * hr - b * hi, a * hi + b * hr
                    yr[g], yi[g] = (tr, ti) if yr[g] is None else (yr[g] + tr, yi[g] + ti)
            for g in range(groups):
                y_ref[r0:r0 + HY_STRIP, g * dc:(g + 1) * dc] = yr[g].astype(BF16)
                y_ref[blk + r0:blk + r0 + HY_STRIP, g * dc:(g + 1) * dc] = yi[g].astype(BF16)

        for r0 in range(0, blk, HY_STRIP):
            strip(r0)

    def transform(hr_ref, hi_ref, hn_ref, skip, mult_ref, mw_ref, mb_ref, nyqs, emit):
        for j in range(nblk):
            r = j * blk
            fa_ref[r:r + blk, :] = _dot(cs_ref[:, 0:blk], zb_ref[r:r + blk, :])
            fb_ref[r:r + blk, :] = _dot(cs_ref[:, blk:2 * blk], zb_ref[r:r + blk, :])
        skip_t = tile(skip)
        out = []
        for i in range(nblk):
            r = i * blk
            block_products(i, hr_ref, hi_ref)
            nyq = None
            for j in range(nblk):
                term = nyqs[j] * tile(hn_ref[i - j + nblk - 1])
                nyq = term if nyq is None else nyq + term
            y = _dot(cs_ref[...], y_ref[...]) + sign * nyq + z_ref[r:r + blk, :] * skip_t
            mult = jnp.concatenate([_dwconv3_rows(mult_ref, g * seq_len, r, blk, seq_len, mw_ref, mb_ref)
                                    for g in range(groups)], axis=1)
            out.append(emit(r, mult * y))
        return out

    nyqs = []
    for j in range(nblk):
        v = jnp.concatenate([_dwconv3_rows(pv_ref, g * seq_len, j * blk, blk, seq_len, wv_ref, bv_ref)
                             for g in range(groups)], axis=1)
        nyqs.append(stage(j * blk, v))

    nyqs = transform(hr1_ref, hi1_ref, hn1_ref, skip_ref[0:1, :], p1_ref, w1_ref, b1_ref, nyqs, stage)

    def write_out(r, val):
        for g in range(groups):
            o_ref[g * seq_len + r:g * seq_len + r + blk, :] = val[:, g * dc:(g + 1) * dc].astype(BF16)

    transform(hr2_ref, hi2_ref, hn2_ref, skip_ref[1:2, :], p2_ref, w2_ref, b2_ref, nyqs, write_out)


def hyena_mix(geo, p, conv_w, conv_b, skip, tables, spectrum, seq_len, row0, nb):
    d = geo.d
    cs = tables[0]
    hr, hi, hn = spectrum
    blk = cs.shape[0]
    ndl = 2 * (seq_len // blk) - 1
    nd = d // DC
    group = max(HY_GROUP, 2048 // seq_len)
    rows = group * seq_len
    width = group * DC
    assert nb % group == 0 and row0 % rows == 0 and seq_len % blk == 0 and blk % HY_STRIP == 0
    blk0 = row0 // rows
    pspec = lambda part: pl.BlockSpec((rows, DC), lambda c, b: (blk0 + b, part * nd + c))
    wspec = lambda part: pl.BlockSpec((3, DC), lambda c, b: (0, part * nd + c))
    bspec = lambda part: pl.BlockSpec((1, DC), lambda c, b: (0, part * nd + c))
    hspec = lambda order: pl.BlockSpec((ndl * blk, DC), lambda c, b: (0, order * nd + c),
                                       pipeline_mode=pl.Buffered(1))
    nspec = lambda order: pl.BlockSpec((ndl, 1, DC), lambda c, b: (0, 0, order * nd + c))
    return pl.pallas_call(
        functools.partial(_hyena_mix_kernel, seq_len=seq_len, blk=blk),
        out_shape=jax.ShapeDtypeStruct((nb * seq_len, d), BF16),
        grid=(nd, nb // group),
        in_specs=[pspec(0), pspec(1), pspec(2), wspec(0), wspec(1), wspec(2), bspec(0), bspec(1), bspec(2),
                  _resident((blk, 2 * blk)),
                  hspec(0), hspec(0), nspec(0), hspec(1), hspec(1), nspec(1),
                  pl.BlockSpec((2, DC), lambda c, b: (0, c))],
        out_specs=pl.BlockSpec((rows, DC), lambda c, b: (b, c)),
        scratch_shapes=[pltpu.VMEM((seq_len, width), F32), pltpu.VMEM((seq_len, width), BF16),
                        pltpu.VMEM((seq_len, width), F32), pltpu.VMEM((seq_len, width), F32),
                        pltpu.VMEM((2 * blk, width), BF16)],
        compiler_params=_params(("arbitrary", "arbitrary")),
        name=f"hyena_mix_{seq_len}",
    )(p, p, p, conv_w, conv_w, conv_w, conv_b, conv_b, conv_b, cs,
      hr, hi, hn, hr, hi, hn, skip)


CONV_ROWS = 64
CONV_PAD = 16


def _stage_padded(src, pad_ref):
    n = src.shape[0]
    zeros = jnp.zeros((CONV_PAD, pad_ref.shape[1]), F32)
    pad_ref[0:CONV_PAD, :] = zeros
    pad_ref[CONV_PAD + n:2 * CONV_PAD + n, :] = zeros
    pad_ref[CONV_PAD:CONV_PAD + n, :] = src


def _dwconv_chunks(pad_ref, w_ref, n, ktaps, emit):
    half = ktaps // 2
    lo = (CONV_PAD - half) // SUBLANES * SUBLANES
    hi = -(-(CONV_PAD + half + CONV_ROWS) // SUBLANES) * SUBLANES
    rows = hi - lo

    def chunk(ci, carry):
        r0 = pl.multiple_of(ci * CONV_ROWS, CONV_ROWS)
        window = pad_ref[pl.ds(pl.multiple_of(r0 + lo, SUBLANES), rows), :]
        acc = None
        for k in range(ktaps):
            off = CONV_PAD - half + k - lo
            win = window if off == 0 else pltpu.roll(window, rows - off, 0)
            term = w_ref[k:k + 1, :] * win[0:CONV_ROWS, :]
            acc = term if acc is None else acc + term
        emit(r0, acc)
        return carry

    lax.fori_loop(0, n // CONV_ROWS, chunk, 0)


def _conv_group(seq_len, nb):
    group = max(1, min(nb, 1024 // seq_len))
    assert nb % group == 0
    return group


def _cf_conv_kernel(u_ref, w_ref, b_ref, o_ref, pad_ref, *, seq_len):
    bias = b_ref[...]
    for base in range(0, u_ref.shape[0], seq_len):
        _stage_padded(u_ref[base:base + seq_len, :].astype(F32), pad_ref)

        def emit(r0, acc, base=base):
            o_ref[pl.ds(base + r0, CONV_ROWS), :] = (acc + bias).astype(BF16)

        _dwconv_chunks(pad_ref, w_ref, seq_len, w_ref.shape[0], emit)


def conformer_dwconv(geo, u, w, b, seq_len, row0, nb):
    d = geo.d
    dc = 256
    ktaps = w.shape[0]
    rows = _conv_group(seq_len, nb) * seq_len
    blk0 = row0 // rows
    return pl.pallas_call(
        functools.partial(_cf_conv_kernel, seq_len=seq_len),
        out_shape=jax.ShapeDtypeStruct((nb * seq_len, d), BF16),
        grid=(nb * seq_len // rows, d // dc),
        in_specs=[pl.BlockSpec((rows, dc), lambda b_, c: (blk0 + b_, c)),
                  pl.BlockSpec((ktaps, dc), lambda b_, c: (0, c)),
                  pl.BlockSpec((1, dc), lambda b_, c: (0, c))],
        out_specs=pl.BlockSpec((rows, dc), lambda b_, c: (b_, c)),
        scratch_shapes=[pltpu.VMEM((seq_len + 2 * CONV_PAD, dc), F32)],
        compiler_params=_params(("parallel", "parallel")),
        name=f"conformer_dwconv_{seq_len}",
    )(u, w, b.reshape(1, d))


def _sc_conv_kernel(q_ref, bg_ref, w_ref, o_ref, pad_ref, *, seq_len):
    for base in range(0, q_ref.shape[0], seq_len):
        _stage_padded(q_ref[base:base + seq_len, :].astype(F32), pad_ref)

        def emit(r0, acc, base=base):
            gate = bg_ref[pl.ds(base + r0, CONV_ROWS), :].astype(F32)
            o_ref[pl.ds(base + r0, CONV_ROWS), :] = (gate * acc).astype(BF16)

        _dwconv_chunks(pad_ref, w_ref, seq_len, w_ref.shape[0], emit)


def shortconv_gate(geo, q, bg, w, seq_len, row0, nb):
    d = geo.d
    dc = 256
    ktaps = w.shape[0]
    rows = _conv_group(seq_len, nb) * seq_len
    blk0 = row0 // rows
    seq = pl.BlockSpec((rows, dc), lambda b_, c: (blk0 + b_, c))
    return pl.pallas_call(
        functools.partial(_sc_conv_kernel, seq_len=seq_len),
        out_shape=jax.ShapeDtypeStruct((nb * seq_len, d), BF16),
        grid=(nb * seq_len // rows, d // dc),
        in_specs=[seq, seq, pl.BlockSpec((ktaps, dc), lambda b_, c: (0, c))],
        out_specs=pl.BlockSpec((rows, dc), lambda b_, c: (b_, c)),
        scratch_shapes=[pltpu.VMEM((seq_len + 2 * CONV_PAD, dc), F32)],
        compiler_params=_params(("parallel", "parallel")),
        name=f"shortconv_gate_{seq_len}",
    )(q, bg, w)


def _mla_proj_kernel(x, mod_ref, gn_ref, wdq_ref, gq_ref, wun_ref, wup_ref, wus_ref,
                     wkc_ref, wkp_ref, wks_ref, gkv_ref, cosq_ref, sinq_ref,
                     q_ref, ckv_ref, kpe_ref, *, heads, nope, rope, qscale):
    h = _prenorm(x, mod_ref, gn_ref, 0)
    cos2, sin2 = cosq_ref[...], sinq_ref[...]
    cq_raw = _dot(h, wdq_ref[...])
    ckv_raw = _dot(h, wkc_ref[...])
    kpe = _dot(h, wkp_ref[...]) * cos2[:, 0:rope] + _dot(h, wks_ref[...]) * sin2[:, 0:rope]
    cq = (_rms(cq_raw) * gq_ref[...]).astype(BF16)
    qn = _dot(cq, wun_ref[...]) * qscale
    cosq, sinq = cos2 * qscale, sin2 * qscale
    for pair in range(heads // 2):
        lo = pair * 2 * rope
        qp = (_dot(cq, wup_ref[:, lo:lo + 2 * rope]) * cosq
              + _dot(cq, wus_ref[:, lo:lo + 2 * rope]) * sinq)
        for j in range(2):
            hd = 2 * pair + j
            q_ref[hd, :, 0:nope] = qn[:, hd * nope:(hd + 1) * nope].astype(BF16)
            q_ref[hd, :, nope:nope + rope] = qp[:, j * rope:(j + 1) * rope].astype(BF16)
    ckv_ref[...] = _rms(ckv_raw) * gkv_ref[...]
    kpe_ref[...] = kpe


def mla_project(geo, x, mod, gn, wdq, gq, wun, wup, wus, wkc, wkp, wks, gkv, cos2, sin2, heads, nope, rope,
                qscale):
    d = geo.d
    qr, kvr = wdq.shape[1], wkc.shape[1]
    row = lambda w: pl.BlockSpec((TM, w), lambda i: (i, 0))
    rot = pl.BlockSpec((TM, 2 * rope), lambda i: (_rope_block_of_tile(geo, i), 0))
    body = functools.partial(_mla_proj_kernel, heads=heads, nope=nope, rope=rope, qscale=qscale)
    return pl.pallas_call(
        _row_kernel(body, geo, x),
        out_shape=(jax.ShapeDtypeStruct((heads, geo.n, nope + rope), BF16),
                   jax.ShapeDtypeStruct((geo.n, kvr), F32),
                   jax.ShapeDtypeStruct((geo.n, rope), F32)),
        grid=(geo.n // TM,),
        in_specs=_row_specs(geo, x) + [_resident((d, qr)), _resident((1, qr)),
                                    _resident((qr, heads * nope)), _resident((qr, heads * rope)),
                                    _resident((qr, heads * rope)),
                                    _resident((d, kvr)), _resident((d, rope)), _resident((d, rope)),
                                    _resident((1, kvr)), rot, rot],
        out_specs=(pl.BlockSpec((heads, TM, nope + rope), lambda i: (0, i, 0)), row(kvr), row(rope)),
        compiler_params=_params(("parallel",)),
        name="mla_project",
    )(*_stream_args(x), mod, gn, wdq, gq.reshape(1, qr), wun, wup, wus, wkc, wkp, wks, gkv.reshape(1, kvr),
      cos2, sin2)


KV_ROWS = 256


def _mla_kv_kernel(*refs, heads, nope, rope, past_tiles, sub):
    wn_ref, wv_ref, k_ref, v_ref = refs[-4:]
    new = refs[-4 - 2 * sub:-4]
    for u in range(sub):
        rows = slice(u * KV_ROWS, (u + 1) * KV_ROWS)
        ckv, kpe = new[2 * u][...], new[2 * u + 1][...]
        if past_tiles:
            cached = pl.program_id(1) * sub + u < past_tiles
            ckv = jnp.where(cached, refs[0][...], ckv)
            kpe = jnp.where(cached, refs[1][...], kpe)
        c = ckv.astype(BF16)
        kn = _dot(c, wn_ref[...])
        v_ref[rows, :] = _dot(c, wv_ref[...]).astype(BF16)
        kp = kpe.astype(BF16)
        for hd in range(heads):
            k_ref[hd, rows, 0:nope] = kn[:, hd * nope:(hd + 1) * nope].astype(BF16)
            k_ref[hd, rows, nope:nope + rope] = kp


def mla_expand_kv(ckv, kpe, cache, layer, seq_len, row0, nb, wn, wv, heads, nope, rope, vdim):
    r = ckv.shape[1]
    past = 0 if cache is None else cache[0].shape[2]
    assert past % KV_ROWS == 0 and seq_len % KV_ROWS == 0 and row0 % KV_ROWS == 0
    pt, st, blk0 = past // KV_ROWS, seq_len // KV_ROWS, row0 // KV_ROWS
    lk = past + seq_len
    tiles = lk // KV_ROWS
    sub = 3 if (tiles % 3 == 0 and pt <= 1) else 1
    specs, args = [], []
    for u in range(sub):
        new_map = lambda b, t, u=u: (blk0 + b * st + jnp.maximum(t * sub + u - pt, 0), 0)
        specs += [pl.BlockSpec((KV_ROWS, r), new_map), pl.BlockSpec((KV_ROWS, rope), new_map)]
        args += [ckv, kpe]
    specs += [_resident((r, heads * nope)), _resident((r, heads * vdim))]
    args += [wn, wv]
    if cache is not None:
        old_map = lambda b, t: (b, layer, jnp.minimum(t * sub, pt - 1), 0)
        specs = [pl.BlockSpec((None, None, KV_ROWS, r), old_map),
                 pl.BlockSpec((None, None, KV_ROWS, rope), old_map)] + specs
        args = list(cache) + args
    return pl.pallas_call(
        functools.partial(_mla_kv_kernel, heads=heads, nope=nope, rope=rope, past_tiles=pt, sub=sub),
        out_shape=(jax.ShapeDtypeStruct((nb, heads, lk, nope + rope), BF16),
                   jax.ShapeDtypeStruct((nb, lk, heads * vdim), BF16)),
        grid=(nb, tiles // sub),
        in_specs=specs,
        out_specs=(pl.BlockSpec((None, heads, sub * KV_ROWS, nope + rope), lambda b, t: (b, 0, t, 0)),
                   pl.BlockSpec((None, sub * KV_ROWS, heads * vdim), lambda b, t: (b, t, 0))),
        compiler_params=_params(("parallel", "arbitrary")),
        name=f"mla_expand_kv_{lk}",
    )(*args)


ATTN_ROWS = 256


def _attn_kernel(q_ref, k_ref, v_ref, o_ref, *, vdim):
    hg, tq, _ = q_ref.shape
    chains = [(h, r) for h in range(hg) for r in range(0, tq, ATTN_ROWS)]

    def scores(h, r):
        return lax.dot_general(q_ref[h, r:r + ATTN_ROWS, :], k_ref[h], (((1,), (1,)), ((), ())),
                               preferred_element_type=F32)

    s_next = scores(*chains[0])
    for i, (h, r) in enumerate(chains):
        s = s_next
        if i + 1 < len(chains):
            s_next = scores(*chains[i + 1])
        m = jnp.max(s, axis=-1, keepdims=True)
        p = jnp.exp2(s - m)
        l = jnp.sum(p, axis=-1, keepdims=True)
        o = _dot(p.astype(BF16), v_ref[:, h * vdim:(h + 1) * vdim])
        o_ref[r:r + ATTN_ROWS, h * vdim:(h + 1) * vdim] = (o * (1.0 / l)).astype(BF16)


def mla_attention(q, k, v, seq_len, row0, nb, vdim):
    heads, _, dq = q.shape
    lk = k.shape[2]
    hg = max(2, min(heads, 2048 // seq_len))
    blk0 = row0 // seq_len
    return pl.pallas_call(
        functools.partial(_attn_kernel, vdim=vdim),
        out_shape=jax.ShapeDtypeStruct((nb * seq_len, heads * vdim), BF16),
        grid=(nb, heads // hg),
        in_specs=[pl.BlockSpec((hg, seq_len, dq), lambda b, h: (h, blk0 + b, 0)),
                  pl.BlockSpec((None, hg, lk, dq), lambda b, h: (b, h, 0, 0)),
                  pl.BlockSpec((None, lk, hg * vdim), lambda b, h: (b, 0, h))],
        out_specs=pl.BlockSpec((seq_len, hg * vdim), lambda b, h: (b, h)),
        compiler_params=_params(("parallel", "parallel")),
        name=f"mla_attention_{seq_len}",
    )(q, k, v)


def rope_tables(geo, rope):
    nf = rope // 4
    rows = geo.l_lat // GRID_W
    row = jnp.repeat(jnp.arange(rows, dtype=F32), GRID_W)
    col = jnp.tile(jnp.arange(GRID_W, dtype=F32), rows)
    inv = jnp.exp(-math.log(ROPE_THETA) * jnp.arange(nf, dtype=F32) * (4.0 / rope))
    ar, ac = row[:, None] * inv, col[:, None] * inv
    cos = jnp.concatenate([jnp.cos(ar), jnp.cos(ar), jnp.cos(ac), jnp.cos(ac)], axis=-1)
    sin = jnp.concatenate([-jnp.sin(ar), jnp.sin(ar), -jnp.sin(ac), jnp.sin(ac)], axis=-1)
    cos = jnp.concatenate([jnp.ones((TM, rope), F32), cos], axis=0)
    sin = jnp.concatenate([jnp.zeros((TM, rope), F32), sin], axis=0)
    return jnp.tile(cos, (1, 2)), jnp.tile(sin, (1, 2))


def _rope_block_of_tile(geo, i):
    nct, per_seq = geo.n_ctx // TM, geo.l_lat // TM
    return jnp.where(i < nct, 0, 1 + (i - nct) % per_seq)


def _swap_rope_halves(w, rope):
    nf = rope // 4
    shape = w.shape
    w = w.reshape(shape[:-1] + (shape[-1] // (2 * nf), 2, nf))
    return w[..., ::-1, :].reshape(shape)


def kernel(x_prompt, x_sample, c, cache_ckv, cache_kpe, c_ctx, w_mod, b_mod, norm_g, ffn_w_gate, ffn_w_up, ffn_w_down, hy_w_in, hy_b_in, hy_conv_w, hy_conv_b, hy_f_w1, hy_f_b1, hy_f_freq, hy_f_w2, hy_f_b2, hy_f_w3, hy_skip, hy_w_out, hy_b_out, cf_w_pw1, cf_b_pw1, cf_dw_w, cf_dw_b, cf_ln_g, cf_ln_b, cf_w_pw2, cf_b_pw2, sc_w_in, sc_conv_w, sc_w_out, mla_w_dq, mla_g_q, mla_w_uq, mla_w_dkv, mla_g_kv, mla_w_ukv, mla_w_o):
    nb_ctx, l_ctx, d = x_prompt.shape
    nb_lat, l_lat, _ = x_sample.shape
    geo = Geometry(nb_ctx, l_ctx, nb_lat, l_lat, d)
    depth = w_mod.shape[0]
    n_mixers = 4

    x = (x_prompt.reshape(geo.n_ctx, d), x_sample.reshape(geo.n_lat, d))
    cond = jnp.concatenate([c_ctx[None, :], c], axis=0)
    n_cond = cond.shape[0]
    cond = jnp.pad(cond, ((0, -n_cond % 8), (0, 0)))
    mod_all = modulation_all(cond, w_mod, b_mod)[:, :n_cond].reshape(depth, n_cond, 6, d)

    passes = ((l_ctx, 0, nb_ctx), (l_lat, geo.n_ctx, nb_lat))
    zero_bias = jnp.zeros((d,), F32)
    ffn_stacks = (ffn_w_gate, ffn_w_up, ffn_w_down)
    mixer_proj = {0: (hy_w_in, hy_w_out), 1: (cf_w_pw1, cf_w_pw2), 2: (sc_w_in, sc_w_out), 3: (mla_w_dq, mla_w_o)}
    new_ckv, new_kpe = [], []

    def tail_casts(layer):
        w_in_, w_out_ = mixer_proj[layer % n_mixers]
        return [(w, layer) for w in ffn_stacks] + [(w_in_, layer // n_mixers), (w_out_, layer // n_mixers)]

    ready = {}

    for i in range(depth):
        m, j = i % n_mixers, i // n_mixers
        mod, gn = mod_all[i], norm_g[i]
        w_in = ready.get("w_in")
        if w_in is None:
            w_in = mixer_proj[m][0][j].astype(BF16)
        if m == 0:
            own = [] if ready else tail_casts(i)[:3] + tail_casts(i)[4:]
            p, cast = hyena_in_proj(geo, x, mod, gn, w_in, hy_b_in[j], own)
            if own:
                ready = dict(ffn=cast[:3], w_out=cast[3])
            parts = []
            for seq_len, row0, nb in passes:
                blk = min(seq_len, HY_BLOCK)
                tables = dft_tables(blk)
                spectrum = hyena_block_spectra(seq_len, blk, d, tables, hy_f_w1[j], hy_f_b1[j], hy_f_freq[j],
                                               hy_f_w2[j], hy_f_b2[j], hy_f_w3[j])
                parts.append(hyena_mix(geo, p, hy_conv_w[j], hy_conv_b[j].reshape(1, -1), hy_skip[j],
                                       tables, spectrum, seq_len, row0, nb))
            a, w_out, b_out, ln = tuple(parts), hy_w_out[j], hy_b_out[j], None
        elif m == 1:
            u = conformer_in_proj(geo, x, mod, gn, w_in, cf_b_pw1[j])
            a = tuple(conformer_dwconv(geo, u, cf_dw_w[j], cf_dw_b[j], seq_len, row0, nb)
                      for seq_len, row0, nb in passes)
            w_out, b_out, ln = cf_w_pw2[j], cf_b_pw2[j], (cf_ln_g[j], cf_ln_b[j])
        elif m == 2:
            bg, q = shortconv_in_proj(geo, x, mod, gn, w_in)
            a = tuple(shortconv_gate(geo, q, bg, sc_conv_w[j], seq_len, row0, nb)
                      for seq_len, row0, nb in passes)
            w_out, b_out, ln = sc_w_out[j], zero_bias, None
        else:
            kvr = mla_g_kv.shape[1]
            rope = mla_w_dkv.shape[2] - kvr
            vdim = 128
            heads = mla_w_o.shape[1] // vdim
            nope = mla_w_uq.shape[2] // heads - rope
            wuq = mla_w_uq[j].reshape(-1, heads, nope + rope)
            wun = wuq[:, :, :nope].reshape(-1, heads * nope).astype(BF16)
            wup = wuq[:, :, nope:].reshape(-1, heads * rope)
            wukv = mla_w_ukv[j].reshape(kvr, heads, nope + vdim)
            wkn = wukv[:, :, :nope].reshape(kvr, heads * nope).astype(BF16)
            wkv = wukv[:, :, nope:].reshape(kvr, heads * vdim).astype(BF16)
            wkc, wkp = mla_w_dkv[j][:, :kvr], mla_w_dkv[j][:, kvr:]
            cos2, sin2 = rope_tables(geo, rope)
            q, ckv, kpe = mla_project(geo, x, mod, gn, w_in, mla_g_q[j], wun,
                                      wup.astype(BF16), _swap_rope_halves(wup, rope).astype(BF16),
                                      wkc.astype(BF16), wkp.astype(BF16),
                                      _swap_rope_halves(wkp, rope).astype(BF16), mla_g_kv[j],
                                      cos2, sin2, heads, nope, rope,
                                      (nope + rope) ** -0.5 * math.log2(math.e))
            new_ckv.append(ckv[:geo.n_ctx].reshape(nb_ctx, l_ctx, kvr))
            new_kpe.append(kpe[:geo.n_ctx].reshape(nb_ctx, l_ctx, rope))
            parts = []
            for (seq_len, row0, nb), cache in zip(passes, (None, (cache_ckv, cache_kpe))):
                kk, vv = mla_expand_kv(ckv, kpe, cache, j, seq_len, row0, nb, wkn, wkv, heads, nope, rope, vdim)
                parts.append(mla_attention(q, kk, vv, seq_len, row0, nb, vdim))
            a, w_out, b_out, ln = tuple(parts), mla_w_o[j], zero_bias, None
        ffn = ready.get("ffn") or [w[i].astype(BF16) for w in ffn_stacks]
        w_out = ready["w_out"] if "w_out" in ready else w_out.astype(BF16)
        last = i == depth - 1
        x, cast = layer_tail(geo, a, x, mod, gn, w_out, b_out, ln, *ffn, split=last,
                             casts=[] if last else tail_casts(i + 1))
        ready = {} if last else dict(ffn=cast[:3], w_in=cast[3], w_out=cast[4])

    y_prompt = x[0].reshape(nb_ctx, l_ctx, d)
    y_sample = x[1].reshape(nb_lat, l_lat, d)
    return (y_prompt, y_sample, jnp.stack(new_ckv, axis=1), jnp.stack(new_kpe, axis=1))
```
